```python
import math
import jax, jax.numpy as jnp
from jax import lax
import numpy as np

D_MODEL = 1024
BATCH = 4
SEQ = 4096
DEPTH = 4

GRID_W = 64
CTX_LEN = 256
N_MIXERS = 3
D_FF = 4 * D_MODEL
EPS = 1e-6
ROPE_THETA = 10000.0

MLA_HEADS = 8
MLA_NOPE = 128
MLA_ROPE = 64
MLA_V = 128
MLA_Q_LORA = 384
MLA_KV_LORA = 256
MLA_SCALE = 1.0 / math.sqrt(MLA_NOPE + MLA_ROPE)
Q_BLOCK = 128

S5_GROUP = 16
S5_GROUPS = D_MODEL // S5_GROUP
S5_STATE = 64
S5_DT_MIN = 1e-3
S5_DT_MAX = 1e-1

LRU_WIDTH = 5 * D_MODEL // 4
LRU_BLOCKS = 10
LRU_BW = LRU_WIDTH // LRU_BLOCKS
LRU_C = 8.0
CONV_W = 4
CONV_LEFT = 1

N_A = (DEPTH + 2) // 3
N_B = (DEPTH + 1) // 3
N_C = DEPTH // 3

kernel_name = 'hybrid_mla_s5_rglru_dit_block'


def rmsnorm(x, g):
    xf = x.astype(jnp.float32)
    inv = lax.rsqrt(jnp.mean(xf * xf, axis=-1, keepdims=True) + EPS)
    return (xf * inv * g.astype(jnp.float32)).astype(x.dtype)


def modulate(x, shift, scale):
    return x * (1 + scale) + shift


def sq_relu_mlp(h, w1, w2):
    return jnp.square(jax.nn.relu(h @ w1)) @ w2


def axial_rope_tables(n_tokens):
    rows = n_tokens // GRID_W
    row = jnp.repeat(jnp.arange(rows, dtype=jnp.float32), GRID_W)
    col = jnp.tile(jnp.arange(GRID_W, dtype=jnp.float32), rows)
    n_freq = MLA_ROPE // 4
    freqs = ROPE_THETA ** (-jnp.arange(n_freq, dtype=jnp.float32) / n_freq)
    ang_r = row[:, None] * freqs[None, :]
    ang_c = col[:, None] * freqs[None, :]
    ang = jnp.concatenate([ang_r, ang_r, ang_c, ang_c], axis=-1)
    return jnp.cos(ang), jnp.sin(ang)


def apply_axial_rope(x, cos, sin):
    x1, x2, x3, x4 = jnp.split(x, 4, axis=-1)
    rot = jnp.concatenate([-x2, x1, -x4, x3], axis=-1)
    return x * cos + rot * sin


def mla_queries(h, w_dq, g_q, w_uq, g_qk):
    bsz, n, _ = h.shape
    q = (rmsnorm(h @ w_dq, g_q) @ w_uq).reshape(bsz, n, MLA_HEADS, MLA_NOPE + MLA_ROPE)
    q_nope = rmsnorm(q[..., :MLA_NOPE], g_qk[0, :MLA_NOPE])
    q_rope = rmsnorm(q[..., MLA_NOPE:], g_qk[0, MLA_NOPE:])
    return q_nope, q_rope


def mla_keys(h, w_dkv, g_kv, w_ukv, g_qk):
    bsz, n, _ = h.shape
    kv = h @ w_dkv
    c_kv = rmsnorm(kv[..., :MLA_KV_LORA], g_kv)
    k_rope = rmsnorm(kv[..., MLA_KV_LORA:], g_qk[1, MLA_NOPE:])
    kvu = (c_kv @ w_ukv).reshape(bsz, n, MLA_HEADS, MLA_NOPE + MLA_V)
    k_nope = rmsnorm(kvu[..., :MLA_NOPE], g_qk[1, :MLA_NOPE])
    v = kvu[..., MLA_NOPE:]
    return k_nope, k_rope, v


def mla_attend(q_nope, q_rope, k_nope, k_rope, v):
    s = (jnp.einsum('bqhd,bkhd->bhqk', q_nope, k_nope)
         + jnp.einsum('bqhr,bkr->bhqk', q_rope, k_rope))
    p = jax.nn.softmax(s.astype(jnp.float32) * MLA_SCALE, axis=-1).astype(v.dtype)
    return jnp.einsum('bhqk,bkhd->bqhd', p, v)


def mla_mixer(h, hc, cos, sin, w_dq, g_q, w_uq, w_dkv, g_kv, w_ukv, g_qk, w_o, need_ctx):
    bsz, n_tok, _ = h.shape
    n_ctx = hc.shape[1]
    qn, qr = mla_queries(h, w_dq, g_q, w_uq, g_qk)
    qr = apply_axial_rope(qr, cos[:, None, :], sin[:, None, :])
    kn, kr, v = mla_keys(h, w_dkv, g_kv, w_ukv, g_qk)
    kr = apply_axial_rope(kr, cos, sin)
    ckn, ckr, cv = mla_keys(hc, w_dkv, g_kv, w_ukv, g_qk)
    kn_all = jnp.concatenate([ckn, kn], axis=1)
    kr_all = jnp.concatenate([ckr, kr], axis=1)
    v_all = jnp.concatenate([cv, v], axis=1)
    n_blk = n_tok // Q_BLOCK

    def to_blocks(t):
        return t.reshape((bsz, n_blk, Q_BLOCK) + t.shape[2:]).swapaxes(0, 1)

    o = lax.map(lambda qs: mla_attend(qs[0], qs[1], kn_all, kr_all, v_all),
                (to_blocks(qn), to_blocks(qr)))
    o = o.swapaxes(0, 1).reshape(bsz, n_tok, MLA_HEADS * MLA_V)
    y = o @ w_o
    yc = None
    if need_ctx:
        cqn, cqr = mla_queries(hc, w_dq, g_q, w_uq, g_qk)
        yc = mla_attend(cqn, cqr, ckn, ckr, cv).reshape(bsz, n_ctx, MLA_HEADS * MLA_V) @ w_o
    return y, yc


def _affine_combine(e1, e2):
    a1, b1 = e1
    a2, b2 = e2
    return a2 * a1, a2 * b1 + b2


def linear_scan(a, b, h0, reverse):
    if h0 is not None:
        edge = b.shape[1] - 1 if reverse else 0
        b = b.at[:, edge].add(a[:, edge] * h0)
    _, h = lax.associative_scan(_affine_combine, (a, b), axis=1, reverse=reverse)
    return h


def s5_discretize(a_re, a_im, log_dt, b_re, b_im):
    lam = lax.complex(a_re.astype(jnp.float32), a_im.astype(jnp.float32))
    dt = jnp.exp(log_dt.astype(jnp.float32))[:, None]
    a_bar = jnp.exp(lam * dt)
    b_mat = lax.complex(b_re.astype(jnp.float32), b_im.astype(jnp.float32))
    b_bar = ((a_bar - 1) / lam)[..., None] * b_mat
    return a_bar, b_bar


def s5_glu(y, w_glu, dtype):
    z = jax.nn.gelu(y).astype(dtype)
    zv, zg = jnp.split(z @ w_glu, 2, axis=-1)
    return zv * jax.nn.sigmoid(zg)


def s5_mixer(h, hc, a_re, a_im, log_dt, b_re, b_im, c_re, c_im, d, w_glu, need_ctx):
    bsz, n_tok, _ = h.shape
    n_ctx = hc.shape[1]
    hf = h.astype(jnp.float32)
    hcf = hc.astype(jnp.float32)
    dd = d.astype(jnp.float32)
    u = hf.reshape(bsz, n_tok, S5_GROUPS, S5_GROUP).astype(jnp.complex64)
    uc = hcf.reshape(bsz, n_ctx, S5_GROUPS, S5_GROUP).astype(jnp.complex64)
    y = hf * dd
    yc = hcf * dd if need_ctx else None
    for direction, reverse in enumerate((False, True)):
        a_bar, b_bar = s5_discretize(a_re[direction], a_im[direction], log_dt[direction],
                                     b_re[direction], b_im[direction])
        c_mat = lax.complex(c_re[direction].astype(jnp.float32), c_im[direction].astype(jnp.float32))
        sc = linear_scan(jnp.broadcast_to(a_bar, (1, n_ctx) + a_bar.shape),
                         jnp.einsum('blgi,gpi->blgp', uc, b_bar), None, reverse)
        s0 = sc[:, 0] if reverse else sc[:, -1]
        s = linear_scan(jnp.broadcast_to(a_bar, (1, n_tok) + a_bar.shape),
                        jnp.einsum('blgi,gpi->blgp', u, b_bar), s0, reverse)
        y = y + jnp.real(jnp.einsum('blgp,gip->blgi', s, c_mat)).reshape(bsz, n_tok, D_MODEL)
        if need_ctx:
            yc = yc + jnp.real(jnp.einsum('blgp,gip->blgi', sc, c_mat)).reshape(bsz, n_ctx, D_MODEL)
    out = s5_glu(y, w_glu, h.dtype)
    out_c = s5_glu(yc, w_glu, h.dtype) if need_ctx else None
    return out, out_c


def depthwise_conv(x, w, b):
    out = lax.conv_general_dilated(x, w[:, None, :], window_strides=(1,),
                                   padding=[(CONV_LEFT, CONV_W - 1 - CONV_LEFT)],
                                   dimension_numbers=('NWC', 'WIO', 'NWC'),
                                   feature_group_count=x.shape[-1])
    return out + b


def rglru_coeffs(x, w_gate, b_gate, lam):
    bsz, n, _ = x.shape
    xf = x.astype(jnp.float32)
    xb = xf.reshape(bsz, n, LRU_BLOCKS, LRU_BW)
    gates = jnp.einsum('blnj,gnjk->gblnk', xb, w_gate.astype(jnp.float32)).reshape(2, bsz, n, LRU_WIDTH)
    gates = gates + b_gate.astype(jnp.float32)[:, None, None, :]
    r = jax.nn.sigmoid(gates[0])
    i_gate = jax.nn.sigmoid(gates[1])
    log_a = -LRU_C * r * jax.nn.softplus(-lam.astype(jnp.float32))
    a = jnp.exp(log_a)
    b = jnp.sqrt(-jnp.expm1(2.0 * log_a)) * (i_gate * xf)
    return a, b


def lru_mixer(h, hc, w_in, conv_w, conv_b, w_gate, b_gate, lam, w_out, need_ctx):
    w_g, w_x = w_in[:, :LRU_WIDTH], w_in[:, LRU_WIDTH:]
    xr = depthwise_conv(h @ w_x, conv_w, conv_b)
    xrc = depthwise_conv(hc @ w_x, conv_w, conv_b)
    hs = None
    hsc = None
    for direction, reverse in enumerate((False, True)):
        ac, bc = rglru_coeffs(xrc, w_gate[direction], b_gate[direction], lam[direction])
        sc = linear_scan(ac, bc, None, reverse)
        s0 = sc[:, 0] if reverse else sc[:, -1]
        a, b = rglru_coeffs(xr, w_gate[direction], b_gate[direction], lam[direction])
        s = linear_scan(a, b, s0, reverse)
        hs = s if hs is None else hs + s
        hsc = sc if hsc is None else hsc + sc
    y = (jax.nn.gelu(h @ w_g) * hs.astype(h.dtype)) @ w_out
    yc = (jax.nn.gelu(hc @ w_g) * hsc.astype(hc.dtype)) @ w_out if need_ctx else None
    return y, yc


def setup_inputs(seed: int = 0) -> dict:
    key = jax.random.key(seed)
    keys = iter(jax.random.split(key, 48))
    f32 = jnp.float32
    D = D_MODEL

    def nrm(shape, scale=1.0):
        return jax.random.normal(next(keys), shape, f32) * scale

    def gain(shape):
        return 1.0 + nrm(shape, 0.02)

    n_idx = jnp.arange(S5_STATE, dtype=f32)
    lru_u = jax.random.uniform(next(keys), (N_C, 2, LRU_WIDTH), f32, 0.9, 0.999)
    lru_a = lru_u ** (1.0 / LRU_C)
    return {
        'x': nrm((BATCH, SEQ, D)),
        'c': nrm((BATCH, D)),
        'ctx': nrm((BATCH, CTX_LEN, D)),
        'c_ctx': nrm((D,)),
        'ada_w': nrm((DEPTH, D, 6 * D), 0.5 * D ** -0.5),
        'ada_b': nrm((DEPTH, 6 * D), 0.01),
        'norm_g': gain((DEPTH, 2, D)),
        'mla_w_dq': nrm((N_A, D, MLA_Q_LORA), D ** -0.5),
        'mla_g_q': gain((N_A, MLA_Q_LORA)),
        'mla_w_uq': nrm((N_A, MLA_Q_LORA, MLA_HEADS * (MLA_NOPE + MLA_ROPE)), MLA_Q_LORA ** -0.5),
        'mla_w_dkv': nrm((N_A, D, MLA_KV_LORA + MLA_ROPE), D ** -0.5),
        'mla_g_kv': gain((N_A, MLA_KV_LORA)),
        'mla_w_ukv': nrm((N_A, MLA_KV_LORA, MLA_HEADS * (MLA_NOPE + MLA_V)), MLA_KV_LORA ** -0.5),
        'mla_g_qk': gain((N_A, 2, MLA_NOPE + MLA_ROPE)),
        'mla_w_o': nrm((N_A, MLA_HEADS * MLA_V, D), (MLA_HEADS * MLA_V) ** -0.5),
        's5_a_re': -0.5 + nrm((N_B, 2, S5_GROUPS, S5_STATE), 0.01),
        's5_a_im': math.pi * n_idx + nrm((N_B, 2, S5_GROUPS, S5_STATE), 0.01),
        's5_log_dt': jax.random.uniform(next(keys), (N_B, 2, S5_GROUPS), f32,
                                        math.log(S5_DT_MIN), math.log(S5_DT_MAX)),
        's5_b_re': nrm((N_B, 2, S5_GROUPS, S5_STATE, S5_GROUP), (2 * S5_GROUP) ** -0.5),
        's5_b_im': nrm((N_B, 2, S5_GROUPS, S5_STATE, S5_GROUP), (2 * S5_GROUP) ** -0.5),
        's5_c_re': nrm((N_B, 2, S5_GROUPS, S5_GROUP, S5_STATE), S5_STATE ** -0.5),
        's5_c_im': nrm((N_B, 2, S5_GROUPS, S5_GROUP, S5_STATE), S5_STATE ** -0.5),
        's5_d': nrm((N_B, D)),
        's5_w_glu': nrm((N_B, D, 2 * D), D ** -0.5),
        'lru_w_in': nrm((N_C, D, 2 * LRU_WIDTH), D ** -0.5),
        'lru_conv_w': nrm((N_C, CONV_W, LRU_WIDTH), CONV_W ** -0.5),
        'lru_conv_b': nrm((N_C, LRU_WIDTH), 0.01),
        'lru_w_gate': nrm((N_C, 2, 2, LRU_BLOCKS, LRU_BW, LRU_BW), LRU_BW ** -0.5),
        'lru_b_gate': nrm((N_C, 2, 2, LRU_WIDTH), 0.01),
        'lru_lambda': jnp.log(lru_a) - jnp.log1p(-lru_a),
        'lru_w_out': nrm((N_C, LRU_WIDTH, D), LRU_WIDTH ** -0.5),
        'mlp_w1': nrm((DEPTH, D, D_FF), D ** -0.5),
        'mlp_w2': nrm((DEPTH, D_FF, D), D_FF ** -0.5),
    }


def reference(x, c, ctx, c_ctx, ada_w, ada_b, norm_g,
              mla_w_dq, mla_g_q, mla_w_uq, mla_w_dkv, mla_g_kv, mla_w_ukv, mla_g_qk, mla_w_o,
              s5_a_re, s5_a_im, s5_log_dt, s5_b_re, s5_b_im, s5_c_re, s5_c_im, s5_d, s5_w_glu,
              lru_w_in, lru_conv_w, lru_conv_b, lru_w_gate, lru_b_gate, lru_lambda, lru_w_out,
              mlp_w1, mlp_w2):
    n_tok = x.shape[1]
    cos, sin = axial_rope_tables(n_tok)
    cos = cos.astype(x.dtype)
    sin = sin.astype(x.dtype)
    s_c = jax.nn.silu(c)
    s_cc = jax.nn.silu(c_ctx)
    lat, cx = x, ctx
    for i in range(DEPTH):
        need_ctx = i < DEPTH - 1
        mod = (s_c @ ada_w[i] + ada_b[i])[:, None, :]
        mod_c = (s_cc @ ada_w[i] + ada_b[i])[None, None, :]
        sh_a, sc_a, g_a, sh_m, sc_m, g_m = jnp.split(mod, 6, axis=-1)
        csh_a, csc_a, cg_a, csh_m, csc_m, cg_m = jnp.split(mod_c, 6, axis=-1)
        h = modulate(rmsnorm(lat, norm_g[i, 0]), sh_a, sc_a)
        hc = modulate(rmsnorm(cx, norm_g[i, 0]), csh_a, csc_a)
        kind, j = i % N_MIXERS, i // N_MIXERS
        if kind == 0:
            y, yc = mla_mixer(h, hc, cos, sin, mla_w_dq[j], mla_g_q[j], mla_w_uq[j], mla_w_dkv[j],
                              mla_g_kv[j], mla_w_ukv[j], mla_g_qk[j], mla_w_o[j], need_ctx)
        elif kind == 1:
            y, yc = s5_mixer(h, hc, s5_a_re[j], s5_a_im[j], s5_log_dt[j], s5_b_re[j], s5_b_im[j],
                             s5_c_re[j], s5_c_im[j], s5_d[j], s5_w_glu[j], need_ctx)
        else:
            y, yc = lru_mixer(h, hc, lru_w_in[j], lru_conv_w[j], lru_conv_b[j], lru_w_gate[j],
                              lru_b_gate[j], lru_lambda[j], lru_w_out[j], need_ctx)
        lat = lat + g_a * y
        lat = lat + g_m * sq_relu_mlp(modulate(rmsnorm(lat, norm_g[i, 1]), sh_m, sc_m), mlp_w1[i], mlp_w2[i])
        if need_ctx:
            cx = cx + cg_a * yc
            cx = cx + cg_m * sq_relu_mlp(modulate(rmsnorm(cx, norm_g[i, 1]), csh_m, csc_m),
                                         mlp_w1[i], mlp_w2[i])
    return lat
```

```python
import functools
import math

import jax
import jax.numpy as jnp
from jax import lax
from jax.experimental import pallas as pl
from jax.experimental.pallas import tpu as pltpu

F32 = jnp.float32
BF16 = jnp.bfloat16

EPS = 1e-6
ROPE_THETA = 10000.0
GRID_W = 64
N_MIXERS = 3

MLA_HEADS = 8
MLA_NOPE = 128
MLA_ROPE = 64
MLA_V = 128
MLA_QK = MLA_NOPE + MLA_ROPE
MLA_SCALE = 1.0 / math.sqrt(MLA_QK)

S5_GROUP = 16
S5_STATE = 64
LRU_BLOCKS = 10
LRU_BW = 128
LRU_C = 8.0

LANES = 128
SUBLANES = 8
MXU_N = 256
VMEM_LIMIT = 56 * 1024 * 1024


def _cparams(sem):
    return pltpu.CompilerParams(dimension_semantics=sem, vmem_limit_bytes=VMEM_LIMIT)


def _dot(a, b):
    return jnp.dot(a, b, preferred_element_type=F32)


def _dot_nt(a, b):
    return lax.dot_general(a, b, (((1,), (1,)), ((), ())), preferred_element_type=F32)


def _rms(x, g):
    inv = lax.rsqrt(jnp.mean(x * x, axis=-1, keepdims=True) + EPS)
    return x * inv * g


def _gelu(x):
    c = math.sqrt(2.0 / math.pi)
    return x * (0.5 * (1.0 + jnp.tanh(c * (x + 0.044715 * (x * x * x)))))


def _sigmoid(x):
    return 1.0 / (1.0 + jnp.exp(-x))


def _pick_tile(*lens, cap=512):
    t = cap
    while any(n % t for n in lens):
        t //= 2
    assert t >= SUBLANES
    return t


def _ada_kernel(c_ref, w_ref, b_ref, o_ref):
    c = c_ref[...]
    s = c * _sigmoid(c)
    w = w_ref[...]
    s_hi = s.astype(BF16)
    s_lo = (s - s_hi.astype(F32)).astype(BF16)
    w_hi = w.astype(BF16)
    w_lo = (w - w_hi.astype(F32)).astype(BF16)
    o_ref[...] = _dot(s_hi, w_hi) + _dot(s_lo, w_hi) + _dot(s_hi, w_lo) + b_ref[...]


def _ada_mod(cc, ada_w, ada_b):
    depth, d, d6 = ada_w.shape
    nk = d6 // d
    return pl.pallas_call(
        _ada_kernel,
        grid=(depth, nk),
        in_specs=[
            pl.BlockSpec((SUBLANES, d), lambda l, k: (0, 0)),
            pl.BlockSpec((None, d, d), lambda l, k: (l, 0, k)),
            pl.BlockSpec((None, 1, d), lambda l, k: (l, 0, k)),
        ],
        out_specs=pl.BlockSpec((None, SUBLANES, d), lambda l, k: (l, 0, k)),
        out_shape=jax.ShapeDtypeStruct((depth, SUBLANES, d6), F32),
        compiler_params=_cparams(("parallel", "parallel")),
        name="ada_mod",
    )(cc, ada_w, ada_b.reshape(depth, 1, d6))


def _mod_spec(k, tm, lseg, d):
    return pl.BlockSpec((None, 1, d), lambda i, *_: ((i * tm) // lseg, 0, k))


def _mlp_kernel(x_ref, y_ref, ga_ref, shm_ref, scm_ref, gm_ref, g2_ref, w1_ref, w2_ref,
                o_ref, x1_s, h_s, acc_s):
    j = pl.program_id(1)

    @pl.when(j == 0)
    def _():
        x1 = x_ref[...] + ga_ref[...] * y_ref[...].astype(F32)
        x1_s[...] = x1
        h = _rms(x1, g2_ref[...]) * (1.0 + scm_ref[...]) + shm_ref[...]
        h_s[...] = h.astype(BF16)
        acc_s[...] = jnp.zeros_like(acc_s)

    t = jnp.maximum(_dot(h_s[...], w1_ref[...]), 0.0)
    acc_s[...] += _dot((t * t).astype(BF16), w2_ref[...])

    @pl.when(j == pl.num_programs(1) - 1)
    def _():
        o_ref[...] = x1_s[...] + gm_ref[...] * acc_s[...]


def _mlp(x, y, mod, lseg, g2, w1, w2):
    n, d = x.shape
    dff = w1.shape[1]
    tm = _pick_tile(lseg)
    tf = min(dff, 1024)
    return pl.pallas_call(
        _mlp_kernel,
        grid=(n // tm, dff // tf),
        in_specs=[
            pl.BlockSpec((tm, d), lambda i, j: (i, 0)),
            pl.BlockSpec((tm, d), lambda i, j: (i, 0)),
            _mod_spec(2, tm, lseg, d),
            _mod_spec(3, tm, lseg, d),
            _mod_spec(4, tm, lseg, d),
            _mod_spec(5, tm, lseg, d),
            pl.BlockSpec((1, d), lambda i, j: (0, 0)),
            pl.BlockSpec((d, tf), lambda i, j: (0, j)),
            pl.BlockSpec((tf, d), lambda i, j: (j, 0)),
        ],
        out_specs=pl.BlockSpec((tm, d), lambda i, j: (i, 0)),
        out_shape=jax.ShapeDtypeStruct((n, d), F32),
        scratch_shapes=[pltpu.VMEM((tm, d), F32), pltpu.VMEM((tm, d), BF16), pltpu.VMEM((tm, d), F32)],
        compiler_params=_cparams(("parallel", "arbitrary")),
        name="mlp",
    )(x, y, mod, mod, mod, mod, g2, w1, w2)


def _proj_kernel(a_ref, w_ref, o_ref):
    o_ref[...] = _dot(a_ref[...].astype(BF16), w_ref[...])


def _proj(a, w, lseg):
    n, k = a.shape
    dout = w.shape[1]
    tm = _pick_tile(lseg)
    return pl.pallas_call(
        _proj_kernel,
        grid=(n // tm,),
        in_specs=[pl.BlockSpec((tm, k), lambda i: (i, 0)), pl.BlockSpec((k, dout), lambda i: (0, 0))],
        out_specs=pl.BlockSpec((tm, dout), lambda i: (i, 0)),
        out_shape=jax.ShapeDtypeStruct((n, dout), F32),
        compiler_params=_cparams(("parallel",)),
        name="proj",
    )(a, w)


def _rope(x, cos, sin_signed):
    width = x.shape[-1]
    quarter = MLA_ROPE // 4
    lane = lax.broadcasted_iota(jnp.int32, x.shape, 1)
    even_quarter = ((lane // quarter) % 2) == 0
    rot = jnp.where(even_quarter, pltpu.roll(x, width - quarter, 1), pltpu.roll(x, quarter, 1))
    return x * cos + rot * sin_signed


def _mla_proj_kernel(x_ref, sh_ref, sc_ref, g1_ref, cos_ref, sin_ref,
                     wdq_ref, gq_ref, wuq_ref, wdkv_ref, gkv_ref, wukv_ref,
                     gqn_ref, gqr_ref, gkn_ref, gkr_ref, seg_ref,
                     q_ref, k_ref, v_ref):
    nh = q_ref.shape[0]
    dn = nh * MLA_NOPE
    dr = nh * MLA_ROPE
    kvl = gkv_ref.shape[-1]
    h = (_rms(x_ref[...], g1_ref[...]) * (1.0 + sc_ref[...]) + sh_ref[...]).astype(BF16)
    ql = _rms(_dot(h, wdq_ref[...]), gq_ref[...]).astype(BF16)
    q = _dot(ql, wuq_ref[...])
    kv = _dot(h, wdkv_ref[...])
    ckv = _rms(kv[:, :kvl], gkv_ref[...]).astype(BF16)
    kvu = _dot(ckv, wukv_ref[...])

    cos = cos_ref[...]
    sin = sin_ref[...]
    reps = dr // LANES
    cos_q = jnp.concatenate([cos] * reps, axis=1)
    sin_q = jnp.concatenate([sin] * reps, axis=1)

    qr = q[:, dn:]
    ms = jnp.dot(qr * qr, seg_ref[...], preferred_element_type=F32, precision=lax.Precision.HIGHEST)
    qr = qr * lax.rsqrt(ms + EPS) * gqr_ref[...]
    qr = _rope(qr, cos_q, sin_q) * MLA_SCALE

    kr = kv[:, kvl:]
    kr = _rms(kr, gkr_ref[...])
    kr = _rope(kr, cos, sin)[:, :MLA_ROPE]

    for hh in range(nh):
        qn = _rms(q[:, hh * MLA_NOPE:(hh + 1) * MLA_NOPE], gqn_ref[...]) * MLA_SCALE
        qrh = qr[:, hh * MLA_ROPE:(hh + 1) * MLA_ROPE]
        q_ref[hh] = jnp.concatenate([qn, qrh], axis=1).astype(q_ref.dtype)
        kn = _rms(kvu[:, hh * MLA_NOPE:(hh + 1) * MLA_NOPE], gkn_ref[...])
        k_ref[hh] = jnp.concatenate([kn, kr], axis=1).astype(k_ref.dtype)
        v_ref[hh] = kvu[:, dn + hh * MLA_V:dn + (hh + 1) * MLA_V].astype(v_ref.dtype)


def _mla_proj(x, mod, lseg, g1, cos_t, sin_t, w):
    n, d = x.shape
    tm = _pick_tile(lseg, cap=256)
    nrope = cos_t.shape[0] // tm
    full = lambda a: pl.BlockSpec(a.shape, lambda i: (0,) * a.ndim)
    weights = [w["wdq"], w["gq"], w["wuq"], w["wdkv"], w["gkv"], w["wukv"],
               w["gqn"], w["gqr"], w["gkn"], w["gkr"], w["seg"]]
    return pl.pallas_call(
        _mla_proj_kernel,
        grid=(n // tm,),
        in_specs=[
            pl.BlockSpec((tm, d), lambda i: (i, 0)),
            _mod_spec(0, tm, lseg, d),
            _mod_spec(1, tm, lseg, d),
            pl.BlockSpec((1, d), lambda i: (0, 0)),
            pl.BlockSpec((tm, LANES), lambda i: (i % nrope, 0)),
            pl.BlockSpec((tm, LANES), lambda i: (i % nrope, 0)),
        ] + [full(a) for a in weights],
        out_specs=[
            pl.BlockSpec((MLA_HEADS, tm, MLA_QK), lambda i: (0, i, 0)),
            pl.BlockSpec((MLA_HEADS, tm, MLA_QK), lambda i: (0, i, 0)),
            pl.BlockSpec((MLA_HEADS, tm, MLA_V), lambda i: (0, i, 0)),
        ],
        out_shape=[
            jax.ShapeDtypeStruct((MLA_HEADS, n, MLA_QK), BF16),
            jax.ShapeDtypeStruct((MLA_HEADS, n, MLA_QK), BF16),
            jax.ShapeDtypeStruct((MLA_HEADS, n, MLA_V), BF16),
        ],
        compiler_params=_cparams(("parallel",)),
        name="mla_proj",
    )(x, mod, mod, g1, cos_t, sin_t, *weights)


def _attn_lat_kernel(q_ref, kc_ref, kl_ref, vc_ref, vl_ref, o_ref):
    q = q_ref[...]
    sc = _dot_nt(q, kc_ref[...])
    sl = _dot_nt(q, kl_ref[...])
    m = jnp.maximum(jnp.max(sc, axis=-1, keepdims=True), jnp.max(sl, axis=-1, keepdims=True))
    pc = jnp.exp(sc - m)
    pl_ = jnp.exp(sl - m)
    den = jnp.sum(pc, axis=-1, keepdims=True) + jnp.sum(pl_, axis=-1, keepdims=True)
    o = _dot(pc.astype(BF16), vc_ref[...]) + _dot(pl_.astype(BF16), vl_ref[...])
    o_ref[...] = (o / den).astype(o_ref.dtype)


def _attn_ctx_kernel(q_ref, kc_ref, vc_ref, o_ref):
    sc = _dot_nt(q_ref[...], kc_ref[...])
    m = jnp.max(sc, axis=-1, keepdims=True)
    pc = jnp.exp(sc - m)
    den = jnp.sum(pc, axis=-1, keepdims=True)
    o_ref[...] = (_dot(pc.astype(BF16), vc_ref[...]) / den).astype(o_ref.dtype)


def _attn_lat(q, kl, vl, kc, vc, nb):
    nh, n, _ = q.shape
    l = n // nb
    lc = kc.shape[1] // nb
    tq = _pick_tile(l, cap=256)
    nq = l // tq
    return pl.pallas_call(
        _attn_lat_kernel,
        grid=(nb, nh, nq),
        in_specs=[
            pl.BlockSpec((None, tq, MLA_QK), lambda b, h, i: (h, b * nq + i, 0)),
            pl.BlockSpec((None, lc, MLA_QK), lambda b, h, i: (h, b, 0)),
            pl.BlockSpec((None, l, MLA_QK), lambda b, h, i: (h, b, 0)),
            pl.BlockSpec((None, lc, MLA_V), lambda b, h, i: (h, b, 0)),
            pl.BlockSpec((None, l, MLA_V), lambda b, h, i: (h, b, 0)),
        ],
        out_specs=pl.BlockSpec((tq, MLA_V), lambda b, h, i: (b * nq + i, h)),
        out_shape=jax.ShapeDtypeStruct((n, nh * MLA_V), BF16),
        compiler_params=_cparams(("parallel", "parallel", "arbitrary")),
        name="attn_lat",
    )(q, kc, kl, vc, vl)


def _attn_ctx(q, kc, vc, nb):
    nh, n, _ = q.shape
    lc = n // nb
    return pl.pallas_call(
        _attn_ctx_kernel,
        grid=(nb, nh),
        in_specs=[
            pl.BlockSpec((None, lc, MLA_QK), lambda b, h: (h, b, 0)),
            pl.BlockSpec((None, lc, MLA_QK), lambda b, h: (h, b, 0)),
            pl.BlockSpec((None, lc, MLA_V), lambda b, h: (h, b, 0)),
        ],
        out_specs=pl.BlockSpec((lc, MLA_V), lambda b, h: (b, h)),
        out_shape=jax.ShapeDtypeStruct((n, nh * MLA_V), BF16),
        compiler_params=_cparams(("parallel", "parallel")),
        name="attn_ctx",
    )(q, kc, vc)


def _rope_tables(n_tok):
    rows = n_tok // GRID_W
    row = jnp.repeat(jnp.arange(rows, dtype=F32), GRID_W)
    col = jnp.tile(jnp.arange(GRID_W, dtype=F32), rows)
    n_freq = MLA_ROPE // 4
    freqs = ROPE_THETA ** (-jnp.arange(n_freq, dtype=F32) / n_freq)
    ang_r = row[:, None] * freqs[None, :]
    ang_c = col[:, None] * freqs[None, :]
    ang = jnp.concatenate([ang_r, ang_r, ang_c, ang_c], axis=-1)
    sign = jnp.tile(jnp.repeat(jnp.array([-1.0, 1.0], F32), n_freq), 2)
    cos = jnp.cos(ang)
    sin = jnp.sin(ang) * sign
    return jnp.concatenate([cos, cos], axis=1), jnp.concatenate([sin, sin], axis=1)


def _mla_weights(w_dq, g_q, w_uq, w_dkv, g_kv, w_ukv, g_qk, w_o):
    nh = MLA_HEADS
    qlora = w_dq.shape[1]
    kvl = g_kv.shape[0]
    uq = w_uq.reshape(qlora, nh, MLA_QK)
    wuq = jnp.concatenate([uq[:, :, :MLA_NOPE].reshape(qlora, nh * MLA_NOPE),
                           uq[:, :, MLA_NOPE:].reshape(qlora, nh * MLA_ROPE)], axis=1)
    ukv = w_ukv.reshape(kvl, nh, MLA_NOPE + MLA_V)
    wukv = jnp.concatenate([ukv[:, :, :MLA_NOPE].reshape(kvl, nh * MLA_NOPE),
                            ukv[:, :, MLA_NOPE:].reshape(kvl, nh * MLA_V)], axis=1)
    wdkv = jnp.concatenate([w_dkv, w_dkv[:, kvl:]], axis=1)
    dr = nh * MLA_ROPE
    seg_id = jnp.arange(dr) // MLA_ROPE
    seg = (seg_id[:, None] == seg_id[None, :]).astype(F32) / MLA_ROPE
    return {
        "wdq": w_dq.astype(BF16), "gq": g_q[None, :], "wuq": wuq.astype(BF16),
        "wdkv": wdkv.astype(BF16), "gkv": g_kv[None, :], "wukv": wukv.astype(BF16),
        "gqn": g_qk[0:1, :MLA_NOPE], "gqr": jnp.tile(g_qk[0:1, MLA_NOPE:], (1, nh)),
        "gkn": g_qk[1:2, :MLA_NOPE], "gkr": jnp.tile(g_qk[1:2, MLA_NOPE:], (1, 2)),
        "seg": seg, "wo": w_o.astype(BF16),
    }


def _mla_layer(lat, cx, mod_l, mod_c, nb, g1, wts, need_ctx):
    l = lat.shape[0] // nb
    lc = cx.shape[0] // nb
    w = _mla_weights(*wts)
    cos_t, sin_t = _rope_tables(l)
    tmc = _pick_tile(lc, cap=256)
    ql, kl, vl = _mla_proj(lat, mod_l, l, g1, cos_t, sin_t, w)
    qc, kc, vc = _mla_proj(cx, mod_c, lc, g1, jnp.ones((tmc, LANES), F32), jnp.zeros((tmc, LANES), F32), w)
    o = _attn_lat(ql, kl, vl, kc, vc, nb)
    y = _proj(o, w["wo"], l)
    yc = None
    if need_ctx:
        yc = _proj(_attn_ctx(qc, kc, vc, nb), w["wo"], lc)
    return y, yc


def _s5_kernel(xf_ref, xr_ref, sh_ref, sc_ref, g1_ref, dd_ref, wb_ref, wc_ref, are_ref, aim_ref, st0_ref,
               yf_ref, yr_ref, st_ref, sf_s, sr_s, *, pitch):
    nb, t, d = xf_ref.shape
    nt = st_ref.shape[2]
    nre = nt // 2
    nv = nre // SUBLANES
    nj = wb_ref.shape[1]
    kblocks = d // LANES
    per_kb = (nj // 2) // kblocks
    nq = wc_ref.shape[1]
    tiles_q = nre // nq
    step = pl.program_id(0)

    @pl.when(step == 0)
    def _():
        st_ref[...] = st0_ref[...]

    def prep(x_ref):
        return _rms(x_ref[...], g1_ref[...]) * (1.0 + sc_ref[...]) + sh_ref[...]

    hs = (prep(xf_ref), prep(xr_ref))
    bufs = (sf_s, sr_s)

    def slab(b, n):
        return (b * nt + n) * pitch

    for dr in range(2):
        hb = hs[dr].reshape(nb * t, d).astype(BF16)
        for j in range(nj):
            kb = (j % (nj // 2)) // per_kb
            r = _dot(hb[:, kb * LANES:(kb + 1) * LANES], wb_ref[dr, j])
            for b in range(nb):
                for half in range(MXU_N // LANES):
                    n = j * (MXU_N // LANES) + half
                    bufs[dr][pl.ds(slab(b, n), t), :] = r[b * t:(b + 1) * t, half * LANES:(half + 1) * LANES]

    consts = [[(are_ref[dr, v * SUBLANES:(v + 1) * SUBLANES, :], aim_ref[dr, v * SUBLANES:(v + 1) * SUBLANES, :])
               for v in range(nv)] for dr in range(2)]
    for b in range(nb):
        init = tuple(st_ref[b, dr, part * nre + v * SUBLANES:part * nre + (v + 1) * SUBLANES, :]
                     for dr in range(2) for v in range(nv) for part in range(2))

        def body(i, carry, b=b):
            out = []
            for dr in range(2):
                tt = i if dr == 0 else t - 1 - i
                for v in range(nv):
                    sre, sim = carry[(dr * nv + v) * 2], carry[(dr * nv + v) * 2 + 1]
                    are, aim = consts[dr][v]
                    row_re = slab(b, v * SUBLANES) + tt
                    row_im = slab(b, nre + v * SUBLANES) + tt
                    bre = bufs[dr][pl.ds(row_re, SUBLANES, stride=pitch), :]
                    bim = bufs[dr][pl.ds(row_im, SUBLANES, stride=pitch), :]
                    new_re = are * sre - aim * sim + bre
                    new_im = are * sim + aim * sre + bim
                    bufs[dr][pl.ds(row_re, SUBLANES, stride=pitch), :] = new_re
                    bufs[dr][pl.ds(row_im, SUBLANES, stride=pitch), :] = new_im
                    out += [new_re, new_im]
            return tuple(out)

        fin = lax.fori_loop(0, t, body, init)
        k = 0
        for dr in range(2):
            for v in range(nv):
                for part in range(2):
                    st_ref[b, dr, part * nre + v * SUBLANES:part * nre + (v + 1) * SUBLANES, :] = fin[k]
                    k += 1

    skip = hs[0] * dd_ref[...]
    outs = (yf_ref, yr_ref)
    for dr in range(2):
        for q in range(nq):
            rows = []
            for b in range(nb):
                pieces = [bufs[dr][pl.ds(slab(b, part * nre + q * tiles_q + i), t), :]
                          for part in range(2) for i in range(tiles_q)]
                rows.append(jnp.concatenate(pieces, axis=1))
            yq = _dot(jnp.concatenate(rows, axis=0).astype(BF16), wc_ref[dr, q])
            for b in range(nb):
                val = yq[b * t:(b + 1) * t, :]
                if dr == 0:
                    val = val + skip[b, :, q * LANES:(q + 1) * LANES]
                outs[dr][b, :, q * LANES:(q + 1) * LANES] = val


def _s5_scan(x3, mod3, g1, dd, wb, wc, are, aim, st0, t):
    nb, l, d = x3.shape
    nc = l // t
    pitch = t + SUBLANES
    nt = st0.shape[2]
    full = lambda a: pl.BlockSpec(a.shape, lambda c: (0,) * a.ndim)
    buf = pltpu.VMEM((nb * nt * pitch, LANES), F32)
    return pl.pallas_call(
        functools.partial(_s5_kernel, pitch=pitch),
        grid=(nc,),
        in_specs=[
            pl.BlockSpec((nb, t, d), lambda c: (0, c, 0)),
            pl.BlockSpec((nb, t, d), lambda c: (0, nc - 1 - c, 0)),
            pl.BlockSpec((nb, 1, d), lambda c: (0, 0, 0)),
            pl.BlockSpec((nb, 1, d), lambda c: (0, 0, 1)),
            full(g1), full(dd), full(wb), full(wc), full(are), full(aim), full(st0),
        ],
        out_specs=[
            pl.BlockSpec((nb, t, d), lambda c: (0, c, 0)),
            pl.BlockSpec((nb, t, d), lambda c: (0, nc - 1 - c, 0)),
            full(st0),
        ],
        out_shape=[
            jax.ShapeDtypeStruct((nb, l, d), F32),
            jax.ShapeDtypeStruct((nb, l, d), F32),
            jax.ShapeDtypeStruct(st0.shape, F32),
        ],
        scratch_shapes=[buf, buf],
        compiler_params=_cparams(("arbitrary",)),
        name="s5_scan",
    )(x3, x3, mod3, mod3, g1, dd, wb, wc, are, aim, st0)


def _s5_glu_kernel(yf_ref, yr_ref, w_ref, o_ref):
    d = o_ref.shape[-1]
    z = _dot(_gelu(yf_ref[...] + yr_ref[...]).astype(BF16), w_ref[...])
    o_ref[...] = z[:, :d] * _sigmoid(z[:, d:])


def _s5_glu(yf, yr, w, lseg):
    n, d = yf.shape
    tm = _pick_tile(lseg)
    return pl.pallas_call(
        _s5_glu_kernel,
        grid=(n // tm,),
        in_specs=[pl.BlockSpec((tm, d), lambda i: (i, 0)), pl.BlockSpec((tm, d), lambda i: (i, 0)),
                  pl.BlockSpec(w.shape, lambda i: (0, 0))],
        out_specs=pl.BlockSpec((tm, d), lambda i: (i, 0)),
        out_shape=jax.ShapeDtypeStruct((n, d), F32),
        compiler_params=_cparams(("parallel",)),
        name="s5_glu",
    )(yf, yr, w)


def _s5_weights(a_re, a_im, log_dt, b_re, b_im, c_re, c_im):
    ng, ns = a_re.shape[1:]
    gi = b_re.shape[-1]
    d = ng * gi
    lam = lax.complex(a_re.astype(F32), a_im.astype(F32))
    dt = jnp.exp(log_dt.astype(F32))[..., None]
    a_bar = jnp.exp(lam * dt)
    b_bar = ((a_bar - 1) / lam)[..., None] * lax.complex(b_re.astype(F32), b_im.astype(F32))
    eye = jnp.eye(ng, dtype=F32)

    def in_full(part):
        bt = jnp.swapaxes(part, -1, -2)
        return (bt[:, :, :, None, :] * eye[None, :, None, :, None]).reshape(2, d, ng * ns)

    def out_full(part):
        ct = jnp.swapaxes(part, -1, -2)
        return (ct[:, :, :, None, :] * eye[None, :, None, :, None]).reshape(2, ng * ns, d)

    ncol = ng * ns // MXU_N
    kblocks = d // LANES
    kb_of = jnp.arange(ncol) // (ncol // kblocks)

    def in_tiles(full):
        f = full.reshape(2, kblocks, LANES, ncol, MXU_N)
        return f[:, kb_of, :, jnp.arange(ncol), :]

    wb = jnp.concatenate([in_tiles(in_full(jnp.real(b_bar))), in_tiles(in_full(jnp.imag(b_bar)))], axis=0)
    wb = jnp.swapaxes(wb, 0, 1).astype(BF16)

    nq = d // LANES
    rows_q = ng * ns // nq

    def out_tiles(full):
        f = full.reshape(2, nq, rows_q, nq, LANES)
        return f[:, jnp.arange(nq), :, jnp.arange(nq), :]

    c_mat_re, c_mat_im = c_re.astype(F32), c_im.astype(F32)
    wc = jnp.concatenate([out_tiles(out_full(c_mat_re)), out_tiles(out_full(-c_mat_im))], axis=2)
    wc = jnp.swapaxes(wc, 0, 1).astype(BF16)
    are = jnp.real(a_bar).reshape(2, ng * ns // LANES, LANES)
    aim = jnp.imag(a_bar).reshape(2, ng * ns // LANES, LANES)
    return wb, wc, are, aim


def _s5_layer(lat, cx, mod_l, mod_c, nb, g1, a_re, a_im, log_dt, b_re, b_im, c_re, c_im, dvec, w_glu, need_ctx):
    d = lat.shape[1]
    l = lat.shape[0] // nb
    lc = cx.shape[0] // nb
    wb, wc, are, aim = _s5_weights(a_re, a_im, log_dt, b_re, b_im, c_re, c_im)
    t = _pick_tile(l, lc, cap=64)
    st0 = jnp.zeros((nb, 2, 2 * are.shape[1], LANES), F32)
    dd = dvec[None, :].astype(F32)
    ycf, ycr, st = _s5_scan(cx.reshape(nb, lc, d), mod_c, g1, dd, wb, wc, are, aim, st0, t)
    yf, yr, _ = _s5_scan(lat.reshape(nb, l, d), mod_l, g1, dd, wb, wc, are, aim, st, t)
    wg = w_glu.astype(BF16)
    y = _s5_glu(yf.reshape(nb * l, d), yr.reshape(nb * l, d), wg, l)
    yc = _s5_glu(ycf.reshape(nb * lc, d), ycr.reshape(nb * lc, d), wg, lc) if need_ctx else None
    return y, yc


def _lru_in_kernel(x_ref, sh_ref, sc_ref, g1_ref, w_ref, xw_ref, gl_ref):
    wd = xw_ref.shape[-1]
    h = (_rms(x_ref[...], g1_ref[...]) * (1.0 + sc_ref[...]) + sh_ref[...]).astype(BF16)
    z = _dot(h, w_ref[...])
    gl_ref[...] = _gelu(z[:, :wd]).astype(gl_ref.dtype)
    xw_ref[...] = z[:, wd:]


def _lru_in(x, mod, lseg, g1, w_in):
    n, d = x.shape
    wd = w_in.shape[1] // 2
    tm = _pick_tile(lseg)
    return pl.pallas_call(
        _lru_in_kernel,
        grid=(n // tm,),
        in_specs=[pl.BlockSpec((tm, d), lambda i: (i, 0)), _mod_spec(0, tm, lseg, d), _mod_spec(1, tm, lseg, d),
                  pl.BlockSpec((1, d), lambda i: (0, 0)), pl.BlockSpec(w_in.shape, lambda i: (0, 0))],
        out_specs=[pl.BlockSpec((tm, wd), lambda i: (i, 0)), pl.BlockSpec((tm, wd), lambda i: (i, 0))],
        out_shape=[jax.ShapeDtypeStruct((n, wd), F32), jax.ShapeDtypeStruct((n, wd), BF16)],
        compiler_params=_cparams(("parallel",)),
        name="lru_in",
    )(x, mod, mod, g1, w_in)


def _lru_scan_kernel(xf_ref, xfp_ref, xfn_ref, xr_ref, xrp_ref, xrn_ref,
                     cw_ref, cb_ref, wg_ref, bg_ref, sp_ref, st0_ref,
                     hf_ref, hr_ref, st_ref, ext_s, a_s, b_s, *, pitch):
    nb, t, wd = xf_ref.shape
    nblk = wd // LANES
    halo = SUBLANES
    step = pl.program_id(0)
    last = pl.num_programs(0) - 1

    @pl.when(step == 0)
    def _():
        st_ref[...] = st0_ref[...]
        a_s[...] = jnp.zeros_like(a_s)
        b_s[...] = jnp.zeros_like(b_s)

    def coeffs(dr, x_ref, prev_ref, next_ref, at_start, at_end):
        ext_s[:, 0:halo, :] = jnp.where(at_start, 0.0, prev_ref[...])
        ext_s[:, halo:halo + t, :] = x_ref[...]
        ext_s[:, halo + t:2 * halo + t, :] = jnp.where(at_end, 0.0, next_ref[...])
        xr = cb_ref[...]
        for tap in range(cw_ref.shape[0]):
            xr = xr + cw_ref[tap:tap + 1, :] * ext_s[:, halo - 1 + tap:halo - 1 + tap + t, :]
        for n in range(nblk):
            xb = xr[:, :, n * LANES:(n + 1) * LANES].reshape(nb * t, LANES)
            g = _dot(xb.astype(BF16), wg_ref[dr, n]) + bg_ref[dr, n]
            r = _sigmoid(g[:, :LANES])
            ig = _sigmoid(g[:, LANES:])
            log_a = -LRU_C * r * sp_ref[dr, n]
            a = jnp.exp(log_a)
            bc = jnp.sqrt(1.0 - jnp.exp(2.0 * log_a)) * (ig * xb)
            for b in range(nb):
                a_s[dr, n, pl.ds(b * pitch, t), :] = a[b * t:(b + 1) * t]
                b_s[dr, n, pl.ds(b * pitch, t), :] = bc[b * t:(b + 1) * t]

    coeffs(0, xf_ref, xfp_ref, xfn_ref, step == 0, step == last)
    coeffs(1, xr_ref, xrp_ref, xrn_ref, step == last, step == 0)

    init = tuple(st_ref[dr, n] for dr in range(2) for n in range(nblk))

    def body(i, carry):
        out = []
        for dr in range(2):
            tt = i if dr == 0 else t - 1 - i
            for n in range(nblk):
                a = a_s[dr, n, pl.ds(tt, SUBLANES, stride=pitch), :]
                bc = b_s[dr, n, pl.ds(tt, SUBLANES, stride=pitch), :]
                hnew = a * carry[dr * nblk + n] + bc
                b_s[dr, n, pl.ds(tt, SUBLANES, stride=pitch), :] = hnew
                out.append(hnew)
        return tuple(out)

    fin = lax.fori_loop(0, t, body, init)
    k = 0
    for dr in range(2):
        for n in range(nblk):
            st_ref[dr, n] = fin[k]
            k += 1
    outs = (hf_ref, hr_ref)
    for dr in range(2):
        for n in range(nblk):
            for b in range(nb):
                outs[dr][b, :, n * LANES:(n + 1) * LANES] = b_s[dr, n, pl.ds(b * pitch, t), :]


def _lru_scan(xw3, cw, cb, wg, bg, sp, st0, t):
    nb, l, wd = xw3.shape
    nc = l // t
    nblk = wd // LANES
    hb = t // SUBLANES
    nhb = l // SUBLANES
    pitch = t + SUBLANES
    full = lambda a: pl.BlockSpec(a.shape, lambda c: (0,) * a.ndim)
    chunk = lambda f: pl.BlockSpec((nb, t, wd), lambda c: (0, f(c), 0))
    halo_prev = lambda f: pl.BlockSpec((nb, SUBLANES, wd), lambda c: (0, jnp.maximum(f(c) * hb - 1, 0), 0))
    halo_next = lambda f: pl.BlockSpec((nb, SUBLANES, wd), lambda c: (0, jnp.minimum((f(c) + 1) * hb, nhb - 1), 0))
    fwd = lambda c: c
    rev = lambda c: nc - 1 - c
    coef = pltpu.VMEM((2, nblk, SUBLANES * pitch, LANES), F32)
    return pl.pallas_call(
        functools.partial(_lru_scan_kernel, pitch=pitch),
        grid=(nc,),
        in_specs=[chunk(fwd), halo_prev(fwd), halo_next(fwd), chunk(rev), halo_prev(rev), halo_next(rev),
                  full(cw), full(cb), full(wg), full(bg), full(sp), full(st0)],
        out_specs=[chunk(fwd), chunk(rev), full(st0)],
        out_shape=[jax.ShapeDtypeStruct((nb, l, wd), F32), jax.ShapeDtypeStruct((nb, l, wd), F32),
                   jax.ShapeDtypeStruct(st0.shape, F32)],
        scratch_shapes=[pltpu.VMEM((nb, t + 2 * SUBLANES, wd), F32), coef, coef],
        compiler_params=_cparams(("arbitrary",)),
        name="lru_scan",
    )(xw3, xw3, xw3, xw3, xw3, xw3, cw, cb, wg, bg, sp, st0)


def _lru_out_kernel(gl_ref, hf_ref, hr_ref, w_ref, o_ref):
    u = gl_ref[...].astype(F32) * (hf_ref[...] + hr_ref[...])
    o_ref[...] = _dot(u.astype(BF16), w_ref[...])


def _lru_out(gl, hf, hr, w, lseg):
    n, wd = gl.shape
    d = w.shape[1]
    tm = _pick_tile(lseg)
    row = lambda: pl.BlockSpec((tm, wd), lambda i: (i, 0))
    return pl.pallas_call(
        _lru_out_kernel,
        grid=(n // tm,),
        in_specs=[row(), row(), row(), pl.BlockSpec(w.shape, lambda i: (0, 0))],
        out_specs=pl.BlockSpec((tm, d), lambda i: (i, 0)),
        out_shape=jax.ShapeDtypeStruct((n, d), F32),
        compiler_params=_cparams(("parallel",)),
        name="lru_out",
    )(gl, hf, hr, w)


def _lru_layer(lat, cx, mod_l, mod_c, nb, g1, w_in, conv_w, conv_b, w_gate, b_gate, lam, w_out, need_ctx):
    l = lat.shape[0] // nb
    lc = cx.shape[0] // nb
    wd = w_in.shape[1] // 2
    nblk = wd // LANES
    w_in_b = w_in.astype(BF16)
    xw, gl = _lru_in(lat, mod_l, l, g1, w_in_b)
    xwc, glc = _lru_in(cx, mod_c, lc, g1, w_in_b)
    wg = jnp.concatenate([w_gate[:, 0], w_gate[:, 1]], axis=-1).astype(BF16)
    bgate = b_gate.astype(F32).reshape(2, 2, nblk, 1, LANES)
    bg = jnp.concatenate([bgate[:, 0], bgate[:, 1]], axis=-1)
    sp = jax.nn.softplus(-lam.astype(F32)).reshape(2, nblk, 1, LANES)
    cw = conv_w.astype(F32)
    cb = conv_b.astype(F32)[None, :]
    t = _pick_tile(l, lc, cap=64)
    st0 = jnp.zeros((2, nblk, SUBLANES, LANES), F32)
    hcf, hcr, st = _lru_scan(xwc.reshape(nb, lc, wd), cw, cb, wg, bg, sp, st0, t)
    hf, hr, _ = _lru_scan(xw.reshape(nb, l, wd), cw, cb, wg, bg, sp, st, t)
    wo = w_out.astype(BF16)
    y = _lru_out(gl, hf.reshape(nb * l, wd), hr.reshape(nb * l, wd), wo, l)
    yc = _lru_out(glc, hcf.reshape(nb * lc, wd), hcr.reshape(nb * lc, wd), wo, lc) if need_ctx else None
    return y, yc


def kernel(x, c, ctx, c_ctx, ada_w, ada_b, norm_g, mla_w_dq, mla_g_q, mla_w_uq, mla_w_dkv, mla_g_kv, mla_w_ukv, mla_g_qk, mla_w_o, s5_a_re, s5_a_im, s5_log_dt, s5_b_re, s5_b_im, s5_c_re, s5_c_im, s5_d, s5_w_glu, lru_w_in, lru_conv_w, lru_conv_b, lru_w_gate, lru_b_gate, lru_lambda, lru_w_out, mlp_w1, mlp_w2):
    nb, l, d = x.shape
    lc = ctx.shape[1]
    depth = ada_w.shape[0]
    assert nb < SUBLANES
    cc = jnp.concatenate([c, c_ctx[None, :], jnp.zeros((SUBLANES - nb - 1, d), F32)], axis=0)
    mod = _ada_mod(cc, ada_w, ada_b)
    lat = x.reshape(nb * l, d)
    cx = ctx.reshape(nb * lc, d)
    for i in range(depth):
        need_ctx = i < depth - 1
        mod_l = mod[i, :nb, None, :]
        mod_c = jnp.broadcast_to(mod[i, nb][None, None, :], (nb, 1, mod.shape[-1]))
        g1 = norm_g[i, 0][None, :]
        g2 = norm_g[i, 1][None, :]
        kind, j = i % N_MIXERS, i // N_MIXERS
        if kind == 0:
            y, yc = _mla_layer(lat, cx, mod_l, mod_c, nb, g1,
                               (mla_w_dq[j], mla_g_q[j], mla_w_uq[j], mla_w_dkv[j], mla_g_kv[j], mla_w_ukv[j],
                                mla_g_qk[j], mla_w_o[j]), need_ctx)
        elif kind == 1:
            y, yc = _s5_layer(lat, cx, mod_l, mod_c, nb, g1, s5_a_re[j], s5_a_im[j], s5_log_dt[j], s5_b_re[j],
                              s5_b_im[j], s5_c_re[j], s5_c_im[j], s5_d[j], s5_w_glu[j], need_ctx)
        else:
            y, yc = _lru_layer(lat, cx, mod_l, mod_c, nb, g1, lru_w_in[j], lru_conv_w[j], lru_conv_b[j],
                               lru_w_gate[j], lru_b_gate[j], lru_lambda[j], lru_w_out[j], need_ctx)
        w1 = mlp_w1[i].astype(BF16)
        w2 = mlp_w2[i].astype(BF16)
        lat = _mlp(lat, y, mod_l, l, g2, w1, w2)
        if need_ctx:
            cx = _mlp(cx, yc, mod_c, lc, g2, w1, w2)
    return lat.reshape(nb, l, d)
```

```python
import functools
import math

import jax
import jax.numpy as jnp
from jax import lax
from jax.experimental import pallas as pl
from jax.experimental.pallas import tpu as pltpu

F32 = jnp.float32
BF16 = jnp.bfloat16

EPS = 1e-6
ROPE_THETA = 10000.0
GRID_W = 64
N_MIXERS = 3

MLA_HEADS = 8
MLA_NOPE = 128
MLA_ROPE = 64
MLA_V = 128
MLA_QK = MLA_NOPE + MLA_ROPE
MLA_SCALE = math.log2(math.e) / math.sqrt(MLA_QK)

S5_GROUP = 16
S5_STATE = 64
LRU_BLOCKS = 10
LRU_BW = 128
LRU_C = 8.0

LANES = 128
SUBLANES = 8
MXU_N = 256
VMEM_LIMIT = 56 * 1024 * 1024


def _cparams(sem):
    return pltpu.CompilerParams(dimension_semantics=sem, vmem_limit_bytes=VMEM_LIMIT)


def _dot(a, b):
    return jnp.dot(a, b, preferred_element_type=F32)


def _dot_nt(a, b):
    return lax.dot_general(a, b, (((1,), (1,)), ((), ())), preferred_element_type=F32)


def _rms(x, g):
    inv = lax.rsqrt(jnp.mean(x * x, axis=-1, keepdims=True) + EPS)
    return x * inv * g


def _gelu(x):
    c = math.sqrt(2.0 / math.pi)
    return x * (0.5 * (1.0 + jnp.tanh(c * (x + 0.044715 * (x * x * x)))))


def _sigmoid(x):
    return 1.0 / (1.0 + jnp.exp(-x))


def _pick_tile(*lens, cap=512):
    t = cap
    while any(n % t for n in lens):
        t //= 2
    assert t >= SUBLANES
    return t


def _ada_kernel(c_ref, w_ref, b_ref, o_ref):
    c = c_ref[...]
    s = c * _sigmoid(c)
    w = w_ref[...]
    s_hi = s.astype(BF16)
    s_lo = (s - s_hi.astype(F32)).astype(BF16)
    w_hi = w.astype(BF16)
    w_lo = (w - w_hi.astype(F32)).astype(BF16)
    o_ref[...] = _dot(s_hi, w_hi) + _dot(s_lo, w_hi) + _dot(s_hi, w_lo) + b_ref[...]


def _ada_mod(cc, ada_w, ada_b):
    depth, d, d6 = ada_w.shape
    nk = d6 // d
    return pl.pallas_call(
        _ada_kernel,
        grid=(depth, nk),
        in_specs=[
            pl.BlockSpec((SUBLANES, d), lambda l, k: (0, 0)),
            pl.BlockSpec((None, d, d), lambda l, k: (l, 0, k)),
            pl.BlockSpec((None, 1, d), lambda l, k: (l, 0, k)),
        ],
        out_specs=pl.BlockSpec((None, SUBLANES, d), lambda l, k: (l, 0, k)),
        out_shape=jax.ShapeDtypeStruct((depth, SUBLANES, d6), F32),
        compiler_params=_cparams(("parallel", "parallel")),
        name="ada_mod",
    )(cc, ada_w, ada_b.reshape(depth, 1, d6))


def _mod_spec(k, tm, lseg, d):
    return pl.BlockSpec((None, 1, d), lambda i, *_: ((i * tm) // lseg, 0, k))


def _mlp_kernel(x_ref, y_ref, ga_ref, shm_ref, scm_ref, gm_ref, g2_ref, w1_ref, w2_ref,
                o_ref, x1_s, h_s, acc_s):
    j = pl.program_id(1)

    @pl.when(j == 0)
    def _():
        x1 = x_ref[...] + ga_ref[...] * y_ref[...].astype(F32)
        x1_s[...] = x1
        h = _rms(x1, g2_ref[...]) * (1.0 + scm_ref[...]) + shm_ref[...]
        h_s[...] = h.astype(BF16)
        acc_s[...] = jnp.zeros_like(acc_s)

    t = jnp.maximum(_dot(h_s[...], w1_ref[...]), 0.0)
    acc_s[...] += _dot((t * t).astype(BF16), w2_ref[...])

    @pl.when(j == pl.num_programs(1) - 1)
    def _():
        o_ref[...] = x1_s[...] + gm_ref[...] * acc_s[...]


def _mlp(x, y, mod, lseg, g2, w1, w2):
    n, d = x.shape
    dff = w1.shape[1]
    tm = _pick_tile(lseg)
    tf = min(dff, 1024)
    return pl.pallas_call(
        _mlp_kernel,
        grid=(n // tm, dff // tf),
        in_specs=[
            pl.BlockSpec((tm, d), lambda i, j: (i, 0)),
            pl.BlockSpec((tm, d), lambda i, j: (i, 0)),
            _mod_spec(2, tm, lseg, d),
            _mod_spec(3, tm, lseg, d),
            _mod_spec(4, tm, lseg, d),
            _mod_spec(5, tm, lseg, d),
            pl.BlockSpec((1, d), lambda i, j: (0, 0)),
            pl.BlockSpec((d, tf), lambda i, j: (0, j)),
            pl.BlockSpec((tf, d), lambda i, j: (j, 0)),
        ],
        out_specs=pl.BlockSpec((tm, d), lambda i, j: (i, 0)),
        out_shape=jax.ShapeDtypeStruct((n, d), F32),
        scratch_shapes=[pltpu.VMEM((tm, d), F32), pltpu.VMEM((tm, d), BF16), pltpu.VMEM((tm, d), F32)],
        compiler_params=_cparams(("parallel", "arbitrary")),
        name="mlp",
    )(x, y, mod, mod, mod, mod, g2, w1, w2)


def _proj_kernel(a_ref, w_ref, o_ref):
    o_ref[...] = _dot(a_ref[...].astype(BF16), w_ref[...])


def _proj(a, w, lseg):
    n, k = a.shape
    dout = w.shape[1]
    tm = _pick_tile(lseg)
    return pl.pallas_call(
        _proj_kernel,
        grid=(n // tm,),
        in_specs=[pl.BlockSpec((tm, k), lambda i: (i, 0)), pl.BlockSpec((k, dout), lambda i: (0, 0))],
        out_specs=pl.BlockSpec((tm, dout), lambda i: (i, 0)),
        out_shape=jax.ShapeDtypeStruct((n, dout), F32),
        compiler_params=_cparams(("parallel",)),
        name="proj",
    )(a, w)


def _rope(x, cos, sin_signed):
    width = x.shape[-1]
    quarter = MLA_ROPE // 4
    lane = lax.broadcasted_iota(jnp.int32, x.shape, 1)
    even_quarter = ((lane // quarter) % 2) == 0
    rot = jnp.where(even_quarter, pltpu.roll(x, width - quarter, 1), pltpu.roll(x, quarter, 1))
    return x * cos + rot * sin_signed


def _mla_proj_kernel(x_ref, sh_ref, sc_ref, g1_ref, cos_ref, sin_ref,
                     wdq_ref, gq_ref, wuq_ref, wdkv_ref, gkv_ref, wukv_ref,
                     gqn_ref, gqr_ref, gkn_ref, gkr_ref, seg_ref,
                     q_ref, k_ref, v_ref):
    nh = q_ref.shape[0]
    dn = nh * MLA_NOPE
    dr = nh * MLA_ROPE
    kvl = gkv_ref.shape[-1]
    h = (_rms(x_ref[...], g1_ref[...]) * (1.0 + sc_ref[...]) + sh_ref[...]).astype(BF16)
    ql = _rms(_dot(h, wdq_ref[...]), gq_ref[...]).astype(BF16)
    q = _dot(ql, wuq_ref[...])
    kv = _dot(h, wdkv_ref[...])
    ckv = _rms(kv[:, :kvl], gkv_ref[...]).astype(BF16)
    kvu = _dot(ckv, wukv_ref[...])

    cos = cos_ref[...]
    sin = sin_ref[...]
    reps = dr // LANES
    cos_q = jnp.concatenate([cos] * reps, axis=1)
    sin_q = jnp.concatenate([sin] * reps, axis=1)

    qr = q[:, dn:]
    ms = jnp.dot(qr * qr, seg_ref[...], preferred_element_type=F32, precision=lax.Precision.HIGHEST)
    qr = qr * lax.rsqrt(ms + EPS) * gqr_ref[...]
    qr = _rope(qr, cos_q, sin_q) * MLA_SCALE

    kr = kv[:, kvl:]
    kr = _rms(kr, gkr_ref[...])
    kr = _rope(kr, cos, sin)[:, :MLA_ROPE]

    for hh in range(nh):
        qn = _rms(q[:, hh * MLA_NOPE:(hh + 1) * MLA_NOPE], gqn_ref[...]) * MLA_SCALE
        qrh = qr[:, hh * MLA_ROPE:(hh + 1) * MLA_ROPE]
        q_ref[hh] = jnp.concatenate([qn, qrh], axis=1).astype(q_ref.dtype)
        kn = _rms(kvu[:, hh * MLA_NOPE:(hh + 1) * MLA_NOPE], gkn_ref[...])
        k_ref[hh] = jnp.concatenate([kn, kr], axis=1).astype(k_ref.dtype)
        v_ref[hh, :, :MLA_V] = kvu[:, dn + hh * MLA_V:dn + (hh + 1) * MLA_V].astype(v_ref.dtype)
        v_ref[hh, :, MLA_V:] = jnp.ones((kvu.shape[0], MLA_V), v_ref.dtype)


def _mla_proj(x, mod, lseg, g1, cos_t, sin_t, w):
    n, d = x.shape
    tm = _pick_tile(lseg, cap=256)
    nrope = cos_t.shape[0] // tm
    full = lambda a: pl.BlockSpec(a.shape, lambda i: (0,) * a.ndim)
    weights = [w["wdq"], w["gq"], w["wuq"], w["wdkv"], w["gkv"], w["wukv"],
               w["gqn"], w["gqr"], w["gkn"], w["gkr"], w["seg"]]
    return pl.pallas_call(
        _mla_proj_kernel,
        grid=(n // tm,),
        in_specs=[
            pl.BlockSpec((tm, d), lambda i: (i, 0)),
            _mod_spec(0, tm, lseg, d),
            _mod_spec(1, tm, lseg, d),
            pl.BlockSpec((1, d), lambda i: (0, 0)),
            pl.BlockSpec((tm, LANES), lambda i: (i % nrope, 0)),
            pl.BlockSpec((tm, LANES), lambda i: (i % nrope, 0)),
        ] + [full(a) for a in weights],
        out_specs=[
            pl.BlockSpec((MLA_HEADS, tm, MLA_QK), lambda i: (0, i, 0)),
            pl.BlockSpec((MLA_HEADS, tm, MLA_QK), lambda i: (0, i, 0)),
            pl.BlockSpec((MLA_HEADS, tm, 2 * MLA_V), lambda i: (0, i, 0)),
        ],
        out_shape=[
            jax.ShapeDtypeStruct((MLA_HEADS, n, MLA_QK), BF16),
            jax.ShapeDtypeStruct((MLA_HEADS, n, MLA_QK), BF16),
            jax.ShapeDtypeStruct((MLA_HEADS, n, 2 * MLA_V), BF16),
        ],
        compiler_params=_cparams(("parallel",)),
        name="mla_proj",
    )(x, mod, mod, g1, cos_t, sin_t, *weights)


def _attn_lat_kernel(q_ref, kc_ref, kl_ref, vc_ref, vl_ref, o_ref, *, streams):
    tq = q_ref.shape[0] // streams

    def scores(s):
        q = q_ref[s * tq:(s + 1) * tq, :]
        return _dot_nt(q, kc_ref[...]), _dot_nt(q, kl_ref[...])

    nxt = scores(0)
    for s in range(streams):
        sc, sl = nxt
        if s + 1 < streams:
            nxt = scores(s + 1)
        m = jnp.maximum(jnp.max(sc, axis=-1, keepdims=True), jnp.max(sl, axis=-1, keepdims=True))
        pc = jnp.exp2(sc - m).astype(BF16)
        pl_ = jnp.exp2(sl - m).astype(BF16)
        o = _dot(pc, vc_ref[...]) + _dot(pl_, vl_ref[...])
        o_ref[s * tq:(s + 1) * tq, :] = (o[:, :MLA_V] / o[:, MLA_V:]).astype(o_ref.dtype)


def _attn_ctx_kernel(q_ref, kc_ref, vc_ref, o_ref):
    sc = _dot_nt(q_ref[...], kc_ref[...])
    m = jnp.max(sc, axis=-1, keepdims=True)
    o = _dot(jnp.exp2(sc - m).astype(BF16), vc_ref[...])
    o_ref[...] = (o[:, :MLA_V] / o[:, MLA_V:]).astype(o_ref.dtype)


def _attn_lat(q, kl, vl, kc, vc, nb):
    nh, n, _ = q.shape
    l = n // nb
    lc = kc.shape[1] // nb
    streams = 4
    tq = _pick_tile(l, cap=streams * 256)
    nq = l // tq
    return pl.pallas_call(
        functools.partial(_attn_lat_kernel, streams=streams),
        grid=(nb, nh, nq),
        in_specs=[
            pl.BlockSpec((None, tq, MLA_QK), lambda b, h, i: (h, b * nq + i, 0)),
            pl.BlockSpec((None, lc, MLA_QK), lambda b, h, i: (h, b, 0)),
            pl.BlockSpec((None, l, MLA_QK), lambda b, h, i: (h, b, 0)),
            pl.BlockSpec((None, lc, 2 * MLA_V), lambda b, h, i: (h, b, 0)),
            pl.BlockSpec((None, l, 2 * MLA_V), lambda b, h, i: (h, b, 0)),
        ],
        out_specs=pl.BlockSpec((tq, MLA_V), lambda b, h, i: (b * nq + i, h)),
        out_shape=jax.ShapeDtypeStruct((n, nh * MLA_V), BF16),
        compiler_params=_cparams(("parallel", "parallel", "arbitrary")),
        name="attn_lat",
    )(q, kc, kl, vc, vl)


def _attn_ctx(q, kc, vc, nb):
    nh, n, _ = q.shape
    lc = n // nb
    return pl.pallas_call(
        _attn_ctx_kernel,
        grid=(nb, nh),
        in_specs=[
            pl.BlockSpec((None, lc, MLA_QK), lambda b, h: (h, b, 0)),
            pl.BlockSpec((None, lc, MLA_QK), lambda b, h: (h, b, 0)),
            pl.BlockSpec((None, lc, 2 * MLA_V), lambda b, h: (h, b, 0)),
        ],
        out_specs=pl.BlockSpec((lc, MLA_V), lambda b, h: (b, h)),
        out_shape=jax.ShapeDtypeStruct((n, nh * MLA_V), BF16),
        compiler_params=_cparams(("parallel", "parallel")),
        name="attn_ctx",
    )(q, kc, vc)


def _rope_tables(n_tok):
    rows = n_tok // GRID_W
    row = jnp.repeat(jnp.arange(rows, dtype=F32), GRID_W)
    col = jnp.tile(jnp.arange(GRID_W, dtype=F32), rows)
    n_freq = MLA_ROPE // 4
    freqs = ROPE_THETA ** (-jnp.arange(n_freq, dtype=F32) / n_freq)
    ang_r = row[:, None] * freqs[None, :]
    ang_c = col[:, None] * freqs[None, :]
    ang = jnp.concatenate([ang_r, ang_r, ang_c, ang_c], axis=-1)
    sign = jnp.tile(jnp.repeat(jnp.array([-1.0, 1.0], F32), n_freq), 2)
    cos = jnp.cos(ang)
    sin = jnp.sin(ang) * sign
    return jnp.concatenate([cos, cos], axis=1), jnp.concatenate([sin, sin], axis=1)


def _mla_weights(w_dq, g_q, w_uq, w_dkv, g_kv, w_ukv, g_qk, w_o):
    nh = MLA_HEADS
    qlora = w_dq.shape[1]
    kvl = g_kv.shape[0]
    uq = w_uq.reshape(qlora, nh, MLA_QK)
    wuq = jnp.concatenate([uq[:, :, :MLA_NOPE].reshape(qlora, nh * MLA_NOPE),
                           uq[:, :, MLA_NOPE:].reshape(qlora, nh * MLA_ROPE)], axis=1)
    ukv = w_ukv.reshape(kvl, nh, MLA_NOPE + MLA_V)
    wukv = jnp.concatenate([ukv[:, :, :MLA_NOPE].reshape(kvl, nh * MLA_NOPE),
                            ukv[:, :, MLA_NOPE:].reshape(kvl, nh * MLA_V)], axis=1)
    wdkv = jnp.concatenate([w_dkv, w_dkv[:, kvl:]], axis=1)
    dr = nh * MLA_ROPE
    seg_id = jnp.arange(dr) // MLA_ROPE
    seg = (seg_id[:, None] == seg_id[None, :]).astype(F32) / MLA_ROPE
    return {
        "wdq": w_dq.astype(BF16), "gq": g_q[None, :], "wuq": wuq.astype(BF16),
        "wdkv": wdkv.astype(BF16), "gkv": g_kv[None, :], "wukv": wukv.astype(BF16),
        "gqn": g_qk[0:1, :MLA_NOPE], "gqr": jnp.tile(g_qk[0:1, MLA_NOPE:], (1, nh)),
        "gkn": g_qk[1:2, :MLA_NOPE], "gkr": jnp.tile(g_qk[1:2, MLA_NOPE:], (1, 2)),
        "seg": seg, "wo": w_o.astype(BF16),
    }


def _mla_layer(lat, cx, mod_l, mod_c, nb, g1, wts, need_ctx):
    l = lat.shape[0] // nb
    lc = cx.shape[0] // nb
    w = _mla_weights(*wts)
    cos_t, sin_t = _rope_tables(l)
    tmc = _pick_tile(lc, cap=256)
    ql, kl, vl = _mla_proj(lat, mod_l, l, g1, cos_t, sin_t, w)
    qc, kc, vc = _mla_proj(cx, mod_c, lc, g1, jnp.ones((tmc, LANES), F32), jnp.zeros((tmc, LANES), F32), w)
    o = _attn_lat(ql, kl, vl, kc, vc, nb)
    y = _proj(o, w["wo"], l)
    yc = None
    if need_ctx:
        yc = _proj(_attn_ctx(qc, kc, vc, nb), w["wo"], lc)
    return y, yc


def _s5_kernel(xf_ref, xr_ref, sh_ref, sc_ref, g1_ref, dd_ref, wb_ref, wc_ref, are_ref, aim_ref, st0_ref,
               yf_ref, yr_ref, st_ref, sf_s, sr_s, *, pitch):
    nb, t, d = xf_ref.shape
    nt = st_ref.shape[2]
    nre = nt // 2
    nv = nre // SUBLANES
    nj = wb_ref.shape[1]
    kblocks = d // LANES
    per_kb = (nj // 2) // kblocks
    nq = wc_ref.shape[1]
    tiles_q = nre // nq
    step = pl.program_id(0)

    @pl.when(step == 0)
    def _():
        st_ref[...] = st0_ref[...]

    def prep(x_ref):
        return _rms(x_ref[...], g1_ref[...]) * (1.0 + sc_ref[...]) + sh_ref[...]

    hs = (prep(xf_ref), prep(xr_ref))
    bufs = (sf_s, sr_s)

    def slab(b, n):
        return (b * nt + n) * pitch

    for dr in range(2):
        hb = hs[dr].reshape(nb * t, d).astype(BF16)
        for j in range(nj):
            kb = (j % (nj // 2)) // per_kb
            r = _dot(hb[:, kb * LANES:(kb + 1) * LANES], wb_ref[dr, j])
            for b in range(nb):
                for half in range(MXU_N // LANES):
                    n = j * (MXU_N // LANES) + half
                    bufs[dr][pl.ds(slab(b, n), t), :] = r[b * t:(b + 1) * t, half * LANES:(half + 1) * LANES]

    consts = [[(are_ref[dr, v * SUBLANES:(v + 1) * SUBLANES, :], aim_ref[dr, v * SUBLANES:(v + 1) * SUBLANES, :])
               for v in range(nv)] for dr in range(2)]
    for b in range(nb):
        init = tuple(st_ref[b, dr, part * nre + v * SUBLANES:part * nre + (v + 1) * SUBLANES, :]
                     for dr in range(2) for v in range(nv) for part in range(2))

        def body(i, carry, b=b):
            out = []
            for dr in range(2):
                tt = i if dr == 0 else t - 1 - i
                for v in range(nv):
                    sre, sim = carry[(dr * nv + v) * 2], carry[(dr * nv + v) * 2 + 1]
                    are, aim = consts[dr][v]
                    row_re = slab(b, v * SUBLANES) + tt
                    row_im = slab(b, nre + v * SUBLANES) + tt
                    bre = bufs[dr][pl.ds(row_re, SUBLANES, stride=pitch), :]
                    bim = bufs[dr][pl.ds(row_im, SUBLANES, stride=pitch), :]
                    new_re = are * sre - aim * sim + bre
                    new_im = are * sim + aim * sre + bim
                    bufs[dr][pl.ds(row_re, SUBLANES, stride=pitch), :] = new_re
                    bufs[dr][pl.ds(row_im, SUBLANES, stride=pitch), :] = new_im
                    out += [new_re, new_im]
            return tuple(out)

        fin = lax.fori_loop(0, t, body, init)
        k = 0
        for dr in range(2):
            for v in range(nv):
                for part in range(2):
                    st_ref[b, dr, part * nre + v * SUBLANES:part * nre + (v + 1) * SUBLANES, :] = fin[k]
                    k += 1

    skip = hs[0] * dd_ref[...]
    outs = (yf_ref, yr_ref)
    for dr in range(2):
        for q in range(nq):
            rows = []
            for b in range(nb):
                pieces = [bufs[dr][pl.ds(slab(b, part * nre + q * tiles_q + i), t), :]
                          for part in range(2) for i in range(tiles_q)]
                rows.append(jnp.concatenate(pieces, axis=1))
            yq = _dot(jnp.concatenate(rows, axis=0).astype(BF16), wc_ref[dr, q])
            for b in range(nb):
                val = yq[b * t:(b + 1) * t, :]
                if dr == 0:
                    val = val + skip[b, :, q * LANES:(q + 1) * LANES]
                outs[dr][b, :, q * LANES:(q + 1) * LANES] = val


def _s5_scan(x3, mod3, g1, dd, wb, wc, are, aim, st0, t):
    nb, l, d = x3.shape
    nc = l // t
    pitch = t + SUBLANES
    nt = st0.shape[2]
    full = lambda a: pl.BlockSpec(a.shape, lambda c: (0,) * a.ndim)
    buf = pltpu.VMEM((nb * nt * pitch, LANES), F32)
    return pl.pallas_call(
        functools.partial(_s5_kernel, pitch=pitch),
        grid=(nc,),
        in_specs=[
            pl.BlockSpec((nb, t, d), lambda c: (0, c, 0)),
            pl.BlockSpec((nb, t, d), lambda c: (0, nc - 1 - c, 0)),
            pl.BlockSpec((nb, 1, d), lambda c: (0, 0, 0)),
            pl.BlockSpec((nb, 1, d), lambda c: (0, 0, 1)),
            full(g1), full(dd), full(wb), full(wc), full(are), full(aim), full(st0),
        ],
        out_specs=[
            pl.BlockSpec((nb, t, d), lambda c: (0, c, 0)),
            pl.BlockSpec((nb, t, d), lambda c: (0, nc - 1 - c, 0)),
            full(st0),
        ],
        out_shape=[
            jax.ShapeDtypeStruct((nb, l, d), F32),
            jax.ShapeDtypeStruct((nb, l, d), F32),
            jax.ShapeDtypeStruct(st0.shape, F32),
        ],
        scratch_shapes=[buf, buf],
        compiler_params=_cparams(("arbitrary",)),
        name="s5_scan",
    )(x3, x3, mod3, mod3, g1, dd, wb, wc, are, aim, st0)


def _s5_glu_kernel(yf_ref, yr_ref, w_ref, o_ref):
    d = o_ref.shape[-1]
    z = _dot(_gelu(yf_ref[...] + yr_ref[...]).astype(BF16), w_ref[...])
    o_ref[...] = z[:, :d] * _sigmoid(z[:, d:])


def _s5_glu(yf, yr, w, lseg):
    n, d = yf.shape
    tm = _pick_tile(lseg)
    return pl.pallas_call(
        _s5_glu_kernel,
        grid=(n // tm,),
        in_specs=[pl.BlockSpec((tm, d), lambda i: (i, 0)), pl.BlockSpec((tm, d), lambda i: (i, 0)),
                  pl.BlockSpec(w.shape, lambda i: (0, 0))],
        out_specs=pl.BlockSpec((tm, d), lambda i: (i, 0)),
        out_shape=jax.ShapeDtypeStruct((n, d), F32),
        compiler_params=_cparams(("parallel",)),
        name="s5_glu",
    )(yf, yr, w)


def _s5_weights(a_re, a_im, log_dt, b_re, b_im, c_re, c_im):
    ng, ns = a_re.shape[1:]
    gi = b_re.shape[-1]
    d = ng * gi
    lam = lax.complex(a_re.astype(F32), a_im.astype(F32))
    dt = jnp.exp(log_dt.astype(F32))[..., None]
    a_bar = jnp.exp(lam * dt)
    b_bar = ((a_bar - 1) / lam)[..., None] * lax.complex(b_re.astype(F32), b_im.astype(F32))
    eye = jnp.eye(ng, dtype=F32)

    def in_full(part):
        bt = jnp.swapaxes(part, -1, -2)
        return (bt[:, :, :, None, :] * eye[None, :, None, :, None]).reshape(2, d, ng * ns)

    def out_full(part):
        ct = jnp.swapaxes(part, -1, -2)
        return (ct[:, :, :, None, :] * eye[None, :, None, :, None]).reshape(2, ng * ns, d)

    ncol = ng * ns // MXU_N
    kblocks = d // LANES
    kb_of = jnp.arange(ncol) // (ncol // kblocks)

    def in_tiles(full):
        f = full.reshape(2, kblocks, LANES, ncol, MXU_N)
        return f[:, kb_of, :, jnp.arange(ncol), :]

    wb = jnp.concatenate([in_tiles(in_full(jnp.real(b_bar))), in_tiles(in_full(jnp.imag(b_bar)))], axis=0)
    wb = jnp.swapaxes(wb, 0, 1).astype(BF16)

    nq = d // LANES
    rows_q = ng * ns // nq

    def out_tiles(full):
        f = full.reshape(2, nq, rows_q, nq, LANES)
        return f[:, jnp.arange(nq), :, jnp.arange(nq), :]

    c_mat_re, c_mat_im = c_re.astype(F32), c_im.astype(F32)
    wc = jnp.concatenate([out_tiles(out_full(c_mat_re)), out_tiles(out_full(-c_mat_im))], axis=2)
    wc = jnp.swapaxes(wc, 0, 1).astype(BF16)
    are = jnp.real(a_bar).reshape(2, ng * ns // LANES, LANES)
    aim = jnp.imag(a_bar).reshape(2, ng * ns // LANES, LANES)
    return wb, wc, are, aim


def _s5_layer(lat, cx, mod_l, mod_c, nb, g1, a_re, a_im, log_dt, b_re, b_im, c_re, c_im, dvec, w_glu, need_ctx):
    d = lat.shape[1]
    l = lat.shape[0] // nb
    lc = cx.shape[0] // nb
    wb, wc, are, aim = _s5_weights(a_re, a_im, log_dt, b_re, b_im, c_re, c_im)
    t = _pick_tile(l, lc, cap=64)
    st0 = jnp.zeros((nb, 2, 2 * are.shape[1], LANES), F32)
    dd = dvec[None, :].astype(F32)
    ycf, ycr, st = _s5_scan(cx.reshape(nb, lc, d), mod_c, g1, dd, wb, wc, are, aim, st0, t)
    yf, yr, _ = _s5_scan(lat.reshape(nb, l, d), mod_l, g1, dd, wb, wc, are, aim, st, t)
    wg = w_glu.astype(BF16)
    y = _s5_glu(yf.reshape(nb * l, d), yr.reshape(nb * l, d), wg, l)
    yc = _s5_glu(ycf.reshape(nb * lc, d), ycr.reshape(nb * lc, d), wg, lc) if need_ctx else None
    return y, yc


def _lru_in_kernel(x_ref, sh_ref, sc_ref, g1_ref, w_ref, xw_ref, gl_ref):
    wd = xw_ref.shape[-1]
    h = (_rms(x_ref[...], g1_ref[...]) * (1.0 + sc_ref[...]) + sh_ref[...]).astype(BF16)
    z = _dot(h, w_ref[...])
    gl_ref[...] = _gelu(z[:, :wd]).astype(gl_ref.dtype)
    xw_ref[...] = z[:, wd:]


def _lru_in(x, mod, lseg, g1, w_in):
    n, d = x.shape
    wd = w_in.shape[1] // 2
    tm = _pick_tile(lseg)
    return pl.pallas_call(
        _lru_in_kernel,
        grid=(n // tm,),
        in_specs=[pl.BlockSpec((tm, d), lambda i: (i, 0)), _mod_spec(0, tm, lseg, d), _mod_spec(1, tm, lseg, d),
                  pl.BlockSpec((1, d), lambda i: (0, 0)), pl.BlockSpec(w_in.shape, lambda i: (0, 0))],
        out_specs=[pl.BlockSpec((tm, wd), lambda i: (i, 0)), pl.BlockSpec((tm, wd), lambda i: (i, 0))],
        out_shape=[jax.ShapeDtypeStruct((n, wd), F32), jax.ShapeDtypeStruct((n, wd), BF16)],
        compiler_params=_cparams(("parallel",)),
        name="lru_in",
    )(x, mod, mod, g1, w_in)


def _lru_scan_kernel(xf_ref, xfp_ref, xfn_ref, xr_ref, xrp_ref, xrn_ref,
                     cw_ref, cb_ref, wg_ref, bg_ref, sp_ref, st0_ref,
                     hf_ref, hr_ref, st_ref, ext_s, a_s, b_s, *, pitch):
    nb, t, wd = xf_ref.shape
    nblk = wd // LANES
    halo = SUBLANES
    step = pl.program_id(0)
    last = pl.num_programs(0) - 1

    @pl.when(step == 0)
    def _():
        st_ref[...] = st0_ref[...]
        a_s[...] = jnp.zeros_like(a_s)
        b_s[...] = jnp.zeros_like(b_s)

    def coeffs(dr, x_ref, prev_ref, next_ref, at_start, at_end):
        ext_s[:, 0:halo, :] = jnp.where(at_start, 0.0, prev_ref[...])
        ext_s[:, halo:halo + t, :] = x_ref[...]
        ext_s[:, halo + t:2 * halo + t, :] = jnp.where(at_end, 0.0, next_ref[...])
        xr = cb_ref[...]
        for tap in range(cw_ref.shape[0]):
            xr = xr + cw_ref[tap:tap + 1, :] * ext_s[:, halo - 1 + tap:halo - 1 + tap + t, :]
        for n in range(nblk):
            xb = xr[:, :, n * LANES:(n + 1) * LANES].reshape(nb * t, LANES)
            g = _dot(xb.astype(BF16), wg_ref[dr, n]) + bg_ref[dr, n]
            r = _sigmoid(g[:, :LANES])
            ig = _sigmoid(g[:, LANES:])
            log_a = -LRU_C * r * sp_ref[dr, n]
            a = jnp.exp(log_a)
            bc = jnp.sqrt(1.0 - jnp.exp(2.0 * log_a)) * (ig * xb)
            for b in range(nb):
                a_s[dr, n, pl.ds(b * pitch, t), :] = a[b * t:(b + 1) * t]
                b_s[dr, n, pl.ds(b * pitch, t), :] = bc[b * t:(b + 1) * t]

    coeffs(0, xf_ref, xfp_ref, xfn_ref, step == 0, step == last)
    coeffs(1, xr_ref, xrp_ref, xrn_ref, step == last, step == 0)

    init = tuple(st_ref[dr, n] for dr in range(2) for n in range(nblk))

    def body(i, carry):
        out = []
        for dr in range(2):
            tt = i if dr == 0 else t - 1 - i
            for n in range(nblk):
                a = a_s[dr, n, pl.ds(tt, SUBLANES, stride=pitch), :]
                bc = b_s[dr, n, pl.ds(tt, SUBLANES, stride=pitch), :]
                hnew = a * carry[dr * nblk + n] + bc
                b_s[dr, n, pl.ds(tt, SUBLANES, stride=pitch), :] = hnew
                out.append(hnew)
        return tuple(out)

    fin = lax.fori_loop(0, t, body, init)
    k = 0
    for dr in range(2):
        for n in range(nblk):
            st_ref[dr, n] = fin[k]
            k += 1
    outs = (hf_ref, hr_ref)
    for dr in range(2):
        for n in range(nblk):
            for b in range(nb):
                outs[dr][b, :, n * LANES:(n + 1) * LANES] = b_s[dr, n, pl.ds(b * pitch, t), :]


def _lru_scan(xw3, cw, cb, wg, bg, sp, st0, t):
    nb, l, wd = xw3.shape
    nc = l // t
    nblk = wd // LANES
    hb = t // SUBLANES
    nhb = l // SUBLANES
    pitch = t + SUBLANES
    full = lambda a: pl.BlockSpec(a.shape, lambda c: (0,) * a.ndim)
    chunk = lambda f: pl.BlockSpec((nb, t, wd), lambda c: (0, f(c), 0))
    halo_prev = lambda f: pl.BlockSpec((nb, SUBLANES, wd), lambda c: (0, jnp.maximum(f(c) * hb - 1, 0), 0))
    halo_next = lambda f: pl.BlockSpec((nb, SUBLANES, wd), lambda c: (0, jnp.minimum((f(c) + 1) * hb, nhb - 1), 0))
    fwd = lambda c: c
    rev = lambda c: nc - 1 - c
    coef = pltpu.VMEM((2, nblk, SUBLANES * pitch, LANES), F32)
    return pl.pallas_call(
        functools.partial(_lru_scan_kernel, pitch=pitch),
        grid=(nc,),
        in_specs=[chunk(fwd), halo_prev(fwd), halo_next(fwd), chunk(rev), halo_prev(rev), halo_next(rev),
                  full(cw), full(cb), full(wg), full(bg), full(sp), full(st0)],
        out_specs=[chunk(fwd), chunk(rev), full(st0)],
        out_shape=[jax.ShapeDtypeStruct((nb, l, wd), F32), jax.ShapeDtypeStruct((nb, l, wd), F32),
                   jax.ShapeDtypeStruct(st0.shape, F32)],
        scratch_shapes=[pltpu.VMEM((nb, t + 2 * SUBLANES, wd), F32), coef, coef],
        compiler_params=_cparams(("arbitrary",)),
        name="lru_scan",
    )(xw3, xw3, xw3, xw3, xw3, xw3, cw, cb, wg, bg, sp, st0)


def _lru_out_kernel(gl_ref, hf_ref, hr_ref, w_ref, o_ref):
    u = gl_ref[...].astype(F32) * (hf_ref[...] + hr_ref[...])
    o_ref[...] = _dot(u.astype(BF16), w_ref[...])


def _lru_out(gl, hf, hr, w, lseg):
    n, wd = gl.shape
    d = w.shape[1]
    tm = _pick_tile(lseg)
    row = lambda: pl.BlockSpec((tm, wd), lambda i: (i, 0))
    return pl.pallas_call(
        _lru_out_kernel,
        grid=(n // tm,),
        in_specs=[row(), row(), row(), pl.BlockSpec(w.shape, lambda i: (0, 0))],
        out_specs=pl.BlockSpec((tm, d), lambda i: (i, 0)),
        out_shape=jax.ShapeDtypeStruct((n, d), F32),
        compiler_params=_cparams(("parallel",)),
        name="lru_out",
    )(gl, hf, hr, w)


def _lru_layer(lat, cx, mod_l, mod_c, nb, g1, w_in, conv_w, conv_b, w_gate, b_gate, lam, w_out, need_ctx):
    l = lat.shape[0] // nb
    lc = cx.shape[0] // nb
    wd = w_in.shape[1] // 2
    nblk = wd // LANES
    w_in_b = w_in.astype(BF16)
    xw, gl = _lru_in(lat, mod_l, l, g1, w_in_b)
    xwc, glc = _lru_in(cx, mod_c, lc, g1, w_in_b)
    wg = jnp.concatenate([w_gate[:, 0], w_gate[:, 1]], axis=-1).astype(BF16)
    bgate = b_gate.astype(F32).reshape(2, 2, nblk, 1, LANES)
    bg = jnp.concatenate([bgate[:, 0], bgate[:, 1]], axis=-1)
    sp = jax.nn.softplus(-lam.astype(F32)).reshape(2, nblk, 1, LANES)
    cw = conv_w.astype(F32)
    cb = conv_b.astype(F32)[None, :]
    t = _pick_tile(l, lc, cap=64)
    st0 = jnp.zeros((2, nblk, SUBLANES, LANES), F32)
    hcf, hcr, st = _lru_scan(xwc.reshape(nb, lc, wd), cw, cb, wg, bg, sp, st0, t)
    hf, hr, _ = _lru_scan(xw.reshape(nb, l, wd), cw, cb, wg, bg, sp, st, t)
    wo = w_out.astype(BF16)
    y = _lru_out(gl, hf.reshape(nb * l, wd), hr.reshape(nb * l, wd), wo, l)
    yc = _lru_out(glc, hcf.reshape(nb * lc, wd), hcr.reshape(nb * lc, wd), wo, lc) if need_ctx else None
    return y, yc


def kernel(x, c, ctx, c_ctx, ada_w, ada_b, norm_g, mla_w_dq, mla_g_q, mla_w_uq, mla_w_dkv, mla_g_kv, mla_w_ukv, mla_g_qk, mla_w_o, s5_a_re, s5_a_im, s5_log_dt, s5_b_re, s5_b_im, s5_c_re, s5_c_im, s5_d, s5_w_glu, lru_w_in, lru_conv_w, lru_conv_b, lru_w_gate, lru_b_gate, lru_lambda, lru_w_out, mlp_w1, mlp_w2):
    nb, l, d = x.shape
    lc = ctx.shape[1]
    depth = ada_w.shape[0]
    assert nb < SUBLANES
    cc = jnp.concatenate([c, c_ctx[None, :], jnp.zeros((SUBLANES - nb - 1, d), F32)], axis=0)
    mod = _ada_mod(cc, ada_w, ada_b)
    lat = x.reshape(nb * l, d)
    cx = ctx.reshape(nb * lc, d)
    for i in range(depth):
        need_ctx = i < depth - 1
        mod_l = mod[i, :nb, None, :]
        mod_c = jnp.broadcast_to(mod[i, nb][None, None, :], (nb, 1, mod.shape[-1]))
        g1 = norm_g[i, 0][None, :]
        g2 = norm_g[i, 1][None, :]
        kind, j = i % N_MIXERS, i // N_MIXERS
        if kind == 0:
            y, yc = _mla_layer(lat, cx, mod_l, mod_c, nb, g1,
                               (mla_w_dq[j], mla_g_q[j], mla_w_uq[j], mla_w_dkv[j], mla_g_kv[j], mla_w_ukv[j],
                                mla_g_qk[j], mla_w_o[j]), need_ctx)
        elif kind == 1:
            y, yc = _s5_layer(lat, cx, mod_l, mod_c, nb, g1, s5_a_re[j], s5_a_im[j], s5_log_dt[j], s5_b_re[j],
                              s5_b_im[j], s5_c_re[j], s5_c_im[j], s5_d[j], s5_w_glu[j], need_ctx)
        else:
            y, yc = _lru_layer(lat, cx, mod_l, mod_c, nb, g1, lru_w_in[j], lru_conv_w[j], lru_conv_b[j],
                               lru_w_gate[j], lru_b_gate[j], lru_lambda[j], lru_w_out[j], need_ctx)
        w1 = mlp_w1[i].astype(BF16)
        w2 = mlp_w2[i].astype(BF16)
        lat = _mlp(lat, y, mod_l, l, g2, w1, w2)
        if need_ctx:
            cx = _mlp(cx, yc, mod_c, lc, g2, w1, w2)
    return lat.reshape(nb, l, d)
```

```python
import functools
import math

import jax
import jax.numpy as jnp
from jax import lax
from jax.experimental import pallas as pl
from jax.experimental.pallas import tpu as pltpu

F32 = jnp.float32
BF16 = jnp.bfloat16

EPS = 1e-6
ROPE_THETA = 10000.0
GRID_W = 64
N_MIXERS = 3

MLA_HEADS = 8
MLA_NOPE = 128
MLA_ROPE = 64
MLA_V = 128
MLA_QK = MLA_NOPE + MLA_ROPE
MLA_SCALE = math.log2(math.e) / math.sqrt(MLA_QK)

S5_GROUP = 16
S5_STATE = 64
LRU_BLOCKS = 10
LRU_BW = 128
LRU_C = 8.0

LANES = 128
SUBLANES = 8
MXU_N = 256
VMEM_LIMIT = 56 * 1024 * 1024


def _cparams(sem):
    return pltpu.CompilerParams(dimension_semantics=sem, vmem_limit_bytes=VMEM_LIMIT)


def _dot(a, b):
    return jnp.dot(a, b, preferred_element_type=F32)


def _dot_nt(a, b):
    return lax.dot_general(a, b, (((1,), (1,)), ((), ())), preferred_element_type=F32)


def _rms(x, g):
    inv = lax.rsqrt(jnp.mean(x * x, axis=-1, keepdims=True) + EPS)
    return x * inv * g


def _gelu(x):
    c = math.sqrt(2.0 / math.pi)
    return x * (0.5 * (1.0 + jnp.tanh(c * (x + 0.044715 * (x * x * x)))))


def _sigmoid(x):
    return 1.0 / (1.0 + jnp.exp(-x))


def _sigmoid_tanh(x):
    return 0.5 * jnp.tanh(0.5 * x) + 0.5


def _pick_tile(*lens, cap=512):
    t = cap
    while any(n % t for n in lens):
        t //= 2
    assert t >= SUBLANES
    return t


def _ada_kernel(c_ref, w_ref, b_ref, o_ref):
    c = c_ref[...]
    s = c * _sigmoid(c)
    w = w_ref[...]
    s_hi = s.astype(BF16)
    s_lo = (s - s_hi.astype(F32)).astype(BF16)
    w_hi = w.astype(BF16)
    w_lo = (w - w_hi.astype(F32)).astype(BF16)
    o_ref[...] = _dot(s_hi, w_hi) + _dot(s_lo, w_hi) + _dot(s_hi, w_lo) + b_ref[...]


def _ada_mod(cc, ada_w, ada_b):
    depth, d, d6 = ada_w.shape
    nk = d6 // d
    return pl.pallas_call(
        _ada_kernel,
        grid=(depth, nk),
        in_specs=[
            pl.BlockSpec((SUBLANES, d), lambda l, k: (0, 0)),
            pl.BlockSpec((None, d, d), lambda l, k: (l, 0, k)),
            pl.BlockSpec((None, 1, d), lambda l, k: (l, 0, k)),
        ],
        out_specs=pl.BlockSpec((None, SUBLANES, d), lambda l, k: (l, 0, k)),
        out_shape=jax.ShapeDtypeStruct((depth, SUBLANES, d6), F32),
        compiler_params=_cparams(("parallel", "parallel")),
        name="ada_mod",
    )(cc, ada_w, ada_b.reshape(depth, 1, d6))


def _mod_spec(k, tm, lseg, d, tile_of=lambda i: i):
    return pl.BlockSpec((None, 1, d), lambda i, *_: ((tile_of(i) * tm) // lseg, 0, k))


def _mlp_kernel(x_ref, y_ref, ga_ref, shm_ref, scm_ref, gm_ref, g2_ref, w1_ref, w2_ref,
                o_ref, w1_s, w2_s):
    step = pl.program_id(0)
    nj = w1_s.shape[0]

    @pl.when(step < nj)
    def _():
        w1_s[step] = w1_ref[...].astype(BF16)
        w2_s[step] = w2_ref[...].astype(BF16)

    @pl.when(step >= nj)
    def _():
        x1 = x_ref[...] + ga_ref[...] * y_ref[...].astype(F32)
        h = (_rms(x1, g2_ref[...]) * (1.0 + scm_ref[...]) + shm_ref[...]).astype(BF16)
        acc = jnp.zeros_like(x1)
        for c in range(nj):
            t = jnp.maximum(_dot(h, w1_s[c]), 0.0)
            acc = acc + _dot((t * t).astype(BF16), w2_s[c])
        o_ref[...] = x1 + gm_ref[...] * acc


def _mlp(x, y, mod, lseg, g2, w1_all, w2_all, layer):
    n, d = x.shape
    dff = w1_all.shape[2]
    tm = _pick_tile(lseg)
    tf = min(dff, 512)
    nj = dff // tf
    tile = lambda s: jnp.maximum(s - nj, 0)
    wblk = lambda s: jnp.minimum(s, nj - 1)
    row = lambda: pl.BlockSpec((tm, d), lambda s: (tile(s), 0))
    return pl.pallas_call(
        _mlp_kernel,
        grid=(nj + n // tm,),
        in_specs=[
            row(), row(),
            _mod_spec(2, tm, lseg, d, tile),
            _mod_spec(3, tm, lseg, d, tile),
            _mod_spec(4, tm, lseg, d, tile),
            _mod_spec(5, tm, lseg, d, tile),
            pl.BlockSpec((1, d), lambda s: (0, 0)),
            pl.BlockSpec((None, d, tf), lambda s: (layer, 0, wblk(s))),
            pl.BlockSpec((None, tf, d), lambda s: (layer, wblk(s), 0)),
        ],
        out_specs=row(),
        out_shape=jax.ShapeDtypeStruct((n, d), F32),
        scratch_shapes=[pltpu.VMEM((nj, d, tf), BF16), pltpu.VMEM((nj, tf, d), BF16)],
        compiler_params=_cparams(("arbitrary",)),
        name="mlp",
    )(x, y, mod, mod, mod, mod, g2, w1_all, w2_all)


def _proj_kernel(a_ref, w_ref, o_ref):
    o_ref[...] = _dot(a_ref[...].astype(BF16), w_ref[...])


def _proj(a, w, lseg):
    n, k = a.shape
    dout = w.shape[1]
    tm = _pick_tile(lseg)
    return pl.pallas_call(
        _proj_kernel,
        grid=(n // tm,),
        in_specs=[pl.BlockSpec((tm, k), lambda i: (i, 0)), pl.BlockSpec((k, dout), lambda i: (0, 0))],
        out_specs=pl.BlockSpec((tm, dout), lambda i: (i, 0)),
        out_shape=jax.ShapeDtypeStruct((n, dout), F32),
        compiler_params=_cparams(("parallel",)),
        name="proj",
    )(a, w)


def _rope(x, cos, sin_signed):
    width = x.shape[-1]
    quarter = MLA_ROPE // 4
    lane = lax.broadcasted_iota(jnp.int32, x.shape, 1)
    even_quarter = ((lane // quarter) % 2) == 0
    rot = jnp.where(even_quarter, pltpu.roll(x, width - quarter, 1), pltpu.roll(x, quarter, 1))
    return x * cos + rot * sin_signed


def _mla_proj_kernel(x_ref, sh_ref, sc_ref, g1_ref, cos_ref, sin_ref,
                     wdq_ref, gq_ref, wuq_ref, wdkv_ref, gkv_ref, wukv_ref,
                     gqn_ref, gqr_ref, gkn_ref, gkr_ref, seg_ref,
                     q_ref, k_ref, v_ref):
    nh = q_ref.shape[0]
    dn = nh * MLA_NOPE
    dr = nh * MLA_ROPE
    kvl = gkv_ref.shape[-1]
    h = (_rms(x_ref[...], g1_ref[...]) * (1.0 + sc_ref[...]) + sh_ref[...]).astype(BF16)
    ql = _rms(_dot(h, wdq_ref[...]), gq_ref[...]).astype(BF16)
    q = _dot(ql, wuq_ref[...])
    kv = _dot(h, wdkv_ref[...])
    ckv = _rms(kv[:, :kvl], gkv_ref[...]).astype(BF16)
    kvu = _dot(ckv, wukv_ref[...])

    cos = cos_ref[...]
    sin = sin_ref[...]
    reps = dr // LANES
    cos_q = jnp.concatenate([cos] * reps, axis=1)
    sin_q = jnp.concatenate([sin] * reps, axis=1)

    qr = q[:, dn:]
    ms = jnp.dot(qr * qr, seg_ref[...], preferred_element_type=F32, precision=lax.Precision.HIGHEST)
    qr = qr * lax.rsqrt(ms + EPS) * gqr_ref[...]
    qr = _rope(qr, cos_q, sin_q) * MLA_SCALE

    kr = kv[:, kvl:]
    kr = _rms(kr, gkr_ref[...])
    kr = _rope(kr, cos, sin)[:, :MLA_ROPE]

    for hh in range(nh):
        qn = _rms(q[:, hh * MLA_NOPE:(hh + 1) * MLA_NOPE], gqn_ref[...]) * MLA_SCALE
        qrh = qr[:, hh * MLA_ROPE:(hh + 1) * MLA_ROPE]
        q_ref[hh] = jnp.concatenate([qn, qrh], axis=1).astype(q_ref.dtype)
        kn = _rms(kvu[:, hh * MLA_NOPE:(hh + 1) * MLA_NOPE], gkn_ref[...])
        k_ref[hh] = jnp.concatenate([kn, kr], axis=1).astype(k_ref.dtype)
        v_ref[hh, :, :MLA_V] = kvu[:, dn + hh * MLA_V:dn + (hh + 1) * MLA_V].astype(v_ref.dtype)
        v_ref[hh, :, MLA_V:] = jnp.ones((kvu.shape[0], MLA_V), v_ref.dtype)


def _mla_proj(x, mod, lseg, g1, cos_t, sin_t, w):
    n, d = x.shape
    tm = _pick_tile(lseg, cap=256)
    nrope = cos_t.shape[0] // tm
    full = lambda a: pl.BlockSpec(a.shape, lambda i: (0,) * a.ndim)
    weights = [w["wdq"], w["gq"], w["wuq"], w["wdkv"], w["gkv"], w["wukv"],
               w["gqn"], w["gqr"], w["gkn"], w["gkr"], w["seg"]]
    return pl.pallas_call(
        _mla_proj_kernel,
        grid=(n // tm,),
        in_specs=[
            pl.BlockSpec((tm, d), lambda i: (i, 0)),
            _mod_spec(0, tm, lseg, d),
            _mod_spec(1, tm, lseg, d),
            pl.BlockSpec((1, d), lambda i: (0, 0)),
            pl.BlockSpec((tm, LANES), lambda i: (i % nrope, 0)),
            pl.BlockSpec((tm, LANES), lambda i: (i % nrope, 0)),
        ] + [full(a) for a in weights],
        out_specs=[
            pl.BlockSpec((MLA_HEADS, tm, MLA_QK), lambda i: (0, i, 0)),
            pl.BlockSpec((MLA_HEADS, tm, MLA_QK), lambda i: (0, i, 0)),
            pl.BlockSpec((MLA_HEADS, tm, 2 * MLA_V), lambda i: (0, i, 0)),
        ],
        out_shape=[
            jax.ShapeDtypeStruct((MLA_HEADS, n, MLA_QK), BF16),
            jax.ShapeDtypeStruct((MLA_HEADS, n, MLA_QK), BF16),
            jax.ShapeDtypeStruct((MLA_HEADS, n, 2 * MLA_V), BF16),
        ],
        compiler_params=_cparams(("parallel",)),
        name="mla_proj",
    )(x, mod, mod, g1, cos_t, sin_t, *weights)


def _attn_lat_kernel(q_ref, kc_ref, kl_ref, vc_ref, vl_ref, o_ref, *, streams):
    tq = q_ref.shape[0] // streams

    def scores(s):
        q = q_ref[s * tq:(s + 1) * tq, :]
        return _dot_nt(q, kc_ref[...]), _dot_nt(q, kl_ref[...])

    nxt = scores(0)
    for s in range(streams):
        sc, sl = nxt
        if s + 1 < streams:
            nxt = scores(s + 1)
        m = jnp.maximum(jnp.max(sc, axis=-1, keepdims=True), jnp.max(sl, axis=-1, keepdims=True))
        pc = jnp.exp2(sc - m).astype(BF16)
        pl_ = jnp.exp2(sl - m).astype(BF16)
        o = _dot(pc, vc_ref[...]) + _dot(pl_, vl_ref[...])
        o_ref[s * tq:(s + 1) * tq, :] = (o[:, :MLA_V] / o[:, MLA_V:]).astype(o_ref.dtype)


def _attn_ctx_kernel(q_ref, kc_ref, vc_ref, o_ref):
    sc = _dot_nt(q_ref[...], kc_ref[...])
    m = jnp.max(sc, axis=-1, keepdims=True)
    o = _dot(jnp.exp2(sc - m).astype(BF16), vc_ref[...])
    o_ref[...] = (o[:, :MLA_V] / o[:, MLA_V:]).astype(o_ref.dtype)


def _attn_lat(q, kl, vl, kc, vc, nb):
    nh, n, _ = q.shape
    l = n // nb
    lc = kc.shape[1] // nb
    streams = 4
    tq = _pick_tile(l, cap=streams * 256)
    nq = l // tq
    return pl.pallas_call(
        functools.partial(_attn_lat_kernel, streams=streams),
        grid=(nb, nh, nq),
        in_specs=[
            pl.BlockSpec((None, tq, MLA_QK), lambda b, h, i: (h, b * nq + i, 0)),
            pl.BlockSpec((None, lc, MLA_QK), lambda b, h, i: (h, b, 0)),
            pl.BlockSpec((None, l, MLA_QK), lambda b, h, i: (h, b, 0)),
            pl.BlockSpec((None, lc, 2 * MLA_V), lambda b, h, i: (h, b, 0)),
            pl.BlockSpec((None, l, 2 * MLA_V), lambda b, h, i: (h, b, 0)),
        ],
        out_specs=pl.BlockSpec((tq, MLA_V), lambda b, h, i: (b * nq + i, h)),
        out_shape=jax.ShapeDtypeStruct((n, nh * MLA_V), BF16),
        compiler_params=_cparams(("parallel", "parallel", "arbitrary")),
        name="attn_lat",
    )(q, kc, kl, vc, vl)


def _attn_ctx(q, kc, vc, nb):
    nh, n, _ = q.shape
    lc = n // nb
    return pl.pallas_call(
        _attn_ctx_kernel,
        grid=(nb, nh),
        in_specs=[
            pl.BlockSpec((None, lc, MLA_QK), lambda b, h: (h, b, 0)),
            pl.BlockSpec((None, lc, MLA_QK), lambda b, h: (h, b, 0)),
            pl.BlockSpec((None, lc, 2 * MLA_V), lambda b, h: (h, b, 0)),
        ],
        out_specs=pl.BlockSpec((lc, MLA_V), lambda b, h: (b, h)),
        out_shape=jax.ShapeDtypeStruct((n, nh * MLA_V), BF16),
        compiler_params=_cparams(("parallel", "parallel")),
        name="attn_ctx",
    )(q, kc, vc)


def _rope_tables(n_tok):
    rows = n_tok // GRID_W
    row = jnp.repeat(jnp.arange(rows, dtype=F32), GRID_W)
    col = jnp.tile(jnp.arange(GRID_W, dtype=F32), rows)
    n_freq = MLA_ROPE // 4
    freqs = ROPE_THETA ** (-jnp.arange(n_freq, dtype=F32) / n_freq)
    ang_r = row[:, None] * freqs[None, :]
    ang_c = col[:, None] * freqs[None, :]
    ang = jnp.concatenate([ang_r, ang_r, ang_c, ang_c], axis=-1)
    sign = jnp.tile(jnp.repeat(jnp.array([-1.0, 1.0], F32), n_freq), 2)
    cos = jnp.cos(ang)
    sin = jnp.sin(ang) * sign
    return jnp.concatenate([cos, cos], axis=1), jnp.concatenate([sin, sin], axis=1)


def _mla_weights(w_dq, g_q, w_uq, w_dkv, g_kv, w_ukv, g_qk, w_o):
    nh = MLA_HEADS
    qlora = w_dq.shape[1]
    kvl = g_kv.shape[0]
    uq = w_uq.reshape(qlora, nh, MLA_QK)
    wuq = jnp.concatenate([uq[:, :, :MLA_NOPE].reshape(qlora, nh * MLA_NOPE),
                           uq[:, :, MLA_NOPE:].reshape(qlora, nh * MLA_ROPE)], axis=1)
    ukv = w_ukv.reshape(kvl, nh, MLA_NOPE + MLA_V)
    wukv = jnp.concatenate([ukv[:, :, :MLA_NOPE].reshape(kvl, nh * MLA_NOPE),
                            ukv[:, :, MLA_NOPE:].reshape(kvl, nh * MLA_V)], axis=1)
    wdkv = jnp.concatenate([w_dkv, w_dkv[:, kvl:]], axis=1)
    dr = nh * MLA_ROPE
    seg_id = jnp.arange(dr) // MLA_ROPE
    seg = (seg_id[:, None] == seg_id[None, :]).astype(F32) / MLA_ROPE
    return {
        "wdq": w_dq.astype(BF16), "gq": g_q[None, :], "wuq": wuq.astype(BF16),
        "wdkv": wdkv.astype(BF16), "gkv": g_kv[None, :], "wukv": wukv.astype(BF16),
        "gqn": g_qk[0:1, :MLA_NOPE], "gqr": jnp.tile(g_qk[0:1, MLA_NOPE:], (1, nh)),
        "gkn": g_qk[1:2, :MLA_NOPE], "gkr": jnp.tile(g_qk[1:2, MLA_NOPE:], (1, 2)),
        "seg": seg, "wo": w_o.astype(BF16),
    }


def _mla_layer(lat, cx, mod_l, mod_c, nb, g1, wts, need_ctx):
    l = lat.shape[0] // nb
    lc = cx.shape[0] // nb
    w = _mla_weights(*wts)
    cos_t, sin_t = _rope_tables(l)
    tmc = _pick_tile(lc, cap=256)
    ql, kl, vl = _mla_proj(lat, mod_l, l, g1, cos_t, sin_t, w)
    qc, kc, vc = _mla_proj(cx, mod_c, lc, g1, jnp.ones((tmc, LANES), F32), jnp.zeros((tmc, LANES), F32), w)
    o = _attn_lat(ql, kl, vl, kc, vc, nb)
    y = _proj(o, w["wo"], l)
    yc = None
    if need_ctx:
        yc = _proj(_attn_ctx(qc, kc, vc, nb), w["wo"], lc)
    return y, yc


def _s5_kernel(xf_ref, xr_ref, sh_ref, sc_ref, g1_ref, dd_ref, wb_ref, wc_ref, are_ref, aim_ref, st0_ref,
               yf_ref, yr_ref, st_ref, sf_s, sr_s, *, pitch):
    nb, t, d = xf_ref.shape
    nt = st_ref.shape[2]
    nre = nt // 2
    nv = nre // SUBLANES
    nj = wb_ref.shape[1]
    kblocks = d // LANES
    per_kb = (nj // 2) // kblocks
    nq = wc_ref.shape[1]
    tiles_q = nre // nq
    step = pl.program_id(0)

    @pl.when(step == 0)
    def _():
        st_ref[...] = st0_ref[...]

    def prep(x_ref):
        return _rms(x_ref[...], g1_ref[...]) * (1.0 + sc_ref[...]) + sh_ref[...]

    hs = (prep(xf_ref), prep(xr_ref))
    bufs = (sf_s, sr_s)

    def slab(b, n):
        return (b * nt + n) * pitch

    for dr in range(2):
        hb = hs[dr].reshape(nb * t, d).astype(BF16)
        for j in range(nj):
            kb = (j % (nj // 2)) // per_kb
            r = _dot(hb[:, kb * LANES:(kb + 1) * LANES], wb_ref[dr, j])
            for b in range(nb):
                for half in range(MXU_N // LANES):
                    n = j * (MXU_N // LANES) + half
                    bufs[dr][pl.ds(slab(b, n), t), :] = r[b * t:(b + 1) * t, half * LANES:(half + 1) * LANES]

    consts = [[(are_ref[dr, v * SUBLANES:(v + 1) * SUBLANES, :], aim_ref[dr, v * SUBLANES:(v + 1) * SUBLANES, :])
               for v in range(nv)] for dr in range(2)]
    for b in range(nb):
        init = tuple(st_ref[b, dr, part * nre + v * SUBLANES:part * nre + (v + 1) * SUBLANES, :]
                     for dr in range(2) for v in range(nv) for part in range(2))

        def body(i, carry, b=b):
            out = []
            for dr in range(2):
                tt = i if dr == 0 else t - 1 - i
                for v in range(nv):
                    sre, sim = carry[(dr * nv + v) * 2], carry[(dr * nv + v) * 2 + 1]
                    are, aim = consts[dr][v]
                    row_re = slab(b, v * SUBLANES) + tt
                    row_im = slab(b, nre + v * SUBLANES) + tt
                    bre = bufs[dr][pl.ds(row_re, SUBLANES, stride=pitch), :]
                    bim = bufs[dr][pl.ds(row_im, SUBLANES, stride=pitch), :]
                    new_re = are * sre - aim * sim + bre
                    new_im = are * sim + aim * sre + bim
                    bufs[dr][pl.ds(row_re, SUBLANES, stride=pitch), :] = new_re
                    bufs[dr][pl.ds(row_im, SUBLANES, stride=pitch), :] = new_im
                    out += [new_re, new_im]
            return tuple(out)

        fin = lax.fori_loop(0, t, body, init)
        k = 0
        for dr in range(2):
            for v in range(nv):
                for part in range(2):
                    st_ref[b, dr, part * nre + v * SUBLANES:part * nre + (v + 1) * SUBLANES, :] = fin[k]
                    k += 1

    skip = hs[0] * dd_ref[...]
    outs = (yf_ref, yr_ref)
    for dr in range(2):
        for q in range(nq):
            rows = []
            for b in range(nb):
                pieces = [bufs[dr][pl.ds(slab(b, part * nre + q * tiles_q + i), t), :]
                          for part in range(2) for i in range(tiles_q)]
                rows.append(jnp.concatenate(pieces, axis=1))
            yq = _dot(jnp.concatenate(rows, axis=0).astype(BF16), wc_ref[dr, q])
            for b in range(nb):
                val = yq[b * t:(b + 1) * t, :]
                if dr == 0:
                    val = val + skip[b, :, q * LANES:(q + 1) * LANES]
                outs[dr][b, :, q * LANES:(q + 1) * LANES] = val


def _s5_scan(x3, mod3, g1, dd, wb, wc, are, aim, st0, t):
    nb, l, d = x3.shape
    nc = l // t
    pitch = t + SUBLANES // 2
    nt = st0.shape[2]
    full = lambda a: pl.BlockSpec(a.shape, lambda c: (0,) * a.ndim)
    buf = pltpu.VMEM((nb * nt * pitch, LANES), F32)
    return pl.pallas_call(
        functools.partial(_s5_kernel, pitch=pitch),
        grid=(nc,),
        in_specs=[
            pl.BlockSpec((nb, t, d), lambda c: (0, c, 0)),
            pl.BlockSpec((nb, t, d), lambda c: (0, nc - 1 - c, 0)),
            pl.BlockSpec((nb, 1, d), lambda c: (0, 0, 0)),
            pl.BlockSpec((nb, 1, d), lambda c: (0, 0, 1)),
            full(g1), full(dd), full(wb), full(wc), full(are), full(aim), full(st0),
        ],
        out_specs=[
            pl.BlockSpec((nb, t, d), lambda c: (0, c, 0)),
            pl.BlockSpec((nb, t, d), lambda c: (0, nc - 1 - c, 0)),
            full(st0),
        ],
        out_shape=[
            jax.ShapeDtypeStruct((nb, l, d), F32),
            jax.ShapeDtypeStruct((nb, l, d), F32),
            jax.ShapeDtypeStruct(st0.shape, F32),
        ],
        scratch_shapes=[buf, buf],
        compiler_params=_cparams(("arbitrary",)),
        name="s5_scan",
    )(x3, x3, mod3, mod3, g1, dd, wb, wc, are, aim, st0)


def _s5_glu_kernel(yf_ref, yr_ref, w_ref, o_ref):
    d = o_ref.shape[-1]
    z = _dot(_gelu(yf_ref[...] + yr_ref[...]).astype(BF16), w_ref[...])
    o_ref[...] = z[:, :d] * _sigmoid_tanh(z[:, d:])


def _s5_glu(yf, yr, w, lseg):
    n, d = yf.shape
    tm = _pick_tile(lseg)
    return pl.pallas_call(
        _s5_glu_kernel,
        grid=(n // tm,),
        in_specs=[pl.BlockSpec((tm, d), lambda i: (i, 0)), pl.BlockSpec((tm, d), lambda i: (i, 0)),
                  pl.BlockSpec(w.shape, lambda i: (0, 0))],
        out_specs=pl.BlockSpec((tm, d), lambda i: (i, 0)),
        out_shape=jax.ShapeDtypeStruct((n, d), F32),
        compiler_params=_cparams(("parallel",)),
        name="s5_glu",
    )(yf, yr, w)


def _s5_weights(a_re, a_im, log_dt, b_re, b_im, c_re, c_im):
    ng, ns = a_re.shape[1:]
    gi = b_re.shape[-1]
    d = ng * gi
    lr, li = a_re.astype(F32), a_im.astype(F32)
    dt = jnp.exp(log_dt.astype(F32))[..., None]
    mag = jnp.exp(lr * dt)
    ar, ai = mag * jnp.cos(li * dt), mag * jnp.sin(li * dt)
    den = lr * lr + li * li
    wr = ((ar - 1.0) * lr + ai * li) / den
    wi = (ai * lr - (ar - 1.0) * li) / den
    br, bi = b_re.astype(F32), b_im.astype(F32)
    bb_re = wr[..., None] * br - wi[..., None] * bi
    bb_im = wr[..., None] * bi + wi[..., None] * br

    gpt = MXU_N // ns
    gpk = LANES // gi
    ncol = ng // gpt
    sel = (jnp.arange(gpk)[None, :, None]
           == gpt * (jnp.arange(ncol)[:, None, None] % (gpk // gpt)) + jnp.arange(gpt)[None, None, :]).astype(F32)

    def in_tiles(part):
        blk = part.reshape(2, ncol, gpt, ns, gi).transpose(0, 1, 4, 2, 3)
        tiles = blk[:, :, None, :, :, :] * sel[None, :, :, None, :, None]
        return tiles.reshape(2, ncol, gpk * gi, gpt * ns)

    wb = jnp.concatenate([in_tiles(bb_re), in_tiles(bb_im)], axis=1).astype(BF16)

    nq = d // LANES
    gpq = ng // nq
    eye = jnp.eye(gpq, dtype=F32)

    def out_tiles(part):
        blk = jnp.swapaxes(part.astype(F32), -1, -2).reshape(2, nq, gpq, ns, gi)
        tiles = blk[:, :, :, :, None, :] * eye[None, None, :, None, :, None]
        return tiles.reshape(2, nq, gpq * ns, gpq * gi)

    wc = jnp.concatenate([out_tiles(c_re), out_tiles(-c_im)], axis=2).astype(BF16)
    are = ar.reshape(2, ng * ns // LANES, LANES)
    aim = ai.reshape(2, ng * ns // LANES, LANES)
    return wb, wc, are, aim


def _s5_layer(lat, cx, mod_l, mod_c, nb, g1, a_re, a_im, log_dt, b_re, b_im, c_re, c_im, dvec, w_glu, need_ctx):
    d = lat.shape[1]
    l = lat.shape[0] // nb
    lc = cx.shape[0] // nb
    wb, wc, are, aim = _s5_weights(a_re, a_im, log_dt, b_re, b_im, c_re, c_im)
    t = _pick_tile(l, lc, cap=64)
    st0 = jnp.zeros((nb, 2, 2 * are.shape[1], LANES), F32)
    dd = dvec[None, :].astype(F32)
    ycf, ycr, st = _s5_scan(cx.reshape(nb, lc, d), mod_c, g1, dd, wb, wc, are, aim, st0, t)
    yf, yr, _ = _s5_scan(lat.reshape(nb, l, d), mod_l, g1, dd, wb, wc, are, aim, st, t)
    wg = w_glu.astype(BF16)
    y = _s5_glu(yf.reshape(nb * l, d), yr.reshape(nb * l, d), wg, l)
    yc = _s5_glu(ycf.reshape(nb * lc, d), ycr.reshape(nb * lc, d), wg, lc) if need_ctx else None
    return y, yc


def _lru_in_kernel(x_ref, sh_ref, sc_ref, g1_ref, w_ref, xw_ref, gl_ref):
    wd = xw_ref.shape[-1]
    h = (_rms(x_ref[...], g1_ref[...]) * (1.0 + sc_ref[...]) + sh_ref[...]).astype(BF16)
    z = _dot(h, w_ref[...])
    gl_ref[...] = _gelu(z[:, :wd]).astype(gl_ref.dtype)
    xw_ref[...] = z[:, wd:]


def _lru_in(x, mod, lseg, g1, w_in):
    n, d = x.shape
    wd = w_in.shape[1] // 2
    tm = _pick_tile(lseg)
    return pl.pallas_call(
        _lru_in_kernel,
        grid=(n // tm,),
        in_specs=[pl.BlockSpec((tm, d), lambda i: (i, 0)), _mod_spec(0, tm, lseg, d), _mod_spec(1, tm, lseg, d),
                  pl.BlockSpec((1, d), lambda i: (0, 0)), pl.BlockSpec(w_in.shape, lambda i: (0, 0))],
        out_specs=[pl.BlockSpec((tm, wd), lambda i: (i, 0)), pl.BlockSpec((tm, wd), lambda i: (i, 0))],
        out_shape=[jax.ShapeDtypeStruct((n, wd), F32), jax.ShapeDtypeStruct((n, wd), BF16)],
        compiler_params=_cparams(("parallel",)),
        name="lru_in",
    )(x, mod, mod, g1, w_in)


def _lru_scan_kernel(xf_ref, xfp_ref, xfn_ref, xr_ref, xrp_ref, xrn_ref,
                     cw_ref, cb_ref, wg_ref, bg_ref, sp_ref, st0_ref,
                     hf_ref, hr_ref, st_ref, ext_s, a_s, b_s, *, pitch):
    nb, t, wd = xf_ref.shape
    nblk = wd // LANES
    halo = SUBLANES
    step = pl.program_id(0)
    last = pl.num_programs(0) - 1

    @pl.when(step == 0)
    def _():
        st_ref[...] = st0_ref[...]

    def coeffs(dr, x_ref, prev_ref, next_ref, at_start, at_end):
        ext_s[:, 0:halo, :] = jnp.where(at_start, 0.0, prev_ref[...])
        ext_s[:, halo:halo + t, :] = x_ref[...]
        ext_s[:, halo + t:2 * halo + t, :] = jnp.where(at_end, 0.0, next_ref[...])
        xr = cb_ref[...]
        for tap in range(cw_ref.shape[0]):
            xr = xr + cw_ref[tap:tap + 1, :] * ext_s[:, halo - 1 + tap:halo - 1 + tap + t, :]
        for n in range(nblk):
            xb = xr[:, :, n * LANES:(n + 1) * LANES].reshape(nb * t, LANES)
            g = _dot(xb.astype(BF16), wg_ref[dr, n]) + bg_ref[dr, n]
            r = _sigmoid_tanh(g[:, :LANES])
            ig = _sigmoid_tanh(g[:, LANES:])
            a = jnp.exp(-LRU_C * r * sp_ref[dr, n])
            bc = jnp.sqrt(1.0 - a * a) * (ig * xb)
            for b in range(nb):
                a_s[dr, n, pl.ds(b * pitch, t), :] = a[b * t:(b + 1) * t]
                b_s[dr, n, pl.ds(b * pitch, t), :] = bc[b * t:(b + 1) * t]

    coeffs(0, xf_ref, xfp_ref, xfn_ref, step == 0, step == last)
    coeffs(1, xr_ref, xrp_ref, xrn_ref, step == last, step == 0)

    init = tuple(st_ref[dr, n] for dr in range(2) for n in range(nblk))

    def body(i, carry):
        out = []
        for dr in range(2):
            tt = i if dr == 0 else t - 1 - i
            for n in range(nblk):
                a = a_s[dr, n, pl.ds(tt, nb, stride=pitch), :]
                bc = b_s[dr, n, pl.ds(tt, nb, stride=pitch), :]
                hnew = a * carry[dr * nblk + n] + bc
                b_s[dr, n, pl.ds(tt, nb, stride=pitch), :] = hnew
                out.append(hnew)
        return tuple(out)

    fin = lax.fori_loop(0, t, body, init)
    k = 0
    for dr in range(2):
        for n in range(nblk):
            st_ref[dr, n] = fin[k]
            k += 1
    outs = (hf_ref, hr_ref)
    for dr in range(2):
        for n in range(nblk):
            for b in range(nb):
                outs[dr][b, :, n * LANES:(n + 1) * LANES] = b_s[dr, n, pl.ds(b * pitch, t), :]


def _lru_scan(xw3, cw, cb, wg, bg, sp, st0, t):
    nb, l, wd = xw3.shape
    nc = l // t
    nblk = wd // LANES
    hb = t // SUBLANES
    nhb = l // SUBLANES
    pitch = t + SUBLANES
    full = lambda a: pl.BlockSpec(a.shape, lambda c: (0,) * a.ndim)
    chunk = lambda f: pl.BlockSpec((nb, t, wd), lambda c: (0, f(c), 0))
    halo_prev = lambda f: pl.BlockSpec((nb, SUBLANES, wd), lambda c: (0, jnp.maximum(f(c) * hb - 1, 0), 0))
    halo_next = lambda f: pl.BlockSpec((nb, SUBLANES, wd), lambda c: (0, jnp.minimum((f(c) + 1) * hb, nhb - 1), 0))
    fwd = lambda c: c
    rev = lambda c: nc - 1 - c
    coef = pltpu.VMEM((2, nblk, nb * pitch, LANES), F32)
    return pl.pallas_call(
        functools.partial(_lru_scan_kernel, pitch=pitch),
        grid=(nc,),
        in_specs=[chunk(fwd), halo_prev(fwd), halo_next(fwd), chunk(rev), halo_prev(rev), halo_next(rev),
                  full(cw), full(cb), full(wg), full(bg), full(sp), full(st0)],
        out_specs=[chunk(fwd), chunk(rev), full(st0)],
        out_shape=[jax.ShapeDtypeStruct((nb, l, wd), F32), jax.ShapeDtypeStruct((nb, l, wd), F32),
                   jax.ShapeDtypeStruct(st0.shape, F32)],
        scratch_shapes=[pltpu.VMEM((nb, t + 2 * SUBLANES, wd), F32), coef, coef],
        compiler_params=_cparams(("arbitrary",)),
        name="lru_scan",
    )(xw3, xw3, xw3, xw3, xw3, xw3, cw, cb, wg, bg, sp, st0)


def _lru_out_kernel(gl_ref, hf_ref, hr_ref, w_ref, o_ref):
    u = gl_ref[...].astype(F32) * (hf_ref[...] + hr_ref[...])
    o_ref[...] = _dot(u.astype(BF16), w_ref[...])


def _lru_out(gl, hf, hr, w, lseg):
    n, wd = gl.shape
    d = w.shape[1]
    tm = _pick_tile(lseg)
    row = lambda: pl.BlockSpec((tm, wd), lambda i: (i, 0))
    return pl.pallas_call(
        _lru_out_kernel,
        grid=(n // tm,),
        in_specs=[row(), row(), row(), pl.BlockSpec(w.shape, lambda i: (0, 0))],
        out_specs=pl.BlockSpec((tm, d), lambda i: (i, 0)),
        out_shape=jax.ShapeDtypeStruct((n, d), F32),
        compiler_params=_cparams(("parallel",)),
        name="lru_out",
    )(gl, hf, hr, w)


def _lru_layer(lat, cx, mod_l, mod_c, nb, g1, w_in, conv_w, conv_b, w_gate, b_gate, lam, w_out, need_ctx):
    l = lat.shape[0] // nb
    lc = cx.shape[0] // nb
    wd = w_in.shape[1] // 2
    nblk = wd // LANES
    w_in_b = w_in.astype(BF16)
    xw, gl = _lru_in(lat, mod_l, l, g1, w_in_b)
    xwc, glc = _lru_in(cx, mod_c, lc, g1, w_in_b)
    wg = jnp.concatenate([w_gate[:, 0], w_gate[:, 1]], axis=-1).astype(BF16)
    bgate = b_gate.astype(F32).reshape(2, 2, nblk, 1, LANES)
    bg = jnp.concatenate([bgate[:, 0], bgate[:, 1]], axis=-1)
    sp = jax.nn.softplus(-lam.astype(F32)).reshape(2, nblk, 1, LANES)
    cw = conv_w.astype(F32)
    cb = conv_b.astype(F32)[None, :]
    t = _pick_tile(l, lc, cap=64)
    st0 = jnp.zeros((2, nblk, nb, LANES), F32)
    hcf, hcr, st = _lru_scan(xwc.reshape(nb, lc, wd), cw, cb, wg, bg, sp, st0, t)
    hf, hr, _ = _lru_scan(xw.reshape(nb, l, wd), cw, cb, wg, bg, sp, st, t)
    wo = w_out.astype(BF16)
    y = _lru_out(gl, hf.reshape(nb * l, wd), hr.reshape(nb * l, wd), wo, l)
    yc = _lru_out(glc, hcf.reshape(nb * lc, wd), hcr.reshape(nb * lc, wd), wo, lc) if need_ctx else None
    return y, yc


def kernel(x, c, ctx, c_ctx, ada_w, ada_b, norm_g, mla_w_dq, mla_g_q, mla_w_uq, mla_w_dkv, mla_g_kv, mla_w_ukv, mla_g_qk, mla_w_o, s5_a_re, s5_a_im, s5_log_dt, s5_b_re, s5_b_im, s5_c_re, s5_c_im, s5_d, s5_w_glu, lru_w_in, lru_conv_w, lru_conv_b, lru_w_gate, lru_b_gate, lru_lambda, lru_w_out, mlp_w1, mlp_w2):
    nb, l, d = x.shape
    lc = ctx.shape[1]
    depth = ada_w.shape[0]
    assert nb < SUBLANES
    cc = jnp.concatenate([c, c_ctx[None, :], jnp.zeros((SUBLANES - nb - 1, d), F32)], axis=0)
    mod = _ada_mod(cc, ada_w, ada_b)
    lat = x.reshape(nb * l, d)
    cx = ctx.reshape(nb * lc, d)
    for i in range(depth):
        need_ctx = i < depth - 1
        mod_l = mod[i, :nb, None, :]
        mod_c = jnp.broadcast_to(mod[i, nb][None, None, :], (nb, 1, mod.shape[-1]))
        g1 = norm_g[i, 0][None, :]
        g2 = norm_g[i, 1][None, :]
        kind, j = i % N_MIXERS, i // N_MIXERS
        if kind == 0:
            y, yc = _mla_layer(lat, cx, mod_l, mod_c, nb, g1,
                               (mla_w_dq[j], mla_g_q[j], mla_w_uq[j], mla_w_dkv[j], mla_g_kv[j], mla_w_ukv[j],
                                mla_g_qk[j], mla_w_o[j]), need_ctx)
        elif kind == 1:
            y, yc = _s5_layer(lat, cx, mod_l, mod_c, nb, g1, s5_a_re[j], s5_a_im[j], s5_log_dt[j], s5_b_re[j],
                              s5_b_im[j], s5_c_re[j], s5_c_im[j], s5_d[j], s5_w_glu[j], need_ctx)
        else:
            y, yc = _lru_layer(lat, cx, mod_l, mod_c, nb, g1, lru_w_in[j], lru_conv_w[j], lru_conv_b[j],
                               lru_w_gate[j], lru_b_gate[j], lru_lambda[j], lru_w_out[j], need_ctx)
        lat = _mlp(lat, y, mod_l, l, g2, mlp_w1, mlp_w2, i)
        if need_ctx:
            cx = _mlp(cx, yc, mod_c, lc, g2, mlp_w1, mlp_w2, i)
    return lat.reshape(nb, l, d)
```

```python
import functools
import math

import jax
import jax.numpy as jnp
from jax import lax
from jax.experimental import pallas as pl
from jax.experimental.pallas import tpu as pltpu

F32 = jnp.float32
BF16 = jnp.bfloat16

EPS = 1e-6
ROPE_THETA = 10000.0
GRID_W = 64
N_MIXERS = 3

MLA_HEADS = 8
MLA_NOPE = 128
MLA_ROPE = 64
MLA_V = 128
MLA_QK = MLA_NOPE + MLA_ROPE
MLA_SCALE = math.log2(math.e) / math.sqrt(MLA_QK)

S5_GROUP = 16
S5_STATE = 64
LRU_BLOCKS = 10
LRU_BW = 128
LRU_C = 8.0

LANES = 128
SUBLANES = 8
MXU_N = 256
VMEM_LIMIT = 56 * 1024 * 1024


def _cparams(sem):
    return pltpu.CompilerParams(dimension_semantics=sem, vmem_limit_bytes=VMEM_LIMIT)


def _dot(a, b):
    return jnp.dot(a, b, preferred_element_type=F32)


def _dot_nt(a, b):
    return lax.dot_general(a, b, (((1,), (1,)), ((), ())), preferred_element_type=F32)


def _rms(x, g):
    inv = lax.rsqrt(jnp.mean(x * x, axis=-1, keepdims=True) + EPS)
    return x * inv * g


def _gelu(x):
    c = math.sqrt(2.0 / math.pi)
    return x * (0.5 * (1.0 + jnp.tanh(c * (x + 0.044715 * (x * x * x)))))


def _sigmoid(x):
    return 1.0 / (1.0 + jnp.exp(-x))


def _sigmoid_tanh(x):
    return 0.5 * jnp.tanh(0.5 * x) + 0.5


def _pick_tile(*lens, cap=512):
    t = cap
    while any(n % t for n in lens):
        t //= 2
    assert t >= SUBLANES
    return t


def _ada_kernel(c_ref, w_ref, b_ref, o_ref):
    c = c_ref[...]
    s = c * _sigmoid(c)
    w = w_ref[...]
    s_hi = s.astype(BF16)
    s_lo = (s - s_hi.astype(F32)).astype(BF16)
    w_hi = w.astype(BF16)
    w_lo = (w - w_hi.astype(F32)).astype(BF16)
    o_ref[...] = _dot(s_hi, w_hi) + _dot(s_lo, w_hi) + _dot(s_hi, w_lo) + b_ref[...]


def _ada_mod(cc, ada_w, ada_b):
    depth, d, d6 = ada_w.shape
    nk = d6 // d
    return pl.pallas_call(
        _ada_kernel,
        grid=(depth, nk),
        in_specs=[
            pl.BlockSpec((SUBLANES, d), lambda l, k: (0, 0)),
            pl.BlockSpec((None, d, d), lambda l, k: (l, 0, k)),
            pl.BlockSpec((None, 1, d), lambda l, k: (l, 0, k)),
        ],
        out_specs=pl.BlockSpec((None, SUBLANES, d), lambda l, k: (l, 0, k)),
        out_shape=jax.ShapeDtypeStruct((depth, SUBLANES, d6), F32),
        compiler_params=_cparams(("parallel", "parallel")),
        name="ada_mod",
    )(cc, ada_w, ada_b.reshape(depth, 1, d6))


def _mod_spec(k, tm, lseg, d, tile_of=lambda i: i):
    return pl.BlockSpec((None, 1, d), lambda i, *_: ((tile_of(i) * tm) // lseg, 0, k))


def _mlp_kernel(x_ref, y_ref, ga_ref, shm_ref, scm_ref, gm_ref, g2_ref, w1_ref, w2_ref,
                o_ref, w1_s, w2_s):
    step = pl.program_id(0)
    nj = w1_s.shape[0]

    @pl.when(step < nj)
    def _():
        w1_s[step] = w1_ref[...].astype(BF16)
        w2_s[step] = w2_ref[...].astype(BF16)

    @pl.when(step >= nj)
    def _():
        x1 = x_ref[...] + ga_ref[...] * y_ref[...].astype(F32)
        h = (_rms(x1, g2_ref[...]) * (1.0 + scm_ref[...]) + shm_ref[...]).astype(BF16)
        acc = jnp.zeros_like(x1)
        for c in range(nj):
            t = jnp.maximum(_dot(h, w1_s[c]), 0.0)
            acc = acc + _dot((t * t).astype(BF16), w2_s[c])
        o_ref[...] = x1 + gm_ref[...] * acc


def _mlp(x, y, mod, lseg, g2, w1_all, w2_all, layer):
    n, d = x.shape
    dff = w1_all.shape[2]
    tm = _pick_tile(lseg)
    tf = min(dff, 512)
    nj = dff // tf
    tile = lambda s: jnp.maximum(s - nj, 0)
    wblk = lambda s: jnp.minimum(s, nj - 1)
    row = lambda: pl.BlockSpec((tm, d), lambda s: (tile(s), 0))
    return pl.pallas_call(
        _mlp_kernel,
        grid=(nj + n // tm,),
        in_specs=[
            row(), row(),
            _mod_spec(2, tm, lseg, d, tile),
            _mod_spec(3, tm, lseg, d, tile),
            _mod_spec(4, tm, lseg, d, tile),
            _mod_spec(5, tm, lseg, d, tile),
            pl.BlockSpec((1, d), lambda s: (0, 0)),
            pl.BlockSpec((None, d, tf), lambda s: (layer, 0, wblk(s))),
            pl.BlockSpec((None, tf, d), lambda s: (layer, wblk(s), 0)),
        ],
        out_specs=row(),
        out_shape=jax.ShapeDtypeStruct((n, d), F32),
        scratch_shapes=[pltpu.VMEM((nj, d, tf), BF16), pltpu.VMEM((nj, tf, d), BF16)],
        compiler_params=_cparams(("arbitrary",)),
        name="mlp",
    )(x, y, mod, mod, mod, mod, g2, w1_all, w2_all)


def _proj_kernel(a_ref, w_ref, o_ref):
    o_ref[...] = _dot(a_ref[...].astype(BF16), w_ref[...])


def _proj(a, w, lseg):
    n, k = a.shape
    dout = w.shape[1]
    tm = _pick_tile(lseg)
    return pl.pallas_call(
        _proj_kernel,
        grid=(n // tm,),
        in_specs=[pl.BlockSpec((tm, k), lambda i: (i, 0)), pl.BlockSpec((k, dout), lambda i: (0, 0))],
        out_specs=pl.BlockSpec((tm, dout), lambda i: (i, 0)),
        out_shape=jax.ShapeDtypeStruct((n, dout), F32),
        compiler_params=_cparams(("parallel",)),
        name="proj",
    )(a, w)


def _rope(x, cos, sin_signed):
    width = x.shape[-1]
    quarter = MLA_ROPE // 4
    lane = lax.broadcasted_iota(jnp.int32, x.shape, 1)
    even_quarter = ((lane // quarter) % 2) == 0
    rot = jnp.where(even_quarter, pltpu.roll(x, width - quarter, 1), pltpu.roll(x, quarter, 1))
    return x * cos + rot * sin_signed


def _mla_proj_kernel(x_ref, sh_ref, sc_ref, g1_ref, cos_ref, sin_ref,
                     wdq_ref, gq_ref, wuq_ref, wdkv_ref, gkv_ref, wukv_ref,
                     gqn_ref, gqr_ref, gkn_ref, gkr_ref, seg_ref,
                     q_ref, k_ref, v_ref):
    nh = q_ref.shape[0]
    dn = nh * MLA_NOPE
    dr = nh * MLA_ROPE
    kvl = gkv_ref.shape[-1]
    h = (_rms(x_ref[...], g1_ref[...]) * (1.0 + sc_ref[...]) + sh_ref[...]).astype(BF16)
    ql = _rms(_dot(h, wdq_ref[...]), gq_ref[...]).astype(BF16)
    q = _dot(ql, wuq_ref[...])
    kv = _dot(h, wdkv_ref[...])
    ckv = _rms(kv[:, :kvl], gkv_ref[...]).astype(BF16)
    kvu = _dot(ckv, wukv_ref[...])

    cos = cos_ref[...]
    sin = sin_ref[...]
    reps = dr // LANES
    cos_q = jnp.concatenate([cos] * reps, axis=1)
    sin_q = jnp.concatenate([sin] * reps, axis=1)

    qr = q[:, dn:]
    sq = qr * qr
    sq_hi = sq.astype(BF16)
    sq_lo = (sq - sq_hi.astype(F32)).astype(BF16)
    ms = _dot(sq_hi, seg_ref[...]) + _dot(sq_lo, seg_ref[...])
    qr = qr * lax.rsqrt(ms + EPS) * gqr_ref[...]
    qr = _rope(qr, cos_q, sin_q) * MLA_SCALE

    kr = kv[:, kvl:]
    kr = _rms(kr, gkr_ref[...])
    kr = _rope(kr, cos, sin)[:, :MLA_ROPE]

    for hh in range(nh):
        qn = _rms(q[:, hh * MLA_NOPE:(hh + 1) * MLA_NOPE], gqn_ref[...]) * MLA_SCALE
        qrh = qr[:, hh * MLA_ROPE:(hh + 1) * MLA_ROPE]
        q_ref[hh] = jnp.concatenate([qn, qrh], axis=1).astype(q_ref.dtype)
        kn = _rms(kvu[:, hh * MLA_NOPE:(hh + 1) * MLA_NOPE], gkn_ref[...])
        k_ref[hh] = jnp.concatenate([kn, kr], axis=1).astype(k_ref.dtype)
        v_ref[hh, :, :MLA_V] = kvu[:, dn + hh * MLA_V:dn + (hh + 1) * MLA_V].astype(v_ref.dtype)
        v_ref[hh, :, MLA_V:] = jnp.ones((kvu.shape[0], MLA_V), v_ref.dtype)


def _mla_proj(x, mod, lseg, g1, cos_t, sin_t, w):
    n, d = x.shape
    tm = _pick_tile(lseg, cap=512)
    nrope = cos_t.shape[0] // tm
    full = lambda a: pl.BlockSpec(a.shape, lambda i: (0,) * a.ndim)
    weights = [w["wdq"], w["gq"], w["wuq"], w["wdkv"], w["gkv"], w["wukv"],
               w["gqn"], w["gqr"], w["gkn"], w["gkr"], w["seg"]]
    return pl.pallas_call(
        _mla_proj_kernel,
        grid=(n // tm,),
        in_specs=[
            pl.BlockSpec((tm, d), lambda i: (i, 0)),
            _mod_spec(0, tm, lseg, d),
            _mod_spec(1, tm, lseg, d),
            pl.BlockSpec((1, d), lambda i: (0, 0)),
            pl.BlockSpec((tm, LANES), lambda i: (i % nrope, 0)),
            pl.BlockSpec((tm, LANES), lambda i: (i % nrope, 0)),
        ] + [full(a) for a in weights],
        out_specs=[
            pl.BlockSpec((MLA_HEADS, tm, MLA_QK), lambda i: (0, i, 0)),
            pl.BlockSpec((MLA_HEADS, tm, MLA_QK), lambda i: (0, i, 0)),
            pl.BlockSpec((MLA_HEADS, tm, 2 * MLA_V), lambda i: (0, i, 0)),
        ],
        out_shape=[
            jax.ShapeDtypeStruct((MLA_HEADS, n, MLA_QK), BF16),
            jax.ShapeDtypeStruct((MLA_HEADS, n, MLA_QK), BF16),
            jax.ShapeDtypeStruct((MLA_HEADS, n, 2 * MLA_V), BF16),
        ],
        compiler_params=_cparams(("parallel",)),
        name="mla_proj",
    )(x, mod, mod, g1, cos_t, sin_t, *weights)


def _attn_lat_kernel(q_ref, kc_ref, kl_ref, vc_ref, vl_ref, o_ref, sa_s, sb_s, ma_s, mb_s, *, tq, kb):
    lq = q_ref.shape[0]
    lc = kc_ref.shape[0]
    n_tiles = lq // tq
    blocks = [(kc_ref, vc_ref, 0, lc, 0)] + [(kl_ref, vl_ref, j * kb, kb, lc + j * kb)
                                             for j in range(kl_ref.shape[0] // kb)]
    slots = ((sa_s, ma_s), (sb_s, mb_s))

    def iteration(i, slot, first, second):
        s_w, m_w = slots[slot]
        s_r, m_r = slots[1 - slot]
        if first:
            q = q_ref[pl.ds(pl.multiple_of(i * tq, tq), tq), :]
            mp = jnp.full((tq, LANES), -jnp.inf, F32)
        if second:
            m = m_r[...]
            acc = jnp.zeros((tq, 2 * MLA_V), F32)
        for k_ref, v_ref, start, size, off in blocks:
            if first:
                s = _dot_nt(q, k_ref[start:start + size, :])
                s_w[:, off:off + size] = s
                for c in range(size // LANES):
                    mp = jnp.maximum(mp, s[:, c * LANES:(c + 1) * LANES])
            if second:
                p = jnp.concatenate(
                    [jnp.exp2(s_r[:, off + c * LANES:off + (c + 1) * LANES] - m)
                     for c in range(size // LANES)], axis=1).astype(BF16)
                acc = acc + _dot(p, v_ref[start:start + size, :])
        if first:
            m_w[...] = jnp.broadcast_to(jnp.max(mp, axis=-1, keepdims=True), (tq, LANES))
        if second:
            rows = pl.ds(pl.multiple_of((i - 1) * tq, tq), tq)
            o_ref[rows, :] = (acc[:, :MLA_V] / acc[:, MLA_V:]).astype(o_ref.dtype)

    iteration(0, 0, True, False)

    def pair(j, carry):
        iteration(2 * j + 1, 1, True, True)
        iteration(2 * j + 2, 0, True, True)
        return carry

    lax.fori_loop(0, (n_tiles - 1) // 2, pair, 0)
    if (n_tiles - 1) % 2:
        iteration(n_tiles - 1, 1, True, True)
    iteration(n_tiles, n_tiles % 2, False, True)


def _attn_ctx_kernel(q_ref, kc_ref, vc_ref, o_ref):
    sc = _dot_nt(q_ref[...], kc_ref[...])
    m = jnp.max(sc, axis=-1, keepdims=True)
    o = _dot(jnp.exp2(sc - m).astype(BF16), vc_ref[...])
    o_ref[...] = (o[:, :MLA_V] / o[:, MLA_V:]).astype(o_ref.dtype)


def _attn_lat(q, kl, vl, kc, vc, nb):
    nh, n, _ = q.shape
    l = n // nb
    lc = kc.shape[1] // nb
    tq = _pick_tile(l, cap=256)
    kb = _pick_tile(l, cap=MXU_N)
    assert lc % LANES == 0
    return pl.pallas_call(
        functools.partial(_attn_lat_kernel, tq=tq, kb=kb),
        grid=(nb, nh),
        in_specs=[
            pl.BlockSpec((None, l, MLA_QK), lambda b, h: (h, b, 0)),
            pl.BlockSpec((None, lc, MLA_QK), lambda b, h: (h, b, 0)),
            pl.BlockSpec((None, l, MLA_QK), lambda b, h: (h, b, 0)),
            pl.BlockSpec((None, lc, 2 * MLA_V), lambda b, h: (h, b, 0)),
            pl.BlockSpec((None, l, 2 * MLA_V), lambda b, h: (h, b, 0)),
        ],
        out_specs=pl.BlockSpec((l, MLA_V), lambda b, h: (b, h)),
        out_shape=jax.ShapeDtypeStruct((n, nh * MLA_V), BF16),
        scratch_shapes=[pltpu.VMEM((tq, lc + l), F32), pltpu.VMEM((tq, lc + l), F32),
                        pltpu.VMEM((tq, LANES), F32), pltpu.VMEM((tq, LANES), F32)],
        compiler_params=_cparams(("parallel", "parallel")),
        name="attn_lat",
    )(q, kc, kl, vc, vl)


def _attn_ctx(q, kc, vc, nb):
    nh, n, _ = q.shape
    lc = n // nb
    return pl.pallas_call(
        _attn_ctx_kernel,
        grid=(nb, nh),
        in_specs=[
            pl.BlockSpec((None, lc, MLA_QK), lambda b, h: (h, b, 0)),
            pl.BlockSpec((None, lc, MLA_QK), lambda b, h: (h, b, 0)),
            pl.BlockSpec((None, lc, 2 * MLA_V), lambda b, h: (h, b, 0)),
        ],
        out_specs=pl.BlockSpec((lc, MLA_V), lambda b, h: (b, h)),
        out_shape=jax.ShapeDtypeStruct((n, nh * MLA_V), BF16),
        compiler_params=_cparams(("parallel", "parallel")),
        name="attn_ctx",
    )(q, kc, vc)


def _rope_tables(n_tok):
    rows = n_tok // GRID_W
    row = jnp.repeat(jnp.arange(rows, dtype=F32), GRID_W)
    col = jnp.tile(jnp.arange(GRID_W, dtype=F32), rows)
    n_freq = MLA_ROPE // 4
    freqs = ROPE_THETA ** (-jnp.arange(n_freq, dtype=F32) / n_freq)
    ang_r = row[:, None] * freqs[None, :]
    ang_c = col[:, None] * freqs[None, :]
    ang = jnp.concatenate([ang_r, ang_r, ang_c, ang_c], axis=-1)
    sign = jnp.tile(jnp.repeat(jnp.array([-1.0, 1.0], F32), n_freq), 2)
    cos = jnp.cos(ang)
    sin = jnp.sin(ang) * sign
    return jnp.concatenate([cos, cos], axis=1), jnp.concatenate([sin, sin], axis=1)


def _mla_weights(w_dq, g_q, w_uq, w_dkv, g_kv, w_ukv, g_qk, w_o):
    nh = MLA_HEADS
    qlora = w_dq.shape[1]
    kvl = g_kv.shape[0]
    uq = w_uq.reshape(qlora, nh, MLA_QK)
    wuq = jnp.concatenate([uq[:, :, :MLA_NOPE].reshape(qlora, nh * MLA_NOPE),
                           uq[:, :, MLA_NOPE:].reshape(qlora, nh * MLA_ROPE)], axis=1)
    ukv = w_ukv.reshape(kvl, nh, MLA_NOPE + MLA_V)
    wukv = jnp.concatenate([ukv[:, :, :MLA_NOPE].reshape(kvl, nh * MLA_NOPE),
                            ukv[:, :, MLA_NOPE:].reshape(kvl, nh * MLA_V)], axis=1)
    wdkv = jnp.concatenate([w_dkv, w_dkv[:, kvl:]], axis=1)
    dr = nh * MLA_ROPE
    seg_id = jnp.arange(dr) // MLA_ROPE
    seg = (seg_id[:, None] == seg_id[None, :]).astype(F32) / MLA_ROPE
    return {
        "wdq": w_dq.astype(BF16), "gq": g_q[None, :], "wuq": wuq.astype(BF16),
        "wdkv": wdkv.astype(BF16), "gkv": g_kv[None, :], "wukv": wukv.astype(BF16),
        "gqn": g_qk[0:1, :MLA_NOPE], "gqr": jnp.tile(g_qk[0:1, MLA_NOPE:], (1, nh)),
        "gkn": g_qk[1:2, :MLA_NOPE], "gkr": jnp.tile(g_qk[1:2, MLA_NOPE:], (1, 2)),
        "seg": seg.astype(BF16), "wo": w_o.astype(BF16),
    }


def _mla_layer(lat, cx, mod_l, mod_c, nb, g1, wts, need_ctx):
    l = lat.shape[0] // nb
    lc = cx.shape[0] // nb
    w = _mla_weights(*wts)
    cos_t, sin_t = _rope_tables(l)
    tmc = _pick_tile(lc, cap=512)
    ql, kl, vl = _mla_proj(lat, mod_l, l, g1, cos_t, sin_t, w)
    qc, kc, vc = _mla_proj(cx, mod_c, lc, g1, jnp.ones((tmc, LANES), F32), jnp.zeros((tmc, LANES), F32), w)
    o = _attn_lat(ql, kl, vl, kc, vc, nb)
    y = _proj(o, w["wo"], l)
    yc = None
    if need_ctx:
        yc = _proj(_attn_ctx(qc, kc, vc, nb), w["wo"], lc)
    return y, yc


def _s5_kernel(xf_ref, xr_ref, sh_ref, sc_ref, g1_ref, dd_ref, wb_ref, wc_ref, are_ref, aim_ref, st0_ref,
               yf_ref, yr_ref, st_ref, sf_s, sr_s, *, pitch):
    nb, t, d = xf_ref.shape
    nt = st_ref.shape[2]
    nre = nt // 2
    nv = nre // SUBLANES
    nj = wb_ref.shape[1]
    kblocks = d // LANES
    per_kb = (nj // 2) // kblocks
    nq = wc_ref.shape[1]
    tiles_q = nre // nq
    step = pl.program_id(0)

    @pl.when(step == 0)
    def _():
        st_ref[...] = st0_ref[...]

    def prep(x_ref):
        return _rms(x_ref[...], g1_ref[...]) * (1.0 + sc_ref[...]) + sh_ref[...]

    hs = (prep(xf_ref), prep(xr_ref))
    bufs = (sf_s, sr_s)

    def slab(b, n):
        return (b * nt + n) * pitch

    for dr in range(2):
        hb = hs[dr].reshape(nb * t, d).astype(BF16)
        for j in range(nj):
            kb = (j % (nj // 2)) // per_kb
            r = _dot(hb[:, kb * LANES:(kb + 1) * LANES], wb_ref[dr, j])
            for b in range(nb):
                for half in range(MXU_N // LANES):
                    n = j * (MXU_N // LANES) + half
                    bufs[dr][pl.ds(slab(b, n), t), :] = r[b * t:(b + 1) * t, half * LANES:(half + 1) * LANES]

    consts = [[(are_ref[dr, v * SUBLANES:(v + 1) * SUBLANES, :], aim_ref[dr, v * SUBLANES:(v + 1) * SUBLANES, :])
               for v in range(nv)] for dr in range(2)]
    for b in range(nb):
        init = tuple(st_ref[b, dr, part * nre + v * SUBLANES:part * nre + (v + 1) * SUBLANES, :]
                     for dr in range(2) for v in range(nv) for part in range(2))

        def body(i, carry, b=b):
            out = []
            for dr in range(2):
                tt = i if dr == 0 else t - 1 - i
                for v in range(nv):
                    sre, sim = carry[(dr * nv + v) * 2], carry[(dr * nv + v) * 2 + 1]
                    are, aim = consts[dr][v]
                    row_re = slab(b, v * SUBLANES) + tt
                    row_im = slab(b, nre + v * SUBLANES) + tt
                    bre = bufs[dr][pl.ds(row_re, SUBLANES, stride=pitch), :]
                    bim = bufs[dr][pl.ds(row_im, SUBLANES, stride=pitch), :]
                    new_re = are * sre - aim * sim + bre
                    new_im = are * sim + aim * sre + bim
                    bufs[dr][pl.ds(row_re, SUBLANES, stride=pitch), :] = new_re
                    bufs[dr][pl.ds(row_im, SUBLANES, stride=pitch), :] = new_im
                    out += [new_re, new_im]
            return tuple(out)

        fin = lax.fori_loop(0, t, body, init)
        k = 0
        for dr in range(2):
            for v in range(nv):
                for part in range(2):
                    st_ref[b, dr, part * nre + v * SUBLANES:part * nre + (v + 1) * SUBLANES, :] = fin[k]
                    k += 1

    skip = hs[0] * dd_ref[...]
    outs = (yf_ref, yr_ref)
    for dr in range(2):
        for q in range(nq):
            rows = []
            for b in range(nb):
                pieces = [bufs[dr][pl.ds(slab(b, part * nre + q * tiles_q + i), t), :]
                          for part in range(2) for i in range(tiles_q)]
                rows.append(jnp.concatenate(pieces, axis=1))
            yq = _dot(jnp.concatenate(rows, axis=0).astype(BF16), wc_ref[dr, q])
            for b in range(nb):
                val = yq[b * t:(b + 1) * t, :]
                if dr == 0:
                    val = val + skip[b, :, q * LANES:(q + 1) * LANES]
                outs[dr][b, :, q * LANES:(q + 1) * LANES] = val


def _s5_scan(x3, mod3, g1, dd, wb, wc, are, aim, st0, t):
    nb, l, d = x3.shape
    nc = l // t
    pitch = t + SUBLANES // 2
    nt = st0.shape[2]
    full = lambda a: pl.BlockSpec(a.shape, lambda c: (0,) * a.ndim)
    buf = pltpu.VMEM((nb * nt * pitch, LANES), F32)
    return pl.pallas_call(
        functools.partial(_s5_kernel, pitch=pitch),
        grid=(nc,),
        in_specs=[
            pl.BlockSpec((nb, t, d), lambda c: (0, c, 0)),
            pl.BlockSpec((nb, t, d), lambda c: (0, nc - 1 - c, 0)),
            pl.BlockSpec((nb, 1, d), lambda c: (0, 0, 0)),
            pl.BlockSpec((nb, 1, d), lambda c: (0, 0, 1)),
            full(g1), full(dd), full(wb), full(wc), full(are), full(aim), full(st0),
        ],
        out_specs=[
            pl.BlockSpec((nb, t, d), lambda c: (0, c, 0)),
            pl.BlockSpec((nb, t, d), lambda c: (0, nc - 1 - c, 0)),
            full(st0),
        ],
        out_shape=[
            jax.ShapeDtypeStruct((nb, l, d), F32),
            jax.ShapeDtypeStruct((nb, l, d), F32),
            jax.ShapeDtypeStruct(st0.shape, F32),
        ],
        scratch_shapes=[buf, buf],
        compiler_params=_cparams(("arbitrary",)),
        name="s5_scan",
    )(x3, x3, mod3, mod3, g1, dd, wb, wc, are, aim, st0)


def _s5_glu_kernel(yf_ref, yr_ref, w_ref, o_ref):
    d = o_ref.shape[-1]
    z = _dot(_gelu(yf_ref[...] + yr_ref[...]).astype(BF16), w_ref[...])
    o_ref[...] = z[:, :d] * _sigmoid_tanh(z[:, d:])


def _s5_glu(yf, yr, w, lseg):
    n, d = yf.shape
    tm = _pick_tile(lseg)
    return pl.pallas_call(
        _s5_glu_kernel,
        grid=(n // tm,),
        in_specs=[pl.BlockSpec((tm, d), lambda i: (i, 0)), pl.BlockSpec((tm, d), lambda i: (i, 0)),
                  pl.BlockSpec(w.shape, lambda i: (0, 0))],
        out_specs=pl.BlockSpec((tm, d), lambda i: (i, 0)),
        out_shape=jax.ShapeDtypeStruct((n, d), F32),
        compiler_params=_cparams(("parallel",)),
        name="s5_glu",
    )(yf, yr, w)


def _s5_weights(a_re, a_im, log_dt, b_re, b_im, c_re, c_im):
    ng, ns = a_re.shape[1:]
    gi = b_re.shape[-1]
    d = ng * gi
    lr, li = a_re.astype(F32), a_im.astype(F32)
    dt = jnp.exp(log_dt.astype(F32))[..., None]
    mag = jnp.exp(lr * dt)
    ar, ai = mag * jnp.cos(li * dt), mag * jnp.sin(li * dt)
    den = lr * lr + li * li
    wr = ((ar - 1.0) * lr + ai * li) / den
    wi = (ai * lr - (ar - 1.0) * li) / den
    br, bi = b_re.astype(F32), b_im.astype(F32)
    bb_re = wr[..., None] * br - wi[..., None] * bi
    bb_im = wr[..., None] * bi + wi[..., None] * br

    gpt = MXU_N // ns
    gpk = LANES // gi
    ncol = ng // gpt
    sel = (jnp.arange(gpk)[None, :, None]
           == gpt * (jnp.arange(ncol)[:, None, None] % (gpk // gpt)) + jnp.arange(gpt)[None, None, :]).astype(F32)

    def in_tiles(part):
        blk = part.reshape(2, ncol, gpt, ns, gi).transpose(0, 1, 4, 2, 3)
        tiles = blk[:, :, None, :, :, :] * sel[None, :, :, None, :, None]
        return tiles.reshape(2, ncol, gpk * gi, gpt * ns)

    wb = jnp.concatenate([in_tiles(bb_re), in_tiles(bb_im)], axis=1).astype(BF16)

    nq = d // LANES
    gpq = ng // nq
    eye = jnp.eye(gpq, dtype=F32)

    def out_tiles(part):
        blk = jnp.swapaxes(part.astype(F32), -1, -2).reshape(2, nq, gpq, ns, gi)
        tiles = blk[:, :, :, :, None, :] * eye[None, None, :, None, :, None]
        return tiles.reshape(2, nq, gpq * ns, gpq * gi)

    wc = jnp.concatenate([out_tiles(c_re), out_tiles(-c_im)], axis=2).astype(BF16)
    are = ar.reshape(2, ng * ns // LANES, LANES)
    aim = ai.reshape(2, ng * ns // LANES, LANES)
    return wb, wc, are, aim


def _s5_layer(lat, cx, mod_l, mod_c, nb, g1, a_re, a_im, log_dt, b_re, b_im, c_re, c_im, dvec, w_glu, need_ctx):
    d = lat.shape[1]
    l = lat.shape[0] // nb
    lc = cx.shape[0] // nb
    wb, wc, are, aim = _s5_weights(a_re, a_im, log_dt, b_re, b_im, c_re, c_im)
    t = _pick_tile(l, lc, cap=64)
    st0 = jnp.zeros((nb, 2, 2 * are.shape[1], LANES), F32)
    dd = dvec[None, :].astype(F32)
    ycf, ycr, st = _s5_scan(cx.reshape(nb, lc, d), mod_c, g1, dd, wb, wc, are, aim, st0, t)
    yf, yr, _ = _s5_scan(lat.reshape(nb, l, d), mod_l, g1, dd, wb, wc, are, aim, st, t)
    wg = w_glu.astype(BF16)
    y = _s5_glu(yf.reshape(nb * l, d), yr.reshape(nb * l, d), wg, l)
    yc = _s5_glu(ycf.reshape(nb * lc, d), ycr.reshape(nb * lc, d), wg, lc) if need_ctx else None
    return y, yc


def _lru_in_kernel(x_ref, sh_ref, sc_ref, g1_ref, w_ref, xw_ref, gl_ref):
    wd = xw_ref.shape[-1]
    h = (_rms(x_ref[...], g1_ref[...]) * (1.0 + sc_ref[...]) + sh_ref[...]).astype(BF16)
    z = _dot(h, w_ref[...])
    gl_ref[...] = _gelu(z[:, :wd]).astype(gl_ref.dtype)
    xw_ref[...] = z[:, wd:]


def _lru_in(x, mod, lseg, g1, w_in):
    n, d = x.shape
    wd = w_in.shape[1] // 2
    tm = _pick_tile(lseg)
    return pl.pallas_call(
        _lru_in_kernel,
        grid=(n // tm,),
        in_specs=[pl.BlockSpec((tm, d), lambda i: (i, 0)), _mod_spec(0, tm, lseg, d), _mod_spec(1, tm, lseg, d),
                  pl.BlockSpec((1, d), lambda i: (0, 0)), pl.BlockSpec(w_in.shape, lambda i: (0, 0))],
        out_specs=[pl.BlockSpec((tm, wd), lambda i: (i, 0)), pl.BlockSpec((tm, wd), lambda i: (i, 0))],
        out_shape=[jax.ShapeDtypeStruct((n, wd), F32), jax.ShapeDtypeStruct((n, wd), BF16)],
        compiler_params=_cparams(("parallel",)),
        name="lru_in",
    )(x, mod, mod, g1, w_in)


def _lru_scan_kernel(xf_ref, xfp_ref, xfn_ref, xr_ref, xrp_ref, xrn_ref,
                     cw_ref, cb_ref, wg_ref, bg_ref, sp_ref, st0_ref,
                     hf_ref, hr_ref, st_ref, ext_s, a_s, b_s, *, pitch):
    nb, t, wd = xf_ref.shape
    nblk = wd // LANES
    halo = SUBLANES
    step = pl.program_id(0)
    last = pl.num_programs(0) - 1

    @pl.when(step == 0)
    def _():
        st_ref[...] = st0_ref[...]

    def coeffs(dr, x_ref, prev_ref, next_ref, at_start, at_end):
        ext_s[:, 0:halo, :] = jnp.where(at_start, 0.0, prev_ref[...])
        ext_s[:, halo:halo + t, :] = x_ref[...]
        ext_s[:, halo + t:2 * halo + t, :] = jnp.where(at_end, 0.0, next_ref[...])
        xr = cb_ref[...]
        for tap in range(cw_ref.shape[0]):
            xr = xr + cw_ref[tap:tap + 1, :] * ext_s[:, halo - 1 + tap:halo - 1 + tap + t, :]
        for n in range(nblk):
            xb = xr[:, :, n * LANES:(n + 1) * LANES].reshape(nb * t, LANES)
            g = _dot(xb.astype(BF16), wg_ref[dr, n]) + bg_ref[dr, n]
            c1 = sp_ref[dr, n]
            a = jnp.exp2(c1 * jnp.tanh(g[:, :LANES]) + c1)
            bc = jnp.sqrt(1.0 - a * a) * ((0.5 * xb) * (jnp.tanh(g[:, LANES:]) + 1.0))
            for b in range(nb):
                a_s[dr, n, pl.ds(b * pitch, t), :] = a[b * t:(b + 1) * t]
                b_s[dr, n, pl.ds(b * pitch, t), :] = bc[b * t:(b + 1) * t]

    coeffs(0, xf_ref, xfp_ref, xfn_ref, step == 0, step == last)
    coeffs(1, xr_ref, xrp_ref, xrn_ref, step == last, step == 0)

    init = tuple(st_ref[dr, n] for dr in range(2) for n in range(nblk))

    def body(i, carry):
        out = []
        for dr in range(2):
            tt = i if dr == 0 else t - 1 - i
            for n in range(nblk):
                a = a_s[dr, n, pl.ds(tt, nb, stride=pitch), :]
                bc = b_s[dr, n, pl.ds(tt, nb, stride=pitch), :]
                hnew = a * carry[dr * nblk + n] + bc
                b_s[dr, n, pl.ds(tt, nb, stride=pitch), :] = hnew
                out.append(hnew)
        return tuple(out)

    fin = lax.fori_loop(0, t, body, init)
    k = 0
    for dr in range(2):
        for n in range(nblk):
            st_ref[dr, n] = fin[k]
            k += 1
    outs = (hf_ref, hr_ref)
    for dr in range(2):
        for n in range(nblk):
            for b in range(nb):
                outs[dr][b, :, n * LANES:(n + 1) * LANES] = b_s[dr, n, pl.ds(b * pitch, t), :]


def _lru_scan(xw3, cw, cb, wg, bg, sp, st0, t):
    nb, l, wd = xw3.shape
    nc = l // t
    nblk = wd // LANES
    hb = t // SUBLANES
    nhb = l // SUBLANES
    pitch = t + SUBLANES
    full = lambda a: pl.BlockSpec(a.shape, lambda c: (0,) * a.ndim)
    chunk = lambda f: pl.BlockSpec((nb, t, wd), lambda c: (0, f(c), 0))
    halo_prev = lambda f: pl.BlockSpec((nb, SUBLANES, wd), lambda c: (0, jnp.maximum(f(c) * hb - 1, 0), 0))
    halo_next = lambda f: pl.BlockSpec((nb, SUBLANES, wd), lambda c: (0, jnp.minimum((f(c) + 1) * hb, nhb - 1), 0))
    fwd = lambda c: c
    rev = lambda c: nc - 1 - c
    coef = pltpu.VMEM((2, nblk, nb * pitch, LANES), F32)
    return pl.pallas_call(
        functools.partial(_lru_scan_kernel, pitch=pitch),
        grid=(nc,),
        in_specs=[chunk(fwd), halo_prev(fwd), halo_next(fwd), chunk(rev), halo_prev(rev), halo_next(rev),
                  full(cw), full(cb), full(wg), full(bg), full(sp), full(st0)],
        out_specs=[chunk(fwd), chunk(rev), full(st0)],
        out_shape=[jax.ShapeDtypeStruct((nb, l, wd), F32), jax.ShapeDtypeStruct((nb, l, wd), F32),
                   jax.ShapeDtypeStruct(st0.shape, F32)],
        scratch_shapes=[pltpu.VMEM((nb, t + 2 * SUBLANES, wd), F32), coef, coef],
        compiler_params=_cparams(("arbitrary",)),
        name="lru_scan",
    )(xw3, xw3, xw3, xw3, xw3, xw3, cw, cb, wg, bg, sp, st0)


def _lru_out_kernel(gl_ref, hf_ref, hr_ref, w_ref, o_ref):
    u = gl_ref[...].astype(F32) * (hf_ref[...] + hr_ref[...])
    o_ref[...] = _dot(u.astype(BF16), w_ref[...])


def _lru_out(gl, hf, hr, w, lseg):
    n, wd = gl.shape
    d = w.shape[1]
    tm = _pick_tile(lseg)
    row = lambda: pl.BlockSpec((tm, wd), lambda i: (i, 0))
    return pl.pallas_call(
        _lru_out_kernel,
        grid=(n // tm,),
        in_specs=[row(), row(), row(), pl.BlockSpec(w.shape, lambda i: (0, 0))],
        out_specs=pl.BlockSpec((tm, d), lambda i: (i, 0)),
        out_shape=jax.ShapeDtypeStruct((n, d), F32),
        compiler_params=_cparams(("parallel",)),
        name="lru_out",
    )(gl, hf, hr, w)


def _lru_layer(lat, cx, mod_l, mod_c, nb, g1, w_in, conv_w, conv_b, w_gate, b_gate, lam, w_out, need_ctx):
    l = lat.shape[0] // nb
    lc = cx.shape[0] // nb
    wd = w_in.shape[1] // 2
    nblk = wd // LANES
    w_in_b = w_in.astype(BF16)
    xw, gl = _lru_in(lat, mod_l, l, g1, w_in_b)
    xwc, glc = _lru_in(cx, mod_c, lc, g1, w_in_b)
    wg = (0.5 * jnp.concatenate([w_gate[:, 0], w_gate[:, 1]], axis=-1)).astype(BF16)
    bgate = b_gate.astype(F32).reshape(2, 2, nblk, 1, LANES)
    bg = 0.5 * jnp.concatenate([bgate[:, 0], bgate[:, 1]], axis=-1)
    sp = (-0.5 * LRU_C * math.log2(math.e)) * jax.nn.softplus(-lam.astype(F32)).reshape(2, nblk, 1, LANES)
    cw = conv_w.astype(F32)
    cb = conv_b.astype(F32)[None, :]
    t = _pick_tile(l, lc, cap=64)
    st0 = jnp.zeros((2, nblk, nb, LANES), F32)
    hcf, hcr, st = _lru_scan(xwc.reshape(nb, lc, wd), cw, cb, wg, bg, sp, st0, t)
    hf, hr, _ = _lru_scan(xw.reshape(nb, l, wd), cw, cb, wg, bg, sp, st, t)
    wo = w_out.astype(BF16)
    y = _lru_out(gl, hf.reshape(nb * l, wd), hr.reshape(nb * l, wd), wo, l)
    yc = _lru_out(glc, hcf.reshape(nb * lc, wd), hcr.reshape(nb * lc, wd), wo, lc) if need_ctx else None
    return y, yc


def kernel(x, c, ctx, c_ctx, ada_w, ada_b, norm_g, mla_w_dq, mla_g_q, mla_w_uq, mla_w_dkv, mla_g_kv, mla_w_ukv, mla_g_qk, mla_w_o, s5_a_re, s5_a_im, s5_log_dt, s5_b_re, s5_b_im, s5_c_re, s5_c_im, s5_d, s5_w_glu, lru_w_in, lru_conv_w, lru_conv_b, lru_w_gate, lru_b_gate, lru_lambda, lru_w_out, mlp_w1, mlp_w2):
    nb, l, d = x.shape
    lc = ctx.shape[1]
    depth = ada_w.shape[0]
    assert nb < SUBLANES
    cc = jnp.concatenate([c, c_ctx[None, :], jnp.zeros((SUBLANES - nb - 1, d), F32)], axis=0)
    mod = _ada_mod(cc, ada_w, ada_b)
    lat = x.reshape(nb * l, d)
    cx = ctx.reshape(nb * lc, d)
    for i in range(depth):
        need_ctx = i < depth - 1
        mod_l = mod[i, :nb, None, :]
        mod_c = jnp.broadcast_to(mod[i, nb][None, None, :], (nb, 1, mod.shape[-1]))
        g1 = norm_g[i, 0][None, :]
        g2 = norm_g[i, 1][None, :]
        kind, j = i % N_MIXERS, i // N_MIXERS
        if kind == 0:
            y, yc = _mla_layer(lat, cx, mod_l, mod_c, nb, g1,
                               (mla_w_dq[j], mla_g_q[j], mla_w_uq[j], mla_w_dkv[j], mla_g_kv[j], mla_w_ukv[j],
                                mla_g_qk[j], mla_w_o[j]), need_ctx)
        elif kind == 1:
            y, yc = _s5_layer(lat, cx, mod_l, mod_c, nb, g1, s5_a_re[j], s5_a_im[j], s5_log_dt[j], s5_b_re[j],
                              s5_b_im[j], s5_c_re[j], s5_c_im[j], s5_d[j], s5_w_glu[j], need_ctx)
        else:
            y, yc = _lru_layer(lat, cx, mod_l, mod_c, nb, g1, lru_w_in[j], lru_conv_w[j], lru_conv_b[j],
                               lru_w_gate[j], lru_b_gate[j], lru_lambda[j], lru_w_out[j], need_ctx)
        lat = _mlp(lat, y, mod_l, l, g2, mlp_w1, mlp_w2, i)
        if need_ctx:
            cx = _mlp(cx, yc, mod_c, lc, g2, mlp_w1, mlp_w2, i)
    return lat.reshape(nb, l, d)
```

```python
import functools
import math

import jax
import jax.numpy as jnp
from jax import lax
from jax.experimental import pallas as pl
from jax.experimental.pallas import tpu as pltpu

F32 = jnp.float32
BF16 = jnp.bfloat16

EPS = 1e-6
ROPE_THETA = 10000.0
GRID_W = 64
N_MIXERS = 3

MLA_HEADS = 8
MLA_NOPE = 128
MLA_ROPE = 64
MLA_V = 128
MLA_QK = MLA_NOPE + MLA_ROPE
MLA_SCALE = math.log2(math.e) / math.sqrt(MLA_QK)

S5_GROUP = 16
S5_STATE = 64
LRU_BLOCKS = 10
LRU_BW = 128
LRU_C = 8.0

LANES = 128
SUBLANES = 8
MXU_N = 256
VMEM_LIMIT = 56 * 1024 * 1024


def _cparams(sem):
    return pltpu.CompilerParams(dimension_semantics=sem, vmem_limit_bytes=VMEM_LIMIT)


def _dot(a, b):
    return jnp.dot(a, b, preferred_element_type=F32)


def _dot_nt(a, b):
    return lax.dot_general(a, b, (((1,), (1,)), ((), ())), preferred_element_type=F32)


def _rms(x, g):
    inv = lax.rsqrt(jnp.mean(x * x, axis=-1, keepdims=True) + EPS)
    return x * inv * g


def _gelu(x):
    c = math.sqrt(2.0 / math.pi)
    return x * (0.5 * (1.0 + jnp.tanh(c * (x + 0.044715 * (x * x * x)))))


def _sigmoid(x):
    return 1.0 / (1.0 + jnp.exp(-x))


def _sigmoid_tanh(x):
    return 0.5 * jnp.tanh(0.5 * x) + 0.5


def _pick_tile(*lens, cap=512):
    t = cap
    while any(n % t for n in lens):
        t //= 2
    assert t >= SUBLANES
    return t


def _ada_kernel(c_ref, w_ref, b_ref, o_ref):
    c = c_ref[...]
    s = c * _sigmoid(c)
    w = w_ref[...]
    s_hi = s.astype(BF16)
    s_lo = (s - s_hi.astype(F32)).astype(BF16)
    w_hi = w.astype(BF16)
    w_lo = (w - w_hi.astype(F32)).astype(BF16)
    o_ref[...] = _dot(s_hi, w_hi) + _dot(s_lo, w_hi) + _dot(s_hi, w_lo) + b_ref[...]


def _ada_mod(cc, ada_w, ada_b):
    depth, d, d6 = ada_w.shape
    nk = d6 // d
    return pl.pallas_call(
        _ada_kernel,
        grid=(depth, nk),
        in_specs=[
            pl.BlockSpec((SUBLANES, d), lambda l, k: (0, 0)),
            pl.BlockSpec((None, d, d), lambda l, k: (l, 0, k)),
            pl.BlockSpec((None, 1, d), lambda l, k: (l, 0, k)),
        ],
        out_specs=pl.BlockSpec((None, SUBLANES, d), lambda l, k: (l, 0, k)),
        out_shape=jax.ShapeDtypeStruct((depth, SUBLANES, d6), F32),
        compiler_params=_cparams(("parallel", "parallel")),
        name="ada_mod",
    )(cc, ada_w, ada_b.reshape(depth, 1, d6))


def _mod_spec(k, tm, lseg, d, tile_of=lambda i: i):
    return pl.BlockSpec((None, 1, d), lambda i, *_: ((tile_of(i) * tm) // lseg, 0, k))


def _mlp_kernel(x_ref, y_ref, ga_ref, shm_ref, scm_ref, gm_ref, g2_ref, w1_ref, w2_ref,
                o_ref, w1_s, w2_s):
    step = pl.program_id(0)
    nj = w1_s.shape[0]

    @pl.when(step < nj)
    def _():
        w1_s[step] = w1_ref[...].astype(BF16)
        w2_s[step] = w2_ref[...].astype(BF16)

    @pl.when(step >= nj)
    def _():
        x1 = x_ref[...] + ga_ref[...] * y_ref[...].astype(F32)
        h = (_rms(x1, g2_ref[...]) * (1.0 + scm_ref[...]) + shm_ref[...]).astype(BF16)
        acc = jnp.zeros_like(x1)
        for c in range(nj):
            t = jnp.maximum(_dot(h, w1_s[c]), 0.0)
            acc = acc + _dot((t * t).astype(BF16), w2_s[c])
        o_ref[...] = x1 + gm_ref[...] * acc


def _mlp(x, y, mod, lseg, g2, w1_all, w2_all, layer):
    n, d = x.shape
    dff = w1_all.shape[2]
    tm = _pick_tile(lseg)
    tf = min(dff, 512)
    nj = dff // tf
    tile = lambda s: jnp.maximum(s - nj, 0)
    wblk = lambda s: jnp.minimum(s, nj - 1)
    row = lambda: pl.BlockSpec((tm, d), lambda s: (tile(s), 0))
    return pl.pallas_call(
        _mlp_kernel,
        grid=(nj + n // tm,),
        in_specs=[
            row(), row(),
            _mod_spec(2, tm, lseg, d, tile),
            _mod_spec(3, tm, lseg, d, tile),
            _mod_spec(4, tm, lseg, d, tile),
            _mod_spec(5, tm, lseg, d, tile),
            pl.BlockSpec((1, d), lambda s: (0, 0)),
            pl.BlockSpec((None, d, tf), lambda s: (layer, 0, wblk(s))),
            pl.BlockSpec((None, tf, d), lambda s: (layer, wblk(s), 0)),
        ],
        out_specs=row(),
        out_shape=jax.ShapeDtypeStruct((n, d), F32),
        scratch_shapes=[pltpu.VMEM((nj, d, tf), BF16), pltpu.VMEM((nj, tf, d), BF16)],
        compiler_params=_cparams(("arbitrary",)),
        name="mlp",
    )(x, y, mod, mod, mod, mod, g2, w1_all, w2_all)


def _proj_kernel(a_ref, w_ref, o_ref):
    o_ref[...] = _dot(a_ref[...].astype(BF16), w_ref[...])


def _proj(a, w, lseg):
    n, k = a.shape
    dout = w.shape[1]
    tm = _pick_tile(lseg)
    return pl.pallas_call(
        _proj_kernel,
        grid=(n // tm,),
        in_specs=[pl.BlockSpec((tm, k), lambda i: (i, 0)), pl.BlockSpec((k, dout), lambda i: (0, 0))],
        out_specs=pl.BlockSpec((tm, dout), lambda i: (i, 0)),
        out_shape=jax.ShapeDtypeStruct((n, dout), F32),
        compiler_params=_cparams(("parallel",)),
        name="proj",
    )(a, w)


def _rope(x, cos, sin_signed):
    width = x.shape[-1]
    quarter = MLA_ROPE // 4
    lane = lax.broadcasted_iota(jnp.int32, x.shape, 1)
    even_quarter = ((lane // quarter) % 2) == 0
    rot = jnp.where(even_quarter, pltpu.roll(x, width - quarter, 1), pltpu.roll(x, quarter, 1))
    return x * cos + rot * sin_signed


def _mla_proj_kernel(x_ref, sh_ref, sc_ref, g1_ref, cos_ref, sin_ref,
                     wd_ref, gq_ref, wuq_ref, gkv_ref, wukv_ref,
                     gqn_ref, gqr_ref, gkn_ref, gkr_ref, seg_ref,
                     q_ref, k_ref, v_ref):
    nh = q_ref.shape[0]
    dn = nh * MLA_NOPE
    dr = nh * MLA_ROPE
    kvl = gkv_ref.shape[-1]
    h = (_rms(x_ref[...], g1_ref[...]) * (1.0 + sc_ref[...]) + sh_ref[...]).astype(BF16)
    qlora = gq_ref.shape[-1]
    down = _dot(h, wd_ref[...])
    ql = _rms(down[:, :qlora], gq_ref[...]).astype(BF16)
    q = _dot(ql, wuq_ref[...])
    kv = down[:, qlora:]
    ckv = _rms(kv[:, :kvl], gkv_ref[...]).astype(BF16)
    kvu = _dot(ckv, wukv_ref[...])

    cos = cos_ref[...]
    sin = sin_ref[...]
    reps = dr // LANES
    cos_q = jnp.concatenate([cos] * reps, axis=1)
    sin_q = jnp.concatenate([sin] * reps, axis=1)

    qr = q[:, dn:]
    sq = qr * qr
    sq_hi = sq.astype(BF16)
    sq_lo = (sq - sq_hi.astype(F32)).astype(BF16)
    ms = _dot(sq_hi, seg_ref[...]) + _dot(sq_lo, seg_ref[...])
    qr = qr * lax.rsqrt(ms + EPS) * gqr_ref[...]
    qr = _rope(qr, cos_q, sin_q) * MLA_SCALE

    kr = kv[:, kvl:]
    kr = _rms(kr, gkr_ref[...])
    kr = _rope(kr, cos, sin)[:, :MLA_ROPE]

    for hh in range(nh):
        qn = _rms(q[:, hh * MLA_NOPE:(hh + 1) * MLA_NOPE], gqn_ref[...]) * MLA_SCALE
        qrh = qr[:, hh * MLA_ROPE:(hh + 1) * MLA_ROPE]
        q_ref[hh] = jnp.concatenate([qn, qrh], axis=1).astype(q_ref.dtype)
        kn = _rms(kvu[:, hh * MLA_NOPE:(hh + 1) * MLA_NOPE], gkn_ref[...])
        k_ref[hh] = jnp.concatenate([kn, kr], axis=1).astype(k_ref.dtype)
        v_ref[hh, :, :MLA_V] = kvu[:, dn + hh * MLA_V:dn + (hh + 1) * MLA_V].astype(v_ref.dtype)
        v_ref[hh, :, MLA_V:] = jnp.ones((kvu.shape[0], MLA_V), v_ref.dtype)


def _mla_proj(x, mod, lseg, g1, cos_t, sin_t, w):
    n, d = x.shape
    tm = _pick_tile(lseg, cap=512)
    nrope = cos_t.shape[0] // tm
    full = lambda a: pl.BlockSpec(a.shape, lambda i: (0,) * a.ndim)
    weights = [w["wd"], w["gq"], w["wuq"], w["gkv"], w["wukv"],
               w["gqn"], w["gqr"], w["gkn"], w["gkr"], w["seg"]]
    return pl.pallas_call(
        _mla_proj_kernel,
        grid=(n // tm,),
        in_specs=[
            pl.BlockSpec((tm, d), lambda i: (i, 0)),
            _mod_spec(0, tm, lseg, d),
            _mod_spec(1, tm, lseg, d),
            pl.BlockSpec((1, d), lambda i: (0, 0)),
            pl.BlockSpec((tm, LANES), lambda i: (i % nrope, 0)),
            pl.BlockSpec((tm, LANES), lambda i: (i % nrope, 0)),
        ] + [full(a) for a in weights],
        out_specs=[
            pl.BlockSpec((MLA_HEADS, tm, MLA_QK), lambda i: (0, i, 0)),
            pl.BlockSpec((MLA_HEADS, tm, MLA_QK), lambda i: (0, i, 0)),
            pl.BlockSpec((MLA_HEADS, tm, 2 * MLA_V), lambda i: (0, i, 0)),
        ],
        out_shape=[
            jax.ShapeDtypeStruct((MLA_HEADS, n, MLA_QK), BF16),
            jax.ShapeDtypeStruct((MLA_HEADS, n, MLA_QK), BF16),
            jax.ShapeDtypeStruct((MLA_HEADS, n, 2 * MLA_V), BF16),
        ],
        compiler_params=_cparams(("parallel",)),
        name="mla_proj",
    )(x, mod, mod, g1, cos_t, sin_t, *weights)


def _attn_lat_kernel(q_ref, kc_ref, kl_ref, vc_ref, vl_ref, o_ref, sa_s, sb_s, ma_s, mb_s, *, tq, kb):
    lq = q_ref.shape[0]
    lc = kc_ref.shape[0]
    n_tiles = lq // tq
    blocks = [(kc_ref, vc_ref, 0, lc, 0)] + [(kl_ref, vl_ref, j * kb, kb, lc + j * kb)
                                             for j in range(kl_ref.shape[0] // kb)]
    slots = ((sa_s, ma_s), (sb_s, mb_s))

    def iteration(i, slot, first, second):
        s_w, m_w = slots[slot]
        s_r, m_r = slots[1 - slot]
        if first:
            q = q_ref[pl.ds(pl.multiple_of(i * tq, tq), tq), :]
            mp = jnp.full((tq, LANES), -jnp.inf, F32)
        if second:
            m = m_r[...]
            acc = jnp.zeros((tq, 2 * MLA_V), F32)
        for k_ref, v_ref, start, size, off in blocks:
            if first:
                s = _dot_nt(q, k_ref[start:start + size, :])
                s_w[:, off:off + size] = s
                for c in range(size // LANES):
                    mp = jnp.maximum(mp, s[:, c * LANES:(c + 1) * LANES])
            if second:
                p = jnp.concatenate(
                    [jnp.exp2(s_r[:, off + c * LANES:off + (c + 1) * LANES] - m)
                     for c in range(size // LANES)], axis=1).astype(BF16)
                acc = acc + _dot(p, v_ref[start:start + size, :])
        if first:
            m_w[...] = jnp.broadcast_to(jnp.max(mp, axis=-1, keepdims=True), (tq, LANES))
        if second:
            rows = pl.ds(pl.multiple_of((i - 1) * tq, tq), tq)
            o_ref[rows, :] = (acc[:, :MLA_V] / acc[:, MLA_V:]).astype(o_ref.dtype)

    iteration(0, 0, True, False)

    def pair(j, carry):
        iteration(2 * j + 1, 1, True, True)
        iteration(2 * j + 2, 0, True, True)
        return carry

    lax.fori_loop(0, (n_tiles - 1) // 2, pair, 0)
    if (n_tiles - 1) % 2:
        iteration(n_tiles - 1, 1, True, True)
    iteration(n_tiles, n_tiles % 2, False, True)


def _attn_ctx_kernel(q_ref, kc_ref, vc_ref, o_ref):
    sc = _dot_nt(q_ref[...], kc_ref[...])
    m = jnp.max(sc, axis=-1, keepdims=True)
    o = _dot(jnp.exp2(sc - m).astype(BF16), vc_ref[...])
    o_ref[...] = (o[:, :MLA_V] / o[:, MLA_V:]).astype(o_ref.dtype)


def _attn_lat(q, kl, vl, kc, vc, nb):
    nh, n, _ = q.shape
    l = n // nb
    lc = kc.shape[1] // nb
    tq = _pick_tile(l, cap=256)
    kb = _pick_tile(l, cap=MXU_N)
    assert lc % LANES == 0
    return pl.pallas_call(
        functools.partial(_attn_lat_kernel, tq=tq, kb=kb),
        grid=(nb, nh),
        in_specs=[
            pl.BlockSpec((None, l, MLA_QK), lambda b, h: (h, b, 0)),
            pl.BlockSpec((None, lc, MLA_QK), lambda b, h: (h, b, 0)),
            pl.BlockSpec((None, l, MLA_QK), lambda b, h: (h, b, 0)),
            pl.BlockSpec((None, lc, 2 * MLA_V), lambda b, h: (h, b, 0)),
            pl.BlockSpec((None, l, 2 * MLA_V), lambda b, h: (h, b, 0)),
        ],
        out_specs=pl.BlockSpec((l, MLA_V), lambda b, h: (b, h)),
        out_shape=jax.ShapeDtypeStruct((n, nh * MLA_V), BF16),
        scratch_shapes=[pltpu.VMEM((tq, lc + l), F32), pltpu.VMEM((tq, lc + l), F32),
                        pltpu.VMEM((tq, LANES), F32), pltpu.VMEM((tq, LANES), F32)],
        compiler_params=_cparams(("parallel", "parallel")),
        name="attn_lat",
    )(q, kc, kl, vc, vl)


def _attn_ctx(q, kc, vc, nb):
    nh, n, _ = q.shape
    lc = n // nb
    return pl.pallas_call(
        _attn_ctx_kernel,
        grid=(nb, nh),
        in_specs=[
            pl.BlockSpec((None, lc, MLA_QK), lambda b, h: (h, b, 0)),
            pl.BlockSpec((None, lc, MLA_QK), lambda b, h: (h, b, 0)),
            pl.BlockSpec((None, lc, 2 * MLA_V), lambda b, h: (h, b, 0)),
        ],
        out_specs=pl.BlockSpec((lc, MLA_V), lambda b, h: (b, h)),
        out_shape=jax.ShapeDtypeStruct((n, nh * MLA_V), BF16),
        compiler_params=_cparams(("parallel", "parallel")),
        name="attn_ctx",
    )(q, kc, vc)


def _rope_tables(n_tok):
    rows = n_tok // GRID_W
    row = jnp.repeat(jnp.arange(rows, dtype=F32), GRID_W)
    col = jnp.tile(jnp.arange(GRID_W, dtype=F32), rows)
    n_freq = MLA_ROPE // 4
    freqs = ROPE_THETA ** (-jnp.arange(n_freq, dtype=F32) / n_freq)
    ang_r = row[:, None] * freqs[None, :]
    ang_c = col[:, None] * freqs[None, :]
    ang = jnp.concatenate([ang_r, ang_r, ang_c, ang_c], axis=-1)
    sign = jnp.tile(jnp.repeat(jnp.array([-1.0, 1.0], F32), n_freq), 2)
    cos = jnp.cos(ang)
    sin = jnp.sin(ang) * sign
    return jnp.concatenate([cos, cos], axis=1), jnp.concatenate([sin, sin], axis=1)


def _mla_weights(w_dq, g_q, w_uq, w_dkv, g_kv, w_ukv, g_qk, w_o):
    nh = MLA_HEADS
    qlora = w_dq.shape[1]
    kvl = g_kv.shape[0]
    uq = w_uq.reshape(qlora, nh, MLA_QK)
    wuq = jnp.concatenate([uq[:, :, :MLA_NOPE].reshape(qlora, nh * MLA_NOPE),
                           uq[:, :, MLA_NOPE:].reshape(qlora, nh * MLA_ROPE)], axis=1)
    ukv = w_ukv.reshape(kvl, nh, MLA_NOPE + MLA_V)
    wukv = jnp.concatenate([ukv[:, :, :MLA_NOPE].reshape(kvl, nh * MLA_NOPE),
                            ukv[:, :, MLA_NOPE:].reshape(kvl, nh * MLA_V)], axis=1)
    wdkv = jnp.concatenate([w_dkv, w_dkv[:, kvl:]], axis=1)
    dr = nh * MLA_ROPE
    seg_id = jnp.arange(dr) // MLA_ROPE
    seg = (seg_id[:, None] == seg_id[None, :]).astype(F32) / MLA_ROPE
    return {
        "wd": jnp.concatenate([w_dq, wdkv], axis=1).astype(BF16), "gq": g_q[None, :], "wuq": wuq.astype(BF16),
        "gkv": g_kv[None, :], "wukv": wukv.astype(BF16),
        "gqn": g_qk[0:1, :MLA_NOPE], "gqr": jnp.tile(g_qk[0:1, MLA_NOPE:], (1, nh)),
        "gkn": g_qk[1:2, :MLA_NOPE], "gkr": jnp.tile(g_qk[1:2, MLA_NOPE:], (1, 2)),
        "seg": seg.astype(BF16), "wo": w_o.astype(BF16),
    }


def _mla_layer(lat, cx, mod_l, mod_c, nb, g1, wts, need_ctx):
    l = lat.shape[0] // nb
    lc = cx.shape[0] // nb
    w = _mla_weights(*wts)
    cos_t, sin_t = _rope_tables(l)
    tmc = _pick_tile(lc, cap=512)
    ql, kl, vl = _mla_proj(lat, mod_l, l, g1, cos_t, sin_t, w)
    qc, kc, vc = _mla_proj(cx, mod_c, lc, g1, jnp.ones((tmc, LANES), F32), jnp.zeros((tmc, LANES), F32), w)
    o = _attn_lat(ql, kl, vl, kc, vc, nb)
    y = _proj(o, w["wo"], l)
    yc = None
    if need_ctx:
        yc = _proj(_attn_ctx(qc, kc, vc, nb), w["wo"], lc)
    return y, yc


def _s5_kernel(xf_ref, xr_ref, sh_ref, sc_ref, g1_ref, dd_ref, wb_ref, wc_ref, are_ref, aim_ref, st0_ref,
               yf_ref, yr_ref, st_ref, in_s, out_s, *, pitch):
    nb, t, d = xf_ref.shape
    nt = st_ref.shape[2]
    nre = nt // 2
    nv = nre // SUBLANES
    nj = wb_ref.shape[1]
    kblocks = d // LANES
    per_kb = (nj // 2) // kblocks
    nq = wc_ref.shape[1]
    tiles_q = nre // nq
    step = pl.program_id(0)

    @pl.when(step == 0)
    def _():
        st_ref[...] = st0_ref[...]

    def prep(x_ref):
        return _rms(x_ref[...], g1_ref[...]) * (1.0 + sc_ref[...]) + sh_ref[...]

    hs = (prep(xf_ref), prep(xr_ref))
    skip = hs[0] * dd_ref[...]
    outs = (yf_ref, yr_ref)
    bpl = 2 if nb % 2 == 0 else 1

    def slab(b, n):
        return (b * nt + n) * pitch

    for dr in range(2):
        hb = hs[dr].reshape(nb * t, d).astype(BF16)
        for j in range(nj):
            kb = (j % (nj // 2)) // per_kb
            r = _dot(hb[:, kb * LANES:(kb + 1) * LANES], wb_ref[dr, j])
            for b in range(nb):
                for half in range(MXU_N // LANES):
                    n = j * (MXU_N // LANES) + half
                    in_s[pl.ds(slab(b, n), t), :] = r[b * t:(b + 1) * t, half * LANES:(half + 1) * LANES]

        consts = [(are_ref[dr, v * SUBLANES:(v + 1) * SUBLANES, :], aim_ref[dr, v * SUBLANES:(v + 1) * SUBLANES, :])
                  for v in range(nv)]
        for b0 in range(0, nb, bpl):
            init = tuple(st_ref[b0 + bb, dr, part * nre + v * SUBLANES:part * nre + (v + 1) * SUBLANES, :]
                         for bb in range(bpl) for v in range(nv) for part in range(2))

            def body(i, carry, b0=b0, dr=dr, consts=consts):
                tt, carry = carry[0], carry[1:]
                out = [tt + (1 if dr == 0 else -1)]
                for bb in range(bpl):
                    for v in range(nv):
                        sre, sim = carry[(bb * nv + v) * 2], carry[(bb * nv + v) * 2 + 1]
                        are, aim = consts[v]
                        row_re = slab(b0 + bb, v * SUBLANES) + tt
                        row_im = slab(b0 + bb, nre + v * SUBLANES) + tt
                        bre = in_s[pl.ds(row_re, SUBLANES, stride=pitch), :]
                        bim = in_s[pl.ds(row_im, SUBLANES, stride=pitch), :]
                        new_re = are * sre - aim * sim + bre
                        new_im = are * sim + aim * sre + bim
                        out_s[pl.ds(row_re, SUBLANES, stride=pitch), :] = new_re
                        out_s[pl.ds(row_im, SUBLANES, stride=pitch), :] = new_im
                        out += [new_re, new_im]
                return tuple(out)

            t_first = jnp.int32(0 if dr == 0 else t - 1)
            fin = lax.fori_loop(0, t, body, (t_first,) + init, unroll=2)[1:]
            k = 0
            for bb in range(bpl):
                for v in range(nv):
                    for part in range(2):
                        st_ref[b0 + bb, dr, part * nre + v * SUBLANES:part * nre + (v + 1) * SUBLANES, :] = fin[k]
                        k += 1

        for q in range(nq):
            rows = []
            for b in range(nb):
                pieces = [out_s[pl.ds(slab(b, part * nre + q * tiles_q + i), t), :]
                          for part in range(2) for i in range(tiles_q)]
                rows.append(jnp.concatenate(pieces, axis=1))
            yq = _dot(jnp.concatenate(rows, axis=0).astype(BF16), wc_ref[dr, q])
            for b in range(nb):
                val = yq[b * t:(b + 1) * t, :]
                if dr == 0:
                    val = val + skip[b, :, q * LANES:(q + 1) * LANES]
                outs[dr][b, :, q * LANES:(q + 1) * LANES] = val


def _s5_scan(x3, mod3, g1, dd, wb, wc, are, aim, st0, t):
    nb, l, d = x3.shape
    nc = l // t
    pitch = t + SUBLANES // 2
    nt = st0.shape[2]
    full = lambda a: pl.BlockSpec(a.shape, lambda c: (0,) * a.ndim)
    buf = pltpu.VMEM((nb * nt * pitch, LANES), F32)
    return pl.pallas_call(
        functools.partial(_s5_kernel, pitch=pitch),
        grid=(nc,),
        in_specs=[
            pl.BlockSpec((nb, t, d), lambda c: (0, c, 0)),
            pl.BlockSpec((nb, t, d), lambda c: (0, nc - 1 - c, 0)),
            pl.BlockSpec((nb, 1, d), lambda c: (0, 0, 0)),
            pl.BlockSpec((nb, 1, d), lambda c: (0, 0, 1)),
            full(g1), full(dd), full(wb), full(wc), full(are), full(aim), full(st0),
        ],
        out_specs=[
            pl.BlockSpec((nb, t, d), lambda c: (0, c, 0)),
            pl.BlockSpec((nb, t, d), lambda c: (0, nc - 1 - c, 0)),
            full(st0),
        ],
        out_shape=[
            jax.ShapeDtypeStruct((nb, l, d), F32),
            jax.ShapeDtypeStruct((nb, l, d), F32),
            jax.ShapeDtypeStruct(st0.shape, F32),
        ],
        scratch_shapes=[buf, buf],
        compiler_params=_cparams(("arbitrary",)),
        name="s5_scan",
    )(x3, x3, mod3, mod3, g1, dd, wb, wc, are, aim, st0)


def _s5_glu_kernel(yf_ref, yr_ref, w_ref, o_ref):
    d = o_ref.shape[-1]
    z = _dot(_gelu(yf_ref[...] + yr_ref[...]).astype(BF16), w_ref[...])
    o_ref[...] = z[:, :d] * _sigmoid_tanh(z[:, d:])


def _s5_glu(yf, yr, w, lseg):
    n, d = yf.shape
    tm = _pick_tile(lseg)
    return pl.pallas_call(
        _s5_glu_kernel,
        grid=(n // tm,),
        in_specs=[pl.BlockSpec((tm, d), lambda i: (i, 0)), pl.BlockSpec((tm, d), lambda i: (i, 0)),
                  pl.BlockSpec(w.shape, lambda i: (0, 0))],
        out_specs=pl.BlockSpec((tm, d), lambda i: (i, 0)),
        out_shape=jax.ShapeDtypeStruct((n, d), F32),
        compiler_params=_cparams(("parallel",)),
        name="s5_glu",
    )(yf, yr, w)


def _s5_weights(a_re, a_im, log_dt, b_re, b_im, c_re, c_im):
    ng, ns = a_re.shape[1:]
    gi = b_re.shape[-1]
    d = ng * gi
    lr, li = a_re.astype(F32), a_im.astype(F32)
    dt = jnp.exp(log_dt.astype(F32))[..., None]
    mag = jnp.exp(lr * dt)
    ar, ai = mag * jnp.cos(li * dt), mag * jnp.sin(li * dt)
    den = lr * lr + li * li
    wr = ((ar - 1.0) * lr + ai * li) / den
    wi = (ai * lr - (ar - 1.0) * li) / den
    br, bi = b_re.astype(F32), b_im.astype(F32)
    bb_re = wr[..., None] * br - wi[..., None] * bi
    bb_im = wr[..., None] * bi + wi[..., None] * br

    gpt = MXU_N // ns
    gpk = LANES // gi
    ncol = ng // gpt
    sel = (jnp.arange(gpk)[None, :, None]
           == gpt * (jnp.arange(ncol)[:, None, None] % (gpk // gpt)) + jnp.arange(gpt)[None, None, :]).astype(F32)

    def in_tiles(part):
        blk = part.reshape(2, ncol, gpt, ns, gi).transpose(0, 1, 4, 2, 3)
        tiles = blk[:, :, None, :, :, :] * sel[None, :, :, None, :, None]
        return tiles.reshape(2, ncol, gpk * gi, gpt * ns)

    wb = jnp.concatenate([in_tiles(bb_re), in_tiles(bb_im)], axis=1).astype(BF16)

    nq = d // LANES
    gpq = ng // nq
    eye = jnp.eye(gpq, dtype=F32)

    def out_tiles(part):
        blk = jnp.swapaxes(part.astype(F32), -1, -2).reshape(2, nq, gpq, ns, gi)
        tiles = blk[:, :, :, :, None, :] * eye[None, None, :, None, :, None]
        return tiles.reshape(2, nq, gpq * ns, gpq * gi)

    wc = jnp.concatenate([out_tiles(c_re), out_tiles(-c_im)], axis=2).astype(BF16)
    are = ar.reshape(2, ng * ns // LANES, LANES)
    aim = ai.reshape(2, ng * ns // LANES, LANES)
    return wb, wc, are, aim


def _s5_layer(lat, cx, mod_l, mod_c, nb, g1, a_re, a_im, log_dt, b_re, b_im, c_re, c_im, dvec, w_glu, need_ctx):
    d = lat.shape[1]
    l = lat.shape[0] // nb
    lc = cx.shape[0] // nb
    wb, wc, are, aim = _s5_weights(a_re, a_im, log_dt, b_re, b_im, c_re, c_im)
    t = _pick_tile(l, lc, cap=64)
    st0 = jnp.zeros((nb, 2, 2 * are.shape[1], LANES), F32)
    dd = dvec[None, :].astype(F32)
    ycf, ycr, st = _s5_scan(cx.reshape(nb, lc, d), mod_c, g1, dd, wb, wc, are, aim, st0, t)
    yf, yr, _ = _s5_scan(lat.reshape(nb, l, d), mod_l, g1, dd, wb, wc, are, aim, st, t)
    wg = w_glu.astype(BF16)
    y = _s5_glu(yf.reshape(nb * l, d), yr.reshape(nb * l, d), wg, l)
    yc = _s5_glu(ycf.reshape(nb * lc, d), ycr.reshape(nb * lc, d), wg, lc) if need_ctx else None
    return y, yc


def _lru_in_kernel(x_ref, sh_ref, sc_ref, g1_ref, w_ref, xw_ref, gl_ref):
    wd = xw_ref.shape[-1]
    h = (_rms(x_ref[...], g1_ref[...]) * (1.0 + sc_ref[...]) + sh_ref[...]).astype(BF16)
    z = _dot(h, w_ref[...])
    gl_ref[...] = _gelu(z[:, :wd]).astype(gl_ref.dtype)
    xw_ref[...] = z[:, wd:]


def _lru_in(x, mod, lseg, g1, w_in):
    n, d = x.shape
    wd = w_in.shape[1] // 2
    tm = _pick_tile(lseg)
    return pl.pallas_call(
        _lru_in_kernel,
        grid=(n // tm,),
        in_specs=[pl.BlockSpec((tm, d), lambda i: (i, 0)), _mod_spec(0, tm, lseg, d), _mod_spec(1, tm, lseg, d),
                  pl.BlockSpec((1, d), lambda i: (0, 0)), pl.BlockSpec(w_in.shape, lambda i: (0, 0))],
        out_specs=[pl.BlockSpec((tm, wd), lambda i: (i, 0)), pl.BlockSpec((tm, wd), lambda i: (i, 0))],
        out_shape=[jax.ShapeDtypeStruct((n, wd), F32), jax.ShapeDtypeStruct((n, wd), BF16)],
        compiler_params=_cparams(("parallel",)),
        name="lru_in",
    )(x, mod, mod, g1, w_in)


def _lru_scan_kernel(xf_ref, xfp_ref, xfn_ref, xr_ref, xrp_ref, xrn_ref,
                     cw_ref, cb_ref, wg_ref, bg_ref, sp_ref, st0_ref,
                     hf_ref, hr_ref, st_ref, ext_s, a_s, b_s, h_s, *, pitch):
    nb, t, wd = xf_ref.shape
    nblk = wd // LANES
    halo = SUBLANES
    step = pl.program_id(0)
    last = pl.num_programs(0) - 1

    @pl.when(step == 0)
    def _():
        st_ref[...] = st0_ref[...]

    def coeffs(dr, x_ref, prev_ref, next_ref, at_start, at_end):
        ext_s[:, 0:halo, :] = jnp.where(at_start, 0.0, prev_ref[...])
        ext_s[:, halo:halo + t, :] = x_ref[...]
        ext_s[:, halo + t:2 * halo + t, :] = jnp.where(at_end, 0.0, next_ref[...])
        xr = cb_ref[...]
        for tap in range(cw_ref.shape[0]):
            xr = xr + cw_ref[tap:tap + 1, :] * ext_s[:, halo - 1 + tap:halo - 1 + tap + t, :]
        for n in range(nblk):
            xb = xr[:, :, n * LANES:(n + 1) * LANES].reshape(nb * t, LANES)
            g = _dot(xb.astype(BF16), wg_ref[dr, n]) + bg_ref[dr, n]
            c1 = sp_ref[dr, n]
            a = jnp.exp2(c1 * jnp.tanh(g[:, :LANES]) + c1)
            bc = jnp.sqrt(1.0 - a * a) * ((0.5 * xb) * (jnp.tanh(g[:, LANES:]) + 1.0))
            for b in range(nb):
                a_s[dr, n, pl.ds(b * pitch, t), :] = a[b * t:(b + 1) * t]
                b_s[dr, n, pl.ds(b * pitch, t), :] = bc[b * t:(b + 1) * t]

    coeffs(0, xf_ref, xfp_ref, xfn_ref, step == 0, step == last)
    coeffs(1, xr_ref, xrp_ref, xrn_ref, step == last, step == 0)

    init = tuple(st_ref[dr, n] for dr in range(2) for n in range(nblk))

    def body(i, carry):
        times, carry = carry[:2], carry[2:]
        out = [times[0] + 1, times[1] - 1]
        for dr in range(2):
            tt = times[dr]
            for n in range(nblk):
                a = a_s[dr, n, pl.ds(tt, nb, stride=pitch), :]
                bc = b_s[dr, n, pl.ds(tt, nb, stride=pitch), :]
                hnew = a * carry[dr * nblk + n] + bc
                h_s[dr, n, pl.ds(tt, nb, stride=pitch), :] = hnew
                out.append(hnew)
        return tuple(out)

    fin = lax.fori_loop(0, t, body, (jnp.int32(0), jnp.int32(t - 1)) + init, unroll=2)[2:]
    k = 0
    for dr in range(2):
        for n in range(nblk):
            st_ref[dr, n] = fin[k]
            k += 1
    outs = (hf_ref, hr_ref)
    for dr in range(2):
        for n in range(nblk):
            for b in range(nb):
                outs[dr][b, :, n * LANES:(n + 1) * LANES] = h_s[dr, n, pl.ds(b * pitch, t), :]


def _lru_scan(xw3, cw, cb, wg, bg, sp, st0, t):
    nb, l, wd = xw3.shape
    nc = l // t
    nblk = wd // LANES
    hb = t // SUBLANES
    nhb = l // SUBLANES
    pitch = t + SUBLANES
    full = lambda a: pl.BlockSpec(a.shape, lambda c: (0,) * a.ndim)
    chunk = lambda f: pl.BlockSpec((nb, t, wd), lambda c: (0, f(c), 0))
    halo_prev = lambda f: pl.BlockSpec((nb, SUBLANES, wd), lambda c: (0, jnp.maximum(f(c) * hb - 1, 0), 0))
    halo_next = lambda f: pl.BlockSpec((nb, SUBLANES, wd), lambda c: (0, jnp.minimum((f(c) + 1) * hb, nhb - 1), 0))
    fwd = lambda c: c
    rev = lambda c: nc - 1 - c
    coef = pltpu.VMEM((2, nblk, nb * pitch, LANES), F32)
    return pl.pallas_call(
        functools.partial(_lru_scan_kernel, pitch=pitch),
        grid=(nc,),
        in_specs=[chunk(fwd), halo_prev(fwd), halo_next(fwd), chunk(rev), halo_prev(rev), halo_next(rev),
                  full(cw), full(cb), full(wg), full(bg), full(sp), full(st0)],
        out_specs=[chunk(fwd), chunk(rev), full(st0)],
        out_shape=[jax.ShapeDtypeStruct((nb, l, wd), F32), jax.ShapeDtypeStruct((nb, l, wd), F32),
                   jax.ShapeDtypeStruct(st0.shape, F32)],
        scratch_shapes=[pltpu.VMEM((nb, t + 2 * SUBLANES, wd), F32), coef, coef, coef],
        compiler_params=_cparams(("arbitrary",)),
        name="lru_scan",
    )(xw3, xw3, xw3, xw3, xw3, xw3, cw, cb, wg, bg, sp, st0)


def _lru_out_kernel(gl_ref, hf_ref, hr_ref, w_ref, o_ref):
    u = gl_ref[...].astype(F32) * (hf_ref[...] + hr_ref[...])
    o_ref[...] = _dot(u.astype(BF16), w_ref[...])


def _lru_out(gl, hf, hr, w, lseg):
    n, wd = gl.shape
    d = w.shape[1]
    tm = _pick_tile(lseg)
    row = lambda: pl.BlockSpec((tm, wd), lambda i: (i, 0))
    return pl.pallas_call(
        _lru_out_kernel,
        grid=(n // tm,),
        in_specs=[row(), row(), row(), pl.BlockSpec(w.shape, lambda i: (0, 0))],
        out_specs=pl.BlockSpec((tm, d), lambda i: (i, 0)),
        out_shape=jax.ShapeDtypeStruct((n, d), F32),
        compiler_params=_cparams(("parallel",)),
        name="lru_out",
    )(gl, hf, hr, w)


def _lru_layer(lat, cx, mod_l, mod_c, nb, g1, w_in, conv_w, conv_b, w_gate, b_gate, lam, w_out, need_ctx):
    l = lat.shape[0] // nb
    lc = cx.shape[0] // nb
    wd = w_in.shape[1] // 2
    nblk = wd // LANES
    w_in_b = w_in.astype(BF16)
    xw, gl = _lru_in(lat, mod_l, l, g1, w_in_b)
    xwc, glc = _lru_in(cx, mod_c, lc, g1, w_in_b)
    wg = (0.5 * jnp.concatenate([w_gate[:, 0], w_gate[:, 1]], axis=-1)).astype(BF16)
    bgate = b_gate.astype(F32).reshape(2, 2, nblk, 1, LANES)
    bg = 0.5 * jnp.concatenate([bgate[:, 0], bgate[:, 1]], axis=-1)
    sp = (-0.5 * LRU_C * math.log2(math.e)) * jax.nn.softplus(-lam.astype(F32)).reshape(2, nblk, 1, LANES)
    cw = conv_w.astype(F32)
    cb = conv_b.astype(F32)[None, :]
    t = _pick_tile(l, lc, cap=64)
    st0 = jnp.zeros((2, nblk, nb, LANES), F32)
    hcf, hcr, st = _lru_scan(xwc.reshape(nb, lc, wd), cw, cb, wg, bg, sp, st0, t)
    hf, hr, _ = _lru_scan(xw.reshape(nb, l, wd), cw, cb, wg, bg, sp, st, t)
    wo = w_out.astype(BF16)
    y = _lru_out(gl, hf.reshape(nb * l, wd), hr.reshape(nb * l, wd), wo, l)
    yc = _lru_out(glc, hcf.reshape(nb * lc, wd), hcr.reshape(nb * lc, wd), wo, lc) if need_ctx else None
    return y, yc


def kernel(x, c, ctx, c_ctx, ada_w, ada_b, norm_g, mla_w_dq, mla_g_q, mla_w_uq, mla_w_dkv, mla_g_kv, mla_w_ukv, mla_g_qk, mla_w_o, s5_a_re, s5_a_im, s5_log_dt, s5_b_re, s5_b_im, s5_c_re, s5_c_im, s5_d, s5_w_glu, lru_w_in, lru_conv_w, lru_conv_b, lru_w_gate, lru_b_gate, lru_lambda, lru_w_out, mlp_w1, mlp_w2):
    nb, l, d = x.shape
    lc = ctx.shape[1]
    depth = ada_w.shape[0]
    assert nb < SUBLANES
    cc = jnp.concatenate([c, c_ctx[None, :], jnp.zeros((SUBLANES - nb - 1, d), F32)], axis=0)
    mod = _ada_mod(cc, ada_w, ada_b)
    lat = x.reshape(nb * l, d)
    cx = ctx.reshape(nb * lc, d)
    for i in range(depth):
        need_ctx = i < depth - 1
        mod_l = mod[i, :nb, None, :]
        mod_c = jnp.broadcast_to(mod[i, nb][None, None, :], (nb, 1, mod.shape[-1]))
        g1 = norm_g[i, 0][None, :]
        g2 = norm_g[i, 1][None, :]
        kind, j = i % N_MIXERS, i // N_MIXERS
        if kind == 0:
            y, yc = _mla_layer(lat, cx, mod_l, mod_c, nb, g1,
                               (mla_w_dq[j], mla_g_q[j], mla_w_uq[j], mla_w_dkv[j], mla_g_kv[j], mla_w_ukv[j],
                                mla_g_qk[j], mla_w_o[j]), need_ctx)
        elif kind == 1:
            y, yc = _s5_layer(lat, cx, mod_l, mod_c, nb, g1, s5_a_re[j], s5_a_im[j], s5_log_dt[j], s5_b_re[j],
                              s5_b_im[j], s5_c_re[j], s5_c_im[j], s5_d[j], s5_w_glu[j], need_ctx)
        else:
            y, yc = _lru_layer(lat, cx, mod_l, mod_c, nb, g1, lru_w_in[j], lru_conv_w[j], lru_conv_b[j],
                               lru_w_gate[j], lru_b_gate[j], lru_lambda[j], lru_w_out[j], need_ctx)
        lat = _mlp(lat, y, mod_l, l, g2, mlp_w1, mlp_w2, i)
        if need_ctx:
            cx = _mlp(cx, yc, mod_c, lc, g2, mlp_w1, mlp_w2, i)
    return lat.reshape(nb, l, d)
```

```python
import functools
import math

import jax
import jax.numpy as jnp
import numpy as np
from jax import lax
from jax.experimental import pallas as pl
from jax.experimental.pallas import tpu as pltpu

F32 = jnp.float32
BF16 = jnp.bfloat16

EPS = 1e-6
ROPE_THETA = 10000.0
GRID_W = 64
N_MIXERS = 3

MLA_HEADS = 8
MLA_NOPE = 128
MLA_ROPE = 64
MLA_V = 128
MLA_QK = MLA_NOPE + MLA_ROPE
MLA_SCALE = math.log2(math.e) / math.sqrt(MLA_QK)

S5_GROUP = 16
S5_STATE = 64
LRU_BLOCKS = 10
LRU_BW = 128
LRU_C = 8.0

LANES = 128
SUBLANES = 8
MXU_N = 256
VMEM_LIMIT = 56 * 1024 * 1024


def _cparams(sem):
    return pltpu.CompilerParams(dimension_semantics=sem, vmem_limit_bytes=VMEM_LIMIT)


def _dot(a, b):
    return jnp.dot(a, b, preferred_element_type=F32)


def _dot_nt(a, b):
    return lax.dot_general(a, b, (((1,), (1,)), ((), ())), preferred_element_type=F32)


def _rms(x, g):
    inv = lax.rsqrt(jnp.mean(x * x, axis=-1, keepdims=True) + EPS)
    return x * inv * g


def _gelu(x):
    c = math.sqrt(2.0 / math.pi)
    return x * (0.5 * (1.0 + jnp.tanh(c * (x + 0.044715 * (x * x * x)))))


def _sigmoid(x):
    return 1.0 / (1.0 + jnp.exp(-x))


def _sigmoid_tanh(x):
    return 0.5 * jnp.tanh(0.5 * x) + 0.5


def _pick_tile(*lens, cap=512):
    t = cap
    while any(n % t for n in lens):
        t //= 2
    assert t >= SUBLANES
    return t


def _ada_kernel(c_ref, w_ref, b_ref, o_ref):
    c = c_ref[...]
    s = c * _sigmoid(c)
    w = w_ref[...]
    s_hi = s.astype(BF16)
    s_lo = (s - s_hi.astype(F32)).astype(BF16)
    w_hi = w.astype(BF16)
    w_lo = (w - w_hi.astype(F32)).astype(BF16)
    o_ref[...] = _dot(s_hi, w_hi) + _dot(s_lo, w_hi) + _dot(s_hi, w_lo) + b_ref[...]


def _ada_mod(cc, ada_w, ada_b):
    depth, d, d6 = ada_w.shape
    nk = d6 // d
    return pl.pallas_call(
        _ada_kernel,
        grid=(depth, nk),
        in_specs=[
            pl.BlockSpec((SUBLANES, d), lambda l, k: (0, 0)),
            pl.BlockSpec((None, d, d), lambda l, k: (l, 0, k)),
            pl.BlockSpec((None, 1, d), lambda l, k: (l, 0, k)),
        ],
        out_specs=pl.BlockSpec((None, SUBLANES, d), lambda l, k: (l, 0, k)),
        out_shape=jax.ShapeDtypeStruct((depth, SUBLANES, d6), F32),
        compiler_params=_cparams(("parallel", "parallel")),
        name="ada_mod",
    )(cc, ada_w, ada_b.reshape(depth, 1, d6))


def _mod_spec(k, tm, lseg, d, tile_of=lambda i: i):
    return pl.BlockSpec((None, 1, d), lambda i, *_: ((tile_of(i) * tm) // lseg, 0, k))


def _mlp_kernel(*refs, has_wo):
    if has_wo:
        (x_ref, y_ref, ga_ref, shm_ref, scm_ref, gm_ref, g2_ref, w1_ref, w2_ref, wo_ref,
         o_ref, w1_s, w2_s, wo_s) = refs
    else:
        x_ref, y_ref, ga_ref, shm_ref, scm_ref, gm_ref, g2_ref, w1_ref, w2_ref, o_ref, w1_s, w2_s = refs
    step = pl.program_id(0)
    nj = w1_s.shape[0]

    @pl.when(step < nj)
    def _():
        w1_s[step] = w1_ref[...].astype(BF16)
        w2_s[step] = w2_ref[...].astype(BF16)

    if has_wo:
        nk, kc = wo_s.shape[:2]

        @pl.when(step < nk)
        def _():
            wo_s[step] = wo_ref[...].astype(BF16)

    @pl.when(step >= nj)
    def _():
        if has_wo:
            yb = y_ref[...].astype(BF16)
            y = _dot(yb[:, :kc], wo_s[0])
            for c in range(1, nk):
                y = y + _dot(yb[:, c * kc:(c + 1) * kc], wo_s[c])
        else:
            y = y_ref[...].astype(F32)
        x1 = x_ref[...] + ga_ref[...] * y
        h = (_rms(x1, g2_ref[...]) * (1.0 + scm_ref[...]) + shm_ref[...]).astype(BF16)
        acc = jnp.zeros_like(x1)
        for c in range(nj):
            t = jnp.maximum(_dot(h, w1_s[c]), 0.0)
            acc = acc + _dot((t * t).astype(BF16), w2_s[c])
        o_ref[...] = x1 + gm_ref[...] * acc


def _mlp(x, y, mod, lseg, g2, w1_all, w2_all, layer, wo=None):
    n, d = x.shape
    dy = y.shape[1]
    dff = w1_all.shape[2]
    tm = _pick_tile(lseg)
    tf = min(dff, 512)
    nj = dff // tf
    tile = lambda s: jnp.maximum(s - nj, 0)
    wblk = lambda s: jnp.minimum(s, nj - 1)
    row = lambda width: pl.BlockSpec((tm, width), lambda s: (tile(s), 0))
    in_specs = [
        row(d), row(dy),
        _mod_spec(2, tm, lseg, d, tile),
        _mod_spec(3, tm, lseg, d, tile),
        _mod_spec(4, tm, lseg, d, tile),
        _mod_spec(5, tm, lseg, d, tile),
        pl.BlockSpec((1, d), lambda s: (0, 0)),
        pl.BlockSpec((None, d, tf), lambda s: (layer, 0, wblk(s))),
        pl.BlockSpec((None, tf, d), lambda s: (layer, wblk(s), 0)),
    ]
    scratch = [pltpu.VMEM((nj, d, tf), BF16), pltpu.VMEM((nj, tf, d), BF16)]
    args = [x, y, mod, mod, mod, mod, g2, w1_all, w2_all]
    if wo is not None:
        kc = min(dy, MXU_N)
        nk = dy // kc
        assert nk <= nj
        in_specs.append(pl.BlockSpec((kc, d), lambda s: (jnp.minimum(s, nk - 1), 0)))
        scratch.append(pltpu.VMEM((nk, kc, d), BF16))
        args.append(wo)
    return pl.pallas_call(
        functools.partial(_mlp_kernel, has_wo=wo is not None),
        grid=(nj + n // tm,),
        in_specs=in_specs,
        out_specs=row(d),
        out_shape=jax.ShapeDtypeStruct((n, d), F32),
        scratch_shapes=scratch,
        compiler_params=_cparams(("arbitrary",)),
        name="mlp",
    )(*args)


def _rope(x, cos, sin_signed):
    width = x.shape[-1]
    quarter = MLA_ROPE // 4
    lane = lax.broadcasted_iota(jnp.int32, x.shape, 1)
    even_quarter = ((lane // quarter) % 2) == 0
    rot = jnp.where(even_quarter, pltpu.roll(x, width - quarter, 1), pltpu.roll(x, quarter, 1))
    return x * cos + rot * sin_signed


def _mla_proj_kernel(x_ref, sh_ref, sc_ref, g1_ref, cos_ref, sin_ref,
                     wd_ref, gq_ref, wuq_ref, gkv_ref, wukv_ref,
                     gqn_ref, gqr_ref, gkn_ref, gkr_ref, seg_ref,
                     q_ref, k_ref, v_ref):
    nh = q_ref.shape[0]
    dn = nh * MLA_NOPE
    dr = nh * MLA_ROPE
    kvl = gkv_ref.shape[-1]
    h = (_rms(x_ref[...], g1_ref[...]) * (1.0 + sc_ref[...]) + sh_ref[...]).astype(BF16)
    qlora = gq_ref.shape[-1]
    down = _dot(h, wd_ref[...])
    ql = _rms(down[:, :qlora], gq_ref[...]).astype(BF16)
    q = _dot(ql, wuq_ref[...])
    kv = down[:, qlora:]
    ckv = _rms(kv[:, :kvl], gkv_ref[...]).astype(BF16)
    kvu = _dot(ckv, wukv_ref[...])

    cos = cos_ref[...]
    sin = sin_ref[...]
    reps = dr // LANES
    cos_q = jnp.concatenate([cos] * reps, axis=1)
    sin_q = jnp.concatenate([sin] * reps, axis=1)

    qr = q[:, dn:]
    sq = qr * qr
    sq_hi = sq.astype(BF16)
    sq_lo = (sq - sq_hi.astype(F32)).astype(BF16)
    ms = _dot(sq_hi, seg_ref[...]) + _dot(sq_lo, seg_ref[...])
    qr = qr * lax.rsqrt(ms + EPS) * gqr_ref[...]
    qr = _rope(qr, cos_q, sin_q) * MLA_SCALE

    kr = kv[:, kvl:]
    kr = _rms(kr, gkr_ref[...])
    kr = _rope(kr, cos, sin)[:, :MLA_ROPE]

    for hh in range(nh):
        qn = _rms(q[:, hh * MLA_NOPE:(hh + 1) * MLA_NOPE], gqn_ref[...]) * MLA_SCALE
        qrh = qr[:, hh * MLA_ROPE:(hh + 1) * MLA_ROPE]
        q_ref[hh] = jnp.concatenate([qn, qrh], axis=1).astype(q_ref.dtype)
        kn = _rms(kvu[:, hh * MLA_NOPE:(hh + 1) * MLA_NOPE], gkn_ref[...])
        k_ref[hh] = jnp.concatenate([kn, kr], axis=1).astype(k_ref.dtype)
        v_ref[hh, :, :MLA_V] = kvu[:, dn + hh * MLA_V:dn + (hh + 1) * MLA_V].astype(v_ref.dtype)
        v_ref[hh, :, MLA_V:] = jnp.ones((kvu.shape[0], MLA_V), v_ref.dtype)


def _mla_proj(x, mod, lseg, g1, cos_t, sin_t, w):
    n, d = x.shape
    tm = _pick_tile(lseg, cap=512)
    nrope = cos_t.shape[0] // tm
    full = lambda a: pl.BlockSpec(a.shape, lambda i: (0,) * a.ndim)
    weights = [w["wd"], w["gq"], w["wuq"], w["gkv"], w["wukv"],
               w["gqn"], w["gqr"], w["gkn"], w["gkr"], w["seg"]]
    return pl.pallas_call(
        _mla_proj_kernel,
        grid=(n // tm,),
        in_specs=[
            pl.BlockSpec((tm, d), lambda i: (i, 0)),
            _mod_spec(0, tm, lseg, d),
            _mod_spec(1, tm, lseg, d),
            pl.BlockSpec((1, d), lambda i: (0, 0)),
            pl.BlockSpec((tm, LANES), lambda i: (i % nrope, 0)),
            pl.BlockSpec((tm, LANES), lambda i: (i % nrope, 0)),
        ] + [full(a) for a in weights],
        out_specs=[
            pl.BlockSpec((MLA_HEADS, tm, MLA_QK), lambda i: (0, i, 0)),
            pl.BlockSpec((MLA_HEADS, tm, MLA_QK), lambda i: (0, i, 0)),
            pl.BlockSpec((MLA_HEADS, tm, 2 * MLA_V), lambda i: (0, i, 0)),
        ],
        out_shape=[
            jax.ShapeDtypeStruct((MLA_HEADS, n, MLA_QK), BF16),
            jax.ShapeDtypeStruct((MLA_HEADS, n, MLA_QK), BF16),
            jax.ShapeDtypeStruct((MLA_HEADS, n, 2 * MLA_V), BF16),
        ],
        compiler_params=_cparams(("parallel",)),
        name="mla_proj",
    )(x, mod, mod, g1, cos_t, sin_t, *weights)


def _attn_lat_kernel(q_ref, kc_ref, kl_ref, vc_ref, vl_ref, o_ref, sa_s, sb_s, ma_s, mb_s, *, tq, kb):
    heads, lq = q_ref.shape[:2]
    lc = kc_ref.shape[1]
    n_tiles = lq // tq
    blocks = [(kc_ref, vc_ref, 0, lc, 0)] + [(kl_ref, vl_ref, j * kb, kb, lc + j * kb)
                                             for j in range(kl_ref.shape[1] // kb)]
    slots = ((sa_s, ma_s), (sb_s, mb_s))

    def iteration(first, second, slot):
        s_w, m_w = slots[slot]
        s_r, m_r = slots[1 - slot]
        if first is not None:
            q = q_ref[first[0], pl.ds(pl.multiple_of(first[1] * tq, tq), tq), :]
            mp = jnp.full((tq, LANES), -jnp.inf, F32)
        if second is not None:
            m = m_r[...]
            acc = jnp.zeros((tq, 2 * MLA_V), F32)
        for k_ref, v_ref, start, size, off in blocks:
            if first is not None:
                s = _dot_nt(q, k_ref[first[0], start:start + size, :])
                s_w[:, off:off + size] = s
                for c in range(size // LANES):
                    mp = jnp.maximum(mp, s[:, c * LANES:(c + 1) * LANES])
            if second is not None:
                p = jnp.concatenate(
                    [jnp.exp2(s_r[:, off + c * LANES:off + (c + 1) * LANES] - m)
                     for c in range(size // LANES)], axis=1).astype(BF16)
                acc = acc + _dot(p, v_ref[second[0], start:start + size, :])
        if first is not None:
            m_w[...] = jnp.broadcast_to(jnp.max(mp, axis=-1, keepdims=True), (tq, LANES))
        if second is not None:
            rows = pl.ds(pl.multiple_of(second[1] * tq, tq), tq)
            o_ref[rows, second[0] * MLA_V:(second[0] + 1) * MLA_V] = (
                acc[:, :MLA_V] / acc[:, MLA_V:]).astype(o_ref.dtype)

    def in_head(hd, lo, hi):
        count = hi - lo
        par = (hd * n_tiles + lo) % 2

        def pair(j, carry):
            i = lo + 2 * j
            iteration((hd, i), (hd, i - 1), par)
            iteration((hd, i + 1), (hd, i), 1 - par)
            return carry

        lax.fori_loop(0, count // 2, pair, 0)
        if count % 2:
            iteration((hd, hi - 1), (hd, hi - 2), (hd * n_tiles + hi - 1) % 2)

    iteration((0, 0), None, 0)
    for hd in range(heads):
        if hd > 0:
            iteration((hd, 0), (hd - 1, n_tiles - 1), (hd * n_tiles) % 2)
        in_head(hd, 1, n_tiles)
    iteration(None, (heads - 1, n_tiles - 1), (heads * n_tiles) % 2)


def _attn_ctx_kernel(q_ref, kc_ref, vc_ref, o_ref):
    sc = _dot_nt(q_ref[...], kc_ref[...])
    m = jnp.max(sc, axis=-1, keepdims=True)
    o = _dot(jnp.exp2(sc - m).astype(BF16), vc_ref[...])
    o_ref[...] = (o[:, :MLA_V] / o[:, MLA_V:]).astype(o_ref.dtype)


def _attn_lat(q, kl, vl, kc, vc, nb):
    nh, n, _ = q.shape
    l = n // nb
    lc = kc.shape[1] // nb
    tq = _pick_tile(l, cap=256)
    kb = _pick_tile(l, cap=MXU_N)
    hp = 2 if nh % 2 == 0 else 1
    assert lc % LANES == 0
    return pl.pallas_call(
        functools.partial(_attn_lat_kernel, tq=tq, kb=kb),
        grid=(nb, nh // hp),
        in_specs=[
            pl.BlockSpec((hp, l, MLA_QK), lambda b, h: (h, b, 0)),
            pl.BlockSpec((hp, lc, MLA_QK), lambda b, h: (h, b, 0)),
            pl.BlockSpec((hp, l, MLA_QK), lambda b, h: (h, b, 0)),
            pl.BlockSpec((hp, lc, 2 * MLA_V), lambda b, h: (h, b, 0)),
            pl.BlockSpec((hp, l, 2 * MLA_V), lambda b, h: (h, b, 0)),
        ],
        out_specs=pl.BlockSpec((l, hp * MLA_V), lambda b, h: (b, h)),
        out_shape=jax.ShapeDtypeStruct((n, nh * MLA_V), BF16),
        scratch_shapes=[pltpu.VMEM((tq, lc + l), F32), pltpu.VMEM((tq, lc + l), F32),
                        pltpu.VMEM((tq, LANES), F32), pltpu.VMEM((tq, LANES), F32)],
        compiler_params=_cparams(("parallel", "parallel")),
        name="attn_lat",
    )(q, kc, kl, vc, vl)


def _attn_ctx(q, kc, vc, nb):
    nh, n, _ = q.shape
    lc = n // nb
    return pl.pallas_call(
        _attn_ctx_kernel,
        grid=(nb, nh),
        in_specs=[
            pl.BlockSpec((None, lc, MLA_QK), lambda b, h: (h, b, 0)),
            pl.BlockSpec((None, lc, MLA_QK), lambda b, h: (h, b, 0)),
            pl.BlockSpec((None, lc, 2 * MLA_V), lambda b, h: (h, b, 0)),
        ],
        out_specs=pl.BlockSpec((lc, MLA_V), lambda b, h: (b, h)),
        out_shape=jax.ShapeDtypeStruct((n, nh * MLA_V), BF16),
        compiler_params=_cparams(("parallel", "parallel")),
        name="attn_ctx",
    )(q, kc, vc)


def _rope_tables(n_tok):
    rows = n_tok // GRID_W
    row = np.repeat(np.arange(rows, dtype=np.float32), GRID_W)
    col = np.tile(np.arange(GRID_W, dtype=np.float32), rows)
    n_freq = MLA_ROPE // 4
    freqs = (np.float32(ROPE_THETA) ** (-np.arange(n_freq, dtype=np.float32) / np.float32(n_freq))).astype(np.float32)
    ang_r = row[:, None] * freqs[None, :]
    ang_c = col[:, None] * freqs[None, :]
    ang = np.concatenate([ang_r, ang_r, ang_c, ang_c], axis=-1).astype(np.float32)
    sign = np.tile(np.repeat(np.array([-1.0, 1.0], np.float32), n_freq), 2)
    cos = np.cos(ang).astype(np.float32)
    sin = (np.sin(ang) * sign).astype(np.float32)
    return jnp.asarray(np.concatenate([cos, cos], axis=1)), jnp.asarray(np.concatenate([sin, sin], axis=1))


def _mla_weights(w_dq, g_q, w_uq, w_dkv, g_kv, w_ukv, g_qk, w_o):
    nh = MLA_HEADS
    qlora = w_dq.shape[1]
    kvl = g_kv.shape[0]
    uq = w_uq.reshape(qlora, nh, MLA_QK)
    wuq = jnp.concatenate([uq[:, :, :MLA_NOPE].reshape(qlora, nh * MLA_NOPE),
                           uq[:, :, MLA_NOPE:].reshape(qlora, nh * MLA_ROPE)], axis=1)
    ukv = w_ukv.reshape(kvl, nh, MLA_NOPE + MLA_V)
    wukv = jnp.concatenate([ukv[:, :, :MLA_NOPE].reshape(kvl, nh * MLA_NOPE),
                            ukv[:, :, MLA_NOPE:].reshape(kvl, nh * MLA_V)], axis=1)
    wdkv = jnp.concatenate([w_dkv, w_dkv[:, kvl:]], axis=1)
    dr = nh * MLA_ROPE
    seg_id = jnp.arange(dr) // MLA_ROPE
    seg = (seg_id[:, None] == seg_id[None, :]).astype(F32) / MLA_ROPE
    return {
        "wd": jnp.concatenate([w_dq, wdkv], axis=1).astype(BF16), "gq": g_q[None, :], "wuq": wuq.astype(BF16),
        "gkv": g_kv[None, :], "wukv": wukv.astype(BF16),
        "gqn": g_qk[0:1, :MLA_NOPE], "gqr": jnp.tile(g_qk[0:1, MLA_NOPE:], (1, nh)),
        "gkn": g_qk[1:2, :MLA_NOPE], "gkr": jnp.tile(g_qk[1:2, MLA_NOPE:], (1, 2)),
        "seg": seg.astype(BF16),
    }


def _mla_layer(lat, cx, mod_l, mod_c, nb, g1, wts, need_ctx):
    l = lat.shape[0] // nb
    lc = cx.shape[0] // nb
    w = _mla_weights(*wts)
    cos_t, sin_t = _rope_tables(l)
    tmc = _pick_tile(lc, cap=512)
    ql, kl, vl = _mla_proj(lat, mod_l, l, g1, cos_t, sin_t, w)
    qc, kc, vc = _mla_proj(cx, mod_c, lc, g1, jnp.ones((tmc, LANES), F32), jnp.zeros((tmc, LANES), F32), w)
    y = _attn_lat(ql, kl, vl, kc, vc, nb)
    yc = _attn_ctx(qc, kc, vc, nb) if need_ctx else None
    return y, yc


def _s5_kernel(xf_ref, xr_ref, sh_ref, sc_ref, g1_ref, dd_ref, wb_ref, wc_ref, are_ref, aim_ref, st0_ref,
               yf_ref, yr_ref, st_ref, in_s, out_s, *, pitch):
    nb, t, d = xf_ref.shape
    nt = st_ref.shape[2]
    nre = nt // 2
    nv = nre // SUBLANES
    nj = wb_ref.shape[1]
    kblocks = d // LANES
    per_kb = (nj // 2) // kblocks
    nq = wc_ref.shape[1]
    tiles_q = nre // nq
    step = pl.program_id(0)

    @pl.when(step == 0)
    def _():
        st_ref[...] = st0_ref[...]

    def prep(x_ref):
        return _rms(x_ref[...], g1_ref[...]) * (1.0 + sc_ref[...]) + sh_ref[...]

    hs = (prep(xf_ref), prep(xr_ref))
    skip = hs[0] * dd_ref[...]
    outs = (yf_ref, yr_ref)
    bpl = 2 if nb % 2 == 0 else 1

    def slab(b, n):
        return (b * nt + n) * pitch

    for dr in range(2):
        hb = hs[dr].reshape(nb * t, d).astype(BF16)
        for j in range(nj):
            kb = (j % (nj // 2)) // per_kb
            r = _dot(hb[:, kb * LANES:(kb + 1) * LANES], wb_ref[dr, j])
            for b in range(nb):
                for half in range(MXU_N // LANES):
                    n = j * (MXU_N // LANES) + half
                    in_s[pl.ds(slab(b, n), t), :] = r[b * t:(b + 1) * t, half * LANES:(half + 1) * LANES]

        consts = [(are_ref[dr, v * SUBLANES:(v + 1) * SUBLANES, :], aim_ref[dr, v * SUBLANES:(v + 1) * SUBLANES, :])
                  for v in range(nv)]
        for b0 in range(0, nb, bpl):
            init = tuple(st_ref[b0 + bb, dr, part * nre + v * SUBLANES:part * nre + (v + 1) * SUBLANES, :]
                         for bb in range(bpl) for v in range(nv) for part in range(2))

            def body(i, carry, b0=b0, dr=dr, consts=consts):
                tt, carry = carry[0], carry[1:]
                out = [tt + (1 if dr == 0 else -1)]
                for bb in range(bpl):
                    for v in range(nv):
                        sre, sim = carry[(bb * nv + v) * 2], carry[(bb * nv + v) * 2 + 1]
                        are, aim = consts[v]
                        row_re = slab(b0 + bb, v * SUBLANES) + tt
                        row_im = slab(b0 + bb, nre + v * SUBLANES) + tt
                        bre = in_s[pl.ds(row_re, SUBLANES, stride=pitch), :]
                        bim = in_s[pl.ds(row_im, SUBLANES, stride=pitch), :]
                        new_re = are * sre - aim * sim + bre
                        new_im = are * sim + aim * sre + bim
                        out_s[pl.ds(row_re, SUBLANES, stride=pitch), :] = new_re
                        out_s[pl.ds(row_im, SUBLANES, stride=pitch), :] = new_im
                        out += [new_re, new_im]
                return tuple(out)

            t_first = jnp.int32(0 if dr == 0 else t - 1)
            fin = lax.fori_loop(0, t, body, (t_first,) + init, unroll=2)[1:]
            k = 0
            for bb in range(bpl):
                for v in range(nv):
                    for part in range(2):
                        st_ref[b0 + bb, dr, part * nre + v * SUBLANES:part * nre + (v + 1) * SUBLANES, :] = fin[k]
                        k += 1

        for q in range(nq):
            rows = []
            for b in range(nb):
                pieces = [out_s[pl.ds(slab(b, part * nre + q * tiles_q + i), t), :]
                          for part in range(2) for i in range(tiles_q)]
                rows.append(jnp.concatenate(pieces, axis=1))
            yq = _dot(jnp.concatenate(rows, axis=0).astype(BF16), wc_ref[dr, q])
            for b in range(nb):
                val = yq[b * t:(b + 1) * t, :]
                if dr == 0:
                    val = val + skip[b, :, q * LANES:(q + 1) * LANES]
                outs[dr][b, :, q * LANES:(q + 1) * LANES] = val


def _s5_scan(x3, mod3, g1, dd, wb, wc, are, aim, st0, t):
    nb, l, d = x3.shape
    nc = l // t
    pitch = t + SUBLANES // 2
    nt = st0.shape[2]
    full = lambda a: pl.BlockSpec(a.shape, lambda c: (0,) * a.ndim)
    buf = pltpu.VMEM((nb * nt * pitch, LANES), F32)
    return pl.pallas_call(
        functools.partial(_s5_kernel, pitch=pitch),
        grid=(nc,),
        in_specs=[
            pl.BlockSpec((nb, t, d), lambda c: (0, c, 0)),
            pl.BlockSpec((nb, t, d), lambda c: (0, nc - 1 - c, 0)),
            pl.BlockSpec((nb, 1, d), lambda c: (0, 0, 0)),
            pl.BlockSpec((nb, 1, d), lambda c: (0, 0, 1)),
            full(g1), full(dd), full(wb), full(wc), full(are), full(aim), full(st0),
        ],
        out_specs=[
            pl.BlockSpec((nb, t, d), lambda c: (0, c, 0)),
            pl.BlockSpec((nb, t, d), lambda c: (0, nc - 1 - c, 0)),
            full(st0),
        ],
        out_shape=[
            jax.ShapeDtypeStruct((nb, l, d), F32),
            jax.ShapeDtypeStruct((nb, l, d), F32),
            jax.ShapeDtypeStruct(st0.shape, F32),
        ],
        scratch_shapes=[buf, buf],
        compiler_params=_cparams(("arbitrary",)),
        name="s5_scan",
    )(x3, x3, mod3, mod3, g1, dd, wb, wc, are, aim, st0)


def _s5_glu_kernel(yf_ref, yr_ref, w_ref, o_ref):
    d = o_ref.shape[-1]
    z = _dot(_gelu(yf_ref[...] + yr_ref[...]).astype(BF16), w_ref[...])
    o_ref[...] = z[:, :d] * _sigmoid_tanh(z[:, d:])


def _s5_glu(yf, yr, w, lseg):
    n, d = yf.shape
    tm = _pick_tile(lseg)
    return pl.pallas_call(
        _s5_glu_kernel,
        grid=(n // tm,),
        in_specs=[pl.BlockSpec((tm, d), lambda i: (i, 0)), pl.BlockSpec((tm, d), lambda i: (i, 0)),
                  pl.BlockSpec(w.shape, lambda i: (0, 0))],
        out_specs=pl.BlockSpec((tm, d), lambda i: (i, 0)),
        out_shape=jax.ShapeDtypeStruct((n, d), F32),
        compiler_params=_cparams(("parallel",)),
        name="s5_glu",
    )(yf, yr, w)


def _s5_weights(a_re, a_im, log_dt, b_re, b_im, c_re, c_im):
    ng, ns = a_re.shape[1:]
    gi = b_re.shape[-1]
    d = ng * gi
    lr, li = a_re.astype(F32), a_im.astype(F32)
    dt = jnp.exp(log_dt.astype(F32))[..., None]
    mag = jnp.exp(lr * dt)
    ar, ai = mag * jnp.cos(li * dt), mag * jnp.sin(li * dt)
    den = lr * lr + li * li
    wr = ((ar - 1.0) * lr + ai * li) / den
    wi = (ai * lr - (ar - 1.0) * li) / den
    br, bi = b_re.astype(F32), b_im.astype(F32)
    bb_re = wr[..., None] * br - wi[..., None] * bi
    bb_im = wr[..., None] * bi + wi[..., None] * br

    gpt = MXU_N // ns
    gpk = LANES // gi
    ncol = ng // gpt
    sel = (jnp.arange(gpk)[None, :, None]
           == gpt * (jnp.arange(ncol)[:, None, None] % (gpk // gpt)) + jnp.arange(gpt)[None, None, :]).astype(F32)

    def in_tiles(part):
        blk = part.reshape(2, ncol, gpt, ns, gi).transpose(0, 1, 4, 2, 3)
        tiles = blk[:, :, None, :, :, :] * sel[None, :, :, None, :, None]
        return tiles.reshape(2, ncol, gpk * gi, gpt * ns)

    wb = jnp.concatenate([in_tiles(bb_re), in_tiles(bb_im)], axis=1).astype(BF16)

    nq = d // LANES
    gpq = ng // nq
    eye = jnp.eye(gpq, dtype=F32)

    def out_tiles(part):
        blk = jnp.swapaxes(part.astype(F32), -1, -2).reshape(2, nq, gpq, ns, gi)
        tiles = blk[:, :, :, :, None, :] * eye[None, None, :, None, :, None]
        return tiles.reshape(2, nq, gpq * ns, gpq * gi)

    wc = jnp.concatenate([out_tiles(c_re), out_tiles(-c_im)], axis=2).astype(BF16)
    are = ar.reshape(2, ng * ns // LANES, LANES)
    aim = ai.reshape(2, ng * ns // LANES, LANES)
    return wb, wc, are, aim


def _s5_layer(lat, cx, mod_l, mod_c, nb, g1, a_re, a_im, log_dt, b_re, b_im, c_re, c_im, dvec, w_glu, need_ctx):
    d = lat.shape[1]
    l = lat.shape[0] // nb
    lc = cx.shape[0] // nb
    wb, wc, are, aim = _s5_weights(a_re, a_im, log_dt, b_re, b_im, c_re, c_im)
    t = _pick_tile(l, lc, cap=64)
    st0 = jnp.zeros((nb, 2, 2 * are.shape[1], LANES), F32)
    dd = dvec[None, :].astype(F32)
    ycf, ycr, st = _s5_scan(cx.reshape(nb, lc, d), mod_c, g1, dd, wb, wc, are, aim, st0, t)
    yf, yr, _ = _s5_scan(lat.reshape(nb, l, d), mod_l, g1, dd, wb, wc, are, aim, st, t)
    wg = w_glu.astype(BF16)
    y = _s5_glu(yf.reshape(nb * l, d), yr.reshape(nb * l, d), wg, l)
    yc = _s5_glu(ycf.reshape(nb * lc, d), ycr.reshape(nb * lc, d), wg, lc) if need_ctx else None
    return y, yc


def _lru_in_kernel(x_ref, sh_ref, sc_ref, g1_ref, w_ref, xw_ref, gl_ref):
    wd = xw_ref.shape[-1]
    h = (_rms(x_ref[...], g1_ref[...]) * (1.0 + sc_ref[...]) + sh_ref[...]).astype(BF16)
    z = _dot(h, w_ref[...])
    gl_ref[...] = _gelu(z[:, :wd]).astype(gl_ref.dtype)
    xw_ref[...] = z[:, wd:]


def _lru_in(x, mod, lseg, g1, w_in):
    n, d = x.shape
    wd = w_in.shape[1] // 2
    tm = _pick_tile(lseg)
    return pl.pallas_call(
        _lru_in_kernel,
        grid=(n // tm,),
        in_specs=[pl.BlockSpec((tm, d), lambda i: (i, 0)), _mod_spec(0, tm, lseg, d), _mod_spec(1, tm, lseg, d),
                  pl.BlockSpec((1, d), lambda i: (0, 0)), pl.BlockSpec(w_in.shape, lambda i: (0, 0))],
        out_specs=[pl.BlockSpec((tm, wd), lambda i: (i, 0)), pl.BlockSpec((tm, wd), lambda i: (i, 0))],
        out_shape=[jax.ShapeDtypeStruct((n, wd), F32), jax.ShapeDtypeStruct((n, wd), BF16)],
        compiler_params=_cparams(("parallel",)),
        name="lru_in",
    )(x, mod, mod, g1, w_in)


def _lru_scan_kernel(xf_ref, xfp_ref, xfn_ref, xr_ref, xrp_ref, xrn_ref,
                     cw_ref, cb_ref, wg_ref, bg_ref, sp_ref, st0_ref,
                     hf_ref, hr_ref, st_ref, ext_s, a_s, b_s, h_s, *, pitch):
    nb, t, wd = xf_ref.shape
    nblk = wd // LANES
    halo = SUBLANES
    step = pl.program_id(0)
    last = pl.num_programs(0) - 1

    @pl.when(step == 0)
    def _():
        st_ref[...] = st0_ref[...]

    def coeffs(dr, x_ref, prev_ref, next_ref, at_start, at_end):
        ext_s[:, 0:halo, :] = jnp.where(at_start, 0.0, prev_ref[...])
        ext_s[:, halo:halo + t, :] = x_ref[...]
        ext_s[:, halo + t:2 * halo + t, :] = jnp.where(at_end, 0.0, next_ref[...])
        xr = cb_ref[...]
        for tap in range(cw_ref.shape[0]):
            xr = xr + cw_ref[tap:tap + 1, :] * ext_s[:, halo - 1 + tap:halo - 1 + tap + t, :]
        for n in range(nblk):
            xb = xr[:, :, n * LANES:(n + 1) * LANES].reshape(nb * t, LANES)
            g = _dot(xb.astype(BF16), wg_ref[dr, n]) + bg_ref[dr, n]
            c1 = sp_ref[dr, n]
            a = jnp.exp2(c1 * jnp.tanh(g[:, :LANES]) + c1)
            bc = jnp.sqrt(1.0 - a * a) * ((0.5 * xb) * (jnp.tanh(g[:, LANES:]) + 1.0))
            for b in range(nb):
                a_s[dr, n, pl.ds(b * pitch, t), :] = a[b * t:(b + 1) * t]
                b_s[dr, n, pl.ds(b * pitch, t), :] = bc[b * t:(b + 1) * t]

    coeffs(0, xf_ref, xfp_ref, xfn_ref, step == 0, step == last)
    coeffs(1, xr_ref, xrp_ref, xrn_ref, step == last, step == 0)

    init = tuple(st_ref[dr, n] for dr in range(2) for n in range(nblk))

    def body(i, carry):
        times, carry = carry[:2], carry[2:]
        out = [times[0] + 1, times[1] - 1]
        for dr in range(2):
            tt = times[dr]
            for n in range(nblk):
                a = a_s[dr, n, pl.ds(tt, nb, stride=pitch), :]
                bc = b_s[dr, n, pl.ds(tt, nb, stride=pitch), :]
                hnew = a * carry[dr * nblk + n] + bc
                h_s[dr, n, pl.ds(tt, nb, stride=pitch), :] = hnew
                out.append(hnew)
        return tuple(out)

    fin = lax.fori_loop(0, t, body, (jnp.int32(0), jnp.int32(t - 1)) + init, unroll=2)[2:]
    k = 0
    for dr in range(2):
        for n in range(nblk):
            st_ref[dr, n] = fin[k]
            k += 1
    outs = (hf_ref, hr_ref)
    for dr in range(2):
        for n in range(nblk):
            for b in range(nb):
                outs[dr][b, :, n * LANES:(n + 1) * LANES] = h_s[dr, n, pl.ds(b * pitch, t), :]


def _lru_scan(xw3, cw, cb, wg, bg, sp, st0, t):
    nb, l, wd = xw3.shape
    nc = l // t
    nblk = wd // LANES
    hb = t // SUBLANES
    nhb = l // SUBLANES
    pitch = t + SUBLANES
    full = lambda a: pl.BlockSpec(a.shape, lambda c: (0,) * a.ndim)
    chunk = lambda f: pl.BlockSpec((nb, t, wd), lambda c: (0, f(c), 0))
    halo_prev = lambda f: pl.BlockSpec((nb, SUBLANES, wd), lambda c: (0, jnp.maximum(f(c) * hb - 1, 0), 0))
    halo_next = lambda f: pl.BlockSpec((nb, SUBLANES, wd), lambda c: (0, jnp.minimum((f(c) + 1) * hb, nhb - 1), 0))
    fwd = lambda c: c
    rev = lambda c: nc - 1 - c
    coef = pltpu.VMEM((2, nblk, nb * pitch, LANES), F32)
    return pl.pallas_call(
        functools.partial(_lru_scan_kernel, pitch=pitch),
        grid=(nc,),
        in_specs=[chunk(fwd), halo_prev(fwd), halo_next(fwd), chunk(rev), halo_prev(rev), halo_next(rev),
                  full(cw), full(cb), full(wg), full(bg), full(sp), full(st0)],
        out_specs=[chunk(fwd), chunk(rev), full(st0)],
        out_shape=[jax.ShapeDtypeStruct((nb, l, wd), F32), jax.ShapeDtypeStruct((nb, l, wd), F32),
                   jax.ShapeDtypeStruct(st0.shape, F32)],
        scratch_shapes=[pltpu.VMEM((nb, t + 2 * SUBLANES, wd), F32), coef, coef, coef],
        compiler_params=_cparams(("arbitrary",)),
        name="lru_scan",
    )(xw3, xw3, xw3, xw3, xw3, xw3, cw, cb, wg, bg, sp, st0)


def _lru_out_kernel(gl_ref, hf_ref, hr_ref, w_ref, o_ref):
    u = gl_ref[...].astype(F32) * (hf_ref[...] + hr_ref[...])
    o_ref[...] = _dot(u.astype(BF16), w_ref[...])


def _lru_out(gl, hf, hr, w, lseg):
    n, wd = gl.shape
    d = w.shape[1]
    tm = _pick_tile(lseg)
    row = lambda: pl.BlockSpec((tm, wd), lambda i: (i, 0))
    return pl.pallas_call(
        _lru_out_kernel,
        grid=(n // tm,),
        in_specs=[row(), row(), row(), pl.BlockSpec(w.shape, lambda i: (0, 0))],
        out_specs=pl.BlockSpec((tm, d), lambda i: (i, 0)),
        out_shape=jax.ShapeDtypeStruct((n, d), F32),
        compiler_params=_cparams(("parallel",)),
        name="lru_out",
    )(gl, hf, hr, w)


def _lru_layer(lat, cx, mod_l, mod_c, nb, g1, w_in, conv_w, conv_b, w_gate, b_gate, lam, w_out, need_ctx):
    l = lat.shape[0] // nb
    lc = cx.shape[0] // nb
    wd = w_in.shape[1] // 2
    nblk = wd // LANES
    w_in_b = w_in.astype(BF16)
    xw, gl = _lru_in(lat, mod_l, l, g1, w_in_b)
    xwc, glc = _lru_in(cx, mod_c, lc, g1, w_in_b)
    wg = (0.5 * jnp.concatenate([w_gate[:, 0], w_gate[:, 1]], axis=-1)).astype(BF16)
    bgate = b_gate.astype(F32).reshape(2, 2, nblk, 1, LANES)
    bg = 0.5 * jnp.concatenate([bgate[:, 0], bgate[:, 1]], axis=-1)
    sp = (-0.5 * LRU_C * math.log2(math.e)) * jax.nn.softplus(-lam.astype(F32)).reshape(2, nblk, 1, LANES)
    cw = conv_w.astype(F32)
    cb = conv_b.astype(F32)[None, :]
    t = _pick_tile(l, lc, cap=64)
    st0 = jnp.zeros((2, nblk, nb, LANES), F32)
    hcf, hcr, st = _lru_scan(xwc.reshape(nb, lc, wd), cw, cb, wg, bg, sp, st0, t)
    hf, hr, _ = _lru_scan(xw.reshape(nb, l, wd), cw, cb, wg, bg, sp, st, t)
    wo = w_out.astype(BF16)
    y = _lru_out(gl, hf.reshape(nb * l, wd), hr.reshape(nb * l, wd), wo, l)
    yc = _lru_out(glc, hcf.reshape(nb * lc, wd), hcr.reshape(nb * lc, wd), wo, lc) if need_ctx else None
    return y, yc


def kernel(x, c, ctx, c_ctx, ada_w, ada_b, norm_g, mla_w_dq, mla_g_q, mla_w_uq, mla_w_dkv, mla_g_kv, mla_w_ukv, mla_g_qk, mla_w_o, s5_a_re, s5_a_im, s5_log_dt, s5_b_re, s5_b_im, s5_c_re, s5_c_im, s5_d, s5_w_glu, lru_w_in, lru_conv_w, lru_conv_b, lru_w_gate, lru_b_gate, lru_lambda, lru_w_out, mlp_w1, mlp_w2):
    nb, l, d = x.shape
    lc = ctx.shape[1]
    depth = ada_w.shape[0]
    assert nb < SUBLANES
    cc = jnp.concatenate([c, c_ctx[None, :], jnp.zeros((SUBLANES - nb - 1, d), F32)], axis=0)
    mod = _ada_mod(cc, ada_w, ada_b)
    lat = x.reshape(nb * l, d)
    cx = ctx.reshape(nb * lc, d)
    for i in range(depth):
        need_ctx = i < depth - 1
        mod_l = mod[i, :nb, None, :]
        mod_c = jnp.broadcast_to(mod[i, nb][None, None, :], (nb, 1, mod.shape[-1]))
        g1 = norm_g[i, 0][None, :]
        g2 = norm_g[i, 1][None, :]
        kind, j = i % N_MIXERS, i // N_MIXERS
        if kind == 0:
            y, yc = _mla_layer(lat, cx, mod_l, mod_c, nb, g1,
                               (mla_w_dq[j], mla_g_q[j], mla_w_uq[j], mla_w_dkv[j], mla_g_kv[j], mla_w_ukv[j],
                                mla_g_qk[j], mla_w_o[j]), need_ctx)
        elif kind == 1:
            y, yc = _s5_layer(lat, cx, mod_l, mod_c, nb, g1, s5_a_re[j], s5_a_im[j], s5_log_dt[j], s5_b_re[j],
                              s5_b_im[j], s5_c_re[j], s5_c_im[j], s5_d[j], s5_w_glu[j], need_ctx)
        else:
            y, yc = _lru_layer(lat, cx, mod_l, mod_c, nb, g1, lru_w_in[j], lru_conv_w[j], lru_conv_b[j],
                               lru_w_gate[j], lru_b_gate[j], lru_lambda[j], lru_w_out[j], need_ctx)
        wo = mla_w_o[j] if kind == 0 else None
        lat = _mlp(lat, y, mod_l, l, g2, mlp_w1, mlp_w2, i, wo)
        if need_ctx:
            cx = _mlp(cx, yc, mod_c, lc, g2, mlp_w1, mlp_w2, i, wo)
    return lat.reshape(nb, l, d)
```

```python
import functools
import math

import jax
import jax.numpy as jnp
import numpy as np
from jax import lax
from jax.experimental import pallas as pl
from jax.experimental.pallas import tpu as pltpu

F32 = jnp.float32
BF16 = jnp.bfloat16

EPS = 1e-6
ROPE_THETA = 10000.0
GRID_W = 64
N_MIXERS = 3

MLA_HEADS = 8
MLA_NOPE = 128
MLA_ROPE = 64
MLA_V = 128
MLA_QK = MLA_NOPE + MLA_ROPE
MLA_SCALE = math.log2(math.e) / math.sqrt(MLA_QK)

S5_GROUP = 16
S5_STATE = 64
LRU_BLOCKS = 10
LRU_BW = 128
LRU_C = 8.0

LANES = 128
SUBLANES = 8
MXU_N = 256
VMEM_LIMIT = 56 * 1024 * 1024


def _cparams(sem):
    return pltpu.CompilerParams(dimension_semantics=sem, vmem_limit_bytes=VMEM_LIMIT)


def _dot(a, b):
    return jnp.dot(a, b, preferred_element_type=F32)


def _dot_nt(a, b):
    return lax.dot_general(a, b, (((1,), (1,)), ((), ())), preferred_element_type=F32)


def _rms(x, g):
    inv = lax.rsqrt(jnp.mean(x * x, axis=-1, keepdims=True) + EPS)
    return x * inv * g


def _gelu(x):
    c = math.sqrt(2.0 / math.pi)
    return x * (0.5 * (1.0 + jnp.tanh(c * (x + 0.044715 * (x * x * x)))))


def _sigmoid(x):
    return 1.0 / (1.0 + jnp.exp(-x))


def _sigmoid_tanh(x):
    return 0.5 * jnp.tanh(0.5 * x) + 0.5


def _pick_tile(*lens, cap=512):
    t = cap
    while any(n % t for n in lens):
        t //= 2
    assert t >= SUBLANES
    return t


def _ada_kernel(c_ref, w_ref, b_ref, o_ref):
    c = c_ref[...]
    s = c * _sigmoid(c)
    w = w_ref[...]
    s_hi = s.astype(BF16)
    s_lo = (s - s_hi.astype(F32)).astype(BF16)
    w_hi = w.astype(BF16)
    w_lo = (w - w_hi.astype(F32)).astype(BF16)
    o_ref[...] = _dot(s_hi, w_hi) + _dot(s_lo, w_hi) + _dot(s_hi, w_lo) + b_ref[...]


def _ada_mod(cc, ada_w, ada_b):
    depth, d, d6 = ada_w.shape
    nk = d6 // d
    return pl.pallas_call(
        _ada_kernel,
        grid=(depth, nk),
        in_specs=[
            pl.BlockSpec((SUBLANES, d), lambda l, k: (0, 0)),
            pl.BlockSpec((None, d, d), lambda l, k: (l, 0, k)),
            pl.BlockSpec((None, 1, d), lambda l, k: (l, 0, k)),
        ],
        out_specs=pl.BlockSpec((None, SUBLANES, d), lambda l, k: (l, 0, k)),
        out_shape=jax.ShapeDtypeStruct((depth, SUBLANES, d6), F32),
        compiler_params=_cparams(("parallel", "parallel")),
        name="ada_mod",
    )(cc, ada_w, ada_b.reshape(depth, 1, d6))


def _mod_spec(k, tm, lseg, d, tile_of=lambda i: i):
    return pl.BlockSpec((None, 1, d), lambda i, *_: ((tile_of(i) * tm) // lseg, 0, k))


def _mlp_kernel(*refs, has_wo):
    if has_wo:
        (x_ref, y_ref, ga_ref, shm_ref, scm_ref, gm_ref, g2_ref, w1_ref, w2_ref, wo_ref,
         o_ref, w1_s, w2_s, wo_s) = refs
    else:
        x_ref, y_ref, ga_ref, shm_ref, scm_ref, gm_ref, g2_ref, w1_ref, w2_ref, o_ref, w1_s, w2_s = refs
    step = pl.program_id(0)
    nj = w1_s.shape[0]

    @pl.when(step < nj)
    def _():
        w1_s[step] = w1_ref[...].astype(BF16)
        w2_s[step] = w2_ref[...].astype(BF16)

    if has_wo:
        nk, kc = wo_s.shape[:2]

        @pl.when(step < nk)
        def _():
            wo_s[step] = wo_ref[...].astype(BF16)

    @pl.when(step >= nj)
    def _():
        if has_wo:
            yb = y_ref[...].astype(BF16)
            y = _dot(yb[:, :kc], wo_s[0])
            for c in range(1, nk):
                y = y + _dot(yb[:, c * kc:(c + 1) * kc], wo_s[c])
        else:
            y = y_ref[...].astype(F32)
        x1 = x_ref[...] + ga_ref[...] * y
        h = (_rms(x1, g2_ref[...]) * (1.0 + scm_ref[...]) + shm_ref[...]).astype(BF16)
        acc = jnp.zeros_like(x1)
        for c in range(nj):
            t = jnp.maximum(_dot(h, w1_s[c]), 0.0)
            acc = acc + _dot((t * t).astype(BF16), w2_s[c])
        o_ref[...] = x1 + gm_ref[...] * acc


def _mlp(x, y, mod, lseg, g2, w1_all, w2_all, layer, wo=None):
    n, d = x.shape
    dy = y.shape[1]
    dff = w1_all.shape[2]
    tm = _pick_tile(lseg)
    tf = min(dff, 512)
    nj = dff // tf
    tile = lambda s: jnp.maximum(s - nj, 0)
    wblk = lambda s: jnp.minimum(s, nj - 1)
    row = lambda width: pl.BlockSpec((tm, width), lambda s: (tile(s), 0))
    in_specs = [
        row(d), row(dy),
        _mod_spec(2, tm, lseg, d, tile),
        _mod_spec(3, tm, lseg, d, tile),
        _mod_spec(4, tm, lseg, d, tile),
        _mod_spec(5, tm, lseg, d, tile),
        pl.BlockSpec((1, d), lambda s: (0, 0)),
        pl.BlockSpec((None, d, tf), lambda s: (layer, 0, wblk(s))),
        pl.BlockSpec((None, tf, d), lambda s: (layer, wblk(s), 0)),
    ]
    scratch = [pltpu.VMEM((nj, d, tf), BF16), pltpu.VMEM((nj, tf, d), BF16)]
    args = [x, y, mod, mod, mod, mod, g2, w1_all, w2_all]
    if wo is not None:
        kc = min(dy, MXU_N)
        nk = dy // kc
        assert nk <= nj
        in_specs.append(pl.BlockSpec((kc, d), lambda s: (jnp.minimum(s, nk - 1), 0)))
        scratch.append(pltpu.VMEM((nk, kc, d), BF16))
        args.append(wo)
    return pl.pallas_call(
        functools.partial(_mlp_kernel, has_wo=wo is not None),
        grid=(nj + n // tm,),
        in_specs=in_specs,
        out_specs=row(d),
        out_shape=jax.ShapeDtypeStruct((n, d), F32),
        scratch_shapes=scratch,
        compiler_params=_cparams(("arbitrary",)),
        name="mlp",
    )(*args)


def _rope(x, cos, sin_signed):
    width = x.shape[-1]
    quarter = MLA_ROPE // 4
    lane = lax.broadcasted_iota(jnp.int32, x.shape, 1)
    even_quarter = ((lane // quarter) % 2) == 0
    rot = jnp.where(even_quarter, pltpu.roll(x, width - quarter, 1), pltpu.roll(x, quarter, 1))
    return x * cos + rot * sin_signed


def _mla_proj_kernel(x_ref, sh_ref, sc_ref, g1_ref, cos_ref, sin_ref,
                     wd_ref, gq_ref, wuq_ref, gkv_ref, wukv_ref,
                     gqn_ref, gqr_ref, gkn_ref, gkr_ref, seg_ref,
                     q_ref, k_ref, v_ref):
    nh = q_ref.shape[0]
    dn = nh * MLA_NOPE
    dr = nh * MLA_ROPE
    kvl = gkv_ref.shape[-1]
    h = (_rms(x_ref[...], g1_ref[...]) * (1.0 + sc_ref[...]) + sh_ref[...]).astype(BF16)
    qlora = gq_ref.shape[-1]
    down = _dot(h, wd_ref[...])
    ql = _rms(down[:, :qlora], gq_ref[...]).astype(BF16)
    q = _dot(ql, wuq_ref[...])
    kv = down[:, qlora:]
    ckv = _rms(kv[:, :kvl], gkv_ref[...]).astype(BF16)
    kvu = _dot(ckv, wukv_ref[...])

    cos = cos_ref[...]
    sin = sin_ref[...]
    reps = dr // LANES
    cos_q = jnp.concatenate([cos] * reps, axis=1)
    sin_q = jnp.concatenate([sin] * reps, axis=1)

    qr = q[:, dn:]
    sq = qr * qr
    sq_hi = sq.astype(BF16)
    sq_lo = (sq - sq_hi.astype(F32)).astype(BF16)
    ms = _dot(sq_hi, seg_ref[...]) + _dot(sq_lo, seg_ref[...])
    qr = qr * lax.rsqrt(ms + EPS) * gqr_ref[...]
    qr = _rope(qr, cos_q, sin_q) * MLA_SCALE

    kr = kv[:, kvl:]
    kr = _rms(kr, gkr_ref[...])
    kr = _rope(kr, cos, sin)[:, :MLA_ROPE]

    for hh in range(nh):
        qn = _rms(q[:, hh * MLA_NOPE:(hh + 1) * MLA_NOPE], gqn_ref[...]) * MLA_SCALE
        qrh = qr[:, hh * MLA_ROPE:(hh + 1) * MLA_ROPE]
        q_ref[hh] = jnp.concatenate([qn, qrh], axis=1).astype(q_ref.dtype)
        kn = _rms(kvu[:, hh * MLA_NOPE:(hh + 1) * MLA_NOPE], gkn_ref[...])
        k_ref[hh] = jnp.concatenate([kn, kr], axis=1).astype(k_ref.dtype)
        v_ref[hh, :, :MLA_V] = kvu[:, dn + hh * MLA_V:dn + (hh + 1) * MLA_V].astype(v_ref.dtype)
        v_ref[hh, :, MLA_V:] = jnp.ones((kvu.shape[0], MLA_V), v_ref.dtype)


def _mla_proj(x, mod, lseg, g1, cos_t, sin_t, w):
    n, d = x.shape
    tm = _pick_tile(lseg, cap=512)
    nrope = cos_t.shape[0] // tm
    full = lambda a: pl.BlockSpec(a.shape, lambda i: (0,) * a.ndim)
    weights = [w["wd"], w["gq"], w["wuq"], w["gkv"], w["wukv"],
               w["gqn"], w["gqr"], w["gkn"], w["gkr"], w["seg"]]
    return pl.pallas_call(
        _mla_proj_kernel,
        grid=(n // tm,),
        in_specs=[
            pl.BlockSpec((tm, d), lambda i: (i, 0)),
            _mod_spec(0, tm, lseg, d),
            _mod_spec(1, tm, lseg, d),
            pl.BlockSpec((1, d), lambda i: (0, 0)),
            pl.BlockSpec((tm, LANES), lambda i: (i % nrope, 0)),
            pl.BlockSpec((tm, LANES), lambda i: (i % nrope, 0)),
        ] + [full(a) for a in weights],
        out_specs=[
            pl.BlockSpec((MLA_HEADS, tm, MLA_QK), lambda i: (0, i, 0)),
            pl.BlockSpec((MLA_HEADS, tm, MLA_QK), lambda i: (0, i, 0)),
            pl.BlockSpec((MLA_HEADS, tm, 2 * MLA_V), lambda i: (0, i, 0)),
        ],
        out_shape=[
            jax.ShapeDtypeStruct((MLA_HEADS, n, MLA_QK), BF16),
            jax.ShapeDtypeStruct((MLA_HEADS, n, MLA_QK), BF16),
            jax.ShapeDtypeStruct((MLA_HEADS, n, 2 * MLA_V), BF16),
        ],
        compiler_params=_cparams(("parallel",)),
        name="mla_proj",
    )(x, mod, mod, g1, cos_t, sin_t, *weights)


def _attn_lat_kernel(q_ref, kc_ref, kl_ref, vc_ref, vl_ref, o_ref, sa_s, sb_s, ma_s, mb_s, *, tq, kb):
    heads, lq = q_ref.shape[:2]
    lc = kc_ref.shape[1]
    n_tiles = lq // tq
    blocks = [(kc_ref, vc_ref, 0, lc, 0)] + [(kl_ref, vl_ref, j * kb, kb, lc + j * kb)
                                             for j in range(kl_ref.shape[1] // kb)]
    slots = ((sa_s, ma_s), (sb_s, mb_s))

    def iteration(first, second, slot):
        s_w, m_w = slots[slot]
        s_r, m_r = slots[1 - slot]
        if first is not None:
            q = q_ref[first[0], pl.ds(pl.multiple_of(first[1] * tq, tq), tq), :]
            mp = jnp.full((tq, LANES), -jnp.inf, F32)
        if second is not None:
            m = m_r[...]
            acc = jnp.zeros((tq, 2 * MLA_V), F32)
        for k_ref, v_ref, start, size, off in blocks:
            if first is not None:
                s = _dot_nt(q, k_ref[first[0], start:start + size, :])
                s_w[:, off:off + size] = s
                for c in range(size // LANES):
                    mp = jnp.maximum(mp, s[:, c * LANES:(c + 1) * LANES])
            if second is not None:
                p = jnp.concatenate(
                    [jnp.exp2(s_r[:, off + c * LANES:off + (c + 1) * LANES] - m)
                     for c in range(size // LANES)], axis=1).astype(BF16)
                acc = acc + _dot(p, v_ref[second[0], start:start + size, :])
        if first is not None:
            m_w[...] = jnp.broadcast_to(jnp.max(mp, axis=-1, keepdims=True), (tq, LANES))
        if second is not None:
            rows = pl.ds(pl.multiple_of(second[1] * tq, tq), tq)
            o_ref[rows, second[0] * MLA_V:(second[0] + 1) * MLA_V] = (
                acc[:, :MLA_V] / acc[:, MLA_V:]).astype(o_ref.dtype)

    def in_head(hd, lo, hi):
        count = hi - lo
        par = (hd * n_tiles + lo) % 2

        def pair(j, carry):
            i = lo + 2 * j
            iteration((hd, i), (hd, i - 1), par)
            iteration((hd, i + 1), (hd, i), 1 - par)
            return carry

        lax.fori_loop(0, count // 2, pair, 0)
        if count % 2:
            iteration((hd, hi - 1), (hd, hi - 2), (hd * n_tiles + hi - 1) % 2)

    iteration((0, 0), None, 0)
    for hd in range(heads):
        if hd > 0:
            iteration((hd, 0), (hd - 1, n_tiles - 1), (hd * n_tiles) % 2)
        in_head(hd, 1, n_tiles)
    iteration(None, (heads - 1, n_tiles - 1), (heads * n_tiles) % 2)


def _attn_ctx_kernel(q_ref, kc_ref, vc_ref, o_ref):
    sc = _dot_nt(q_ref[...], kc_ref[...])
    m = jnp.max(sc, axis=-1, keepdims=True)
    o = _dot(jnp.exp2(sc - m).astype(BF16), vc_ref[...])
    o_ref[...] = (o[:, :MLA_V] / o[:, MLA_V:]).astype(o_ref.dtype)


def _attn_lat(q, kl, vl, kc, vc, nb):
    nh, n, _ = q.shape
    l = n // nb
    lc = kc.shape[1] // nb
    tq = _pick_tile(l, cap=256)
    kb = _pick_tile(l, cap=MXU_N)
    hp = 2 if nh % 2 == 0 else 1
    assert lc % LANES == 0
    return pl.pallas_call(
        functools.partial(_attn_lat_kernel, tq=tq, kb=kb),
        grid=(nb, nh // hp),
        in_specs=[
            pl.BlockSpec((hp, l, MLA_QK), lambda b, h: (h, b, 0)),
            pl.BlockSpec((hp, lc, MLA_QK), lambda b, h: (h, b, 0)),
            pl.BlockSpec((hp, l, MLA_QK), lambda b, h: (h, b, 0)),
            pl.BlockSpec((hp, lc, 2 * MLA_V), lambda b, h: (h, b, 0)),
            pl.BlockSpec((hp, l, 2 * MLA_V), lambda b, h: (h, b, 0)),
        ],
        out_specs=pl.BlockSpec((l, hp * MLA_V), lambda b, h: (b, h)),
        out_shape=jax.ShapeDtypeStruct((n, nh * MLA_V), BF16),
        scratch_shapes=[pltpu.VMEM((tq, lc + l), F32), pltpu.VMEM((tq, lc + l), F32),
                        pltpu.VMEM((tq, LANES), F32), pltpu.VMEM((tq, LANES), F32)],
        compiler_params=_cparams(("parallel", "parallel")),
        name="attn_lat",
    )(q, kc, kl, vc, vl)


def _attn_ctx(q, kc, vc, nb):
    nh, n, _ = q.shape
    lc = n // nb
    return pl.pallas_call(
        _attn_ctx_kernel,
        grid=(nb, nh),
        in_specs=[
            pl.BlockSpec((None, lc, MLA_QK), lambda b, h: (h, b, 0)),
            pl.BlockSpec((None, lc, MLA_QK), lambda b, h: (h, b, 0)),
            pl.BlockSpec((None, lc, 2 * MLA_V), lambda b, h: (h, b, 0)),
        ],
        out_specs=pl.BlockSpec((lc, MLA_V), lambda b, h: (b, h)),
        out_shape=jax.ShapeDtypeStruct((n, nh * MLA_V), BF16),
        compiler_params=_cparams(("parallel", "parallel")),
        name="attn_ctx",
    )(q, kc, vc)


def _rope_tables(n_tok):
    rows = n_tok // GRID_W
    row = np.repeat(np.arange(rows, dtype=np.float32), GRID_W)
    col = np.tile(np.arange(GRID_W, dtype=np.float32), rows)
    n_freq = MLA_ROPE // 4
    freqs = (np.float32(ROPE_THETA) ** (-np.arange(n_freq, dtype=np.float32) / np.float32(n_freq))).astype(np.float32)
    ang_r = row[:, None] * freqs[None, :]
    ang_c = col[:, None] * freqs[None, :]
    ang = np.concatenate([ang_r, ang_r, ang_c, ang_c], axis=-1).astype(np.float32)
    sign = np.tile(np.repeat(np.array([-1.0, 1.0], np.float32), n_freq), 2)
    cos = np.cos(ang).astype(np.float32)
    sin = (np.sin(ang) * sign).astype(np.float32)
    return jnp.asarray(np.concatenate([cos, cos], axis=1)), jnp.asarray(np.concatenate([sin, sin], axis=1))


def _mla_weights(w_dq, g_q, w_uq, w_dkv, g_kv, w_ukv, g_qk, w_o):
    nh = MLA_HEADS
    qlora = w_dq.shape[1]
    kvl = g_kv.shape[0]
    uq = w_uq.reshape(qlora, nh, MLA_QK)
    wuq = jnp.concatenate([uq[:, :, :MLA_NOPE].reshape(qlora, nh * MLA_NOPE),
                           uq[:, :, MLA_NOPE:].reshape(qlora, nh * MLA_ROPE)], axis=1)
    ukv = w_ukv.reshape(kvl, nh, MLA_NOPE + MLA_V)
    wukv = jnp.concatenate([ukv[:, :, :MLA_NOPE].reshape(kvl, nh * MLA_NOPE),
                            ukv[:, :, MLA_NOPE:].reshape(kvl, nh * MLA_V)], axis=1)
    wdkv = jnp.concatenate([w_dkv, w_dkv[:, kvl:]], axis=1)
    dr = nh * MLA_ROPE
    seg_id = jnp.arange(dr) // MLA_ROPE
    seg = (seg_id[:, None] == seg_id[None, :]).astype(F32) / MLA_ROPE
    return {
        "wd": jnp.concatenate([w_dq, wdkv], axis=1).astype(BF16), "gq": g_q[None, :], "wuq": wuq.astype(BF16),
        "gkv": g_kv[None, :], "wukv": wukv.astype(BF16),
        "gqn": g_qk[0:1, :MLA_NOPE], "gqr": jnp.tile(g_qk[0:1, MLA_NOPE:], (1, nh)),
        "gkn": g_qk[1:2, :MLA_NOPE], "gkr": jnp.tile(g_qk[1:2, MLA_NOPE:], (1, 2)),
        "seg": seg.astype(BF16),
    }


def _mla_layer(lat, cx, mod_l, mod_c, nb, g1, wts, need_ctx):
    l = lat.shape[0] // nb
    lc = cx.shape[0] // nb
    w = _mla_weights(*wts)
    cos_t, sin_t = _rope_tables(l)
    tmc = _pick_tile(lc, cap=512)
    ql, kl, vl = _mla_proj(lat, mod_l, l, g1, cos_t, sin_t, w)
    qc, kc, vc = _mla_proj(cx, mod_c, lc, g1, jnp.ones((tmc, LANES), F32), jnp.zeros((tmc, LANES), F32), w)
    y = _attn_lat(ql, kl, vl, kc, vc, nb)
    yc = _attn_ctx(qc, kc, vc, nb) if need_ctx else None
    return y, yc


def _s5_kernel(xf_ref, xr_ref, sh_ref, sc_ref, g1_ref, dd_ref, wb_ref, wc_ref, are_ref, aim_ref, st0_ref,
               yf_ref, yr_ref, st_ref, in_s, out_s, in2_s, *, pitch):
    nb, t, d = xf_ref.shape
    nt = st_ref.shape[2]
    nre = nt // 2
    nv = nre // SUBLANES
    nj = wb_ref.shape[1]
    kblocks = d // LANES
    per_kb = (nj // 2) // kblocks
    nq = wc_ref.shape[1]
    tiles_q = nre // nq
    step = pl.program_id(0)

    @pl.when(step == 0)
    def _():
        st_ref[...] = st0_ref[...]

    def prep(x_ref):
        return _rms(x_ref[...], g1_ref[...]) * (1.0 + sc_ref[...]) + sh_ref[...]

    hs = (prep(xf_ref), prep(xr_ref))
    skip = hs[0] * dd_ref[...]
    outs = (yf_ref, yr_ref)
    bpl = 2 if nb % 2 == 0 else 1
    hbs = [hs[dr].reshape(nb * t, d).astype(BF16) for dr in range(2)]

    def slab(b, n):
        return (b * nt + n) * pitch

    def in_tile(dr, j, dst):
        kb = (j % (nj // 2)) // per_kb
        r = _dot(hbs[dr][:, kb * LANES:(kb + 1) * LANES], wb_ref[dr, j])
        for b in range(nb):
            for half in range(MXU_N // LANES):
                n = j * (MXU_N // LANES) + half
                dst[pl.ds(slab(b, n), t), :] = r[b * t:(b + 1) * t, half * LANES:(half + 1) * LANES]

    kparts = 2 * tiles_q // (MXU_N // LANES)

    def out_part(dr, q, kc, src, acc):
        rows = []
        for b in range(nb):
            pieces = []
            for i in range(MXU_N // LANES):
                idx = kc * (MXU_N // LANES) + i
                part, within = divmod(idx, tiles_q)
                pieces.append(src[pl.ds(slab(b, part * nre + q * tiles_q + within), t), :])
            rows.append(jnp.concatenate(pieces, axis=1))
        part_dot = _dot(jnp.concatenate(rows, axis=0).astype(BF16), wc_ref[dr, q, kc * MXU_N:(kc + 1) * MXU_N, :])
        return part_dot if acc is None else acc + part_dot

    def out_store(dr, q, yq):
        for b in range(nb):
            val = yq[b * t:(b + 1) * t, :]
            if dr == 0:
                val = val + skip[b, :, q * LANES:(q + 1) * LANES]
            outs[dr][b, :, q * LANES:(q + 1) * LANES] = val

    def scan_phase(dr, src, dst, side_work):
        consts = [(are_ref[dr, v * SUBLANES:(v + 1) * SUBLANES, :], aim_ref[dr, v * SUBLANES:(v + 1) * SUBLANES, :])
                  for v in range(nv)]
        groups = list(range(0, nb, bpl))
        total = len(groups) * t
        done = 0
        for gi, b0 in enumerate(groups):
            state = [st_ref[b0 + bb, dr, part * nre + v * SUBLANES:part * nre + (v + 1) * SUBLANES, :]
                     for bb in range(bpl) for v in range(nv) for part in range(2)]
            for i in range(t):
                tt = i if dr == 0 else t - 1 - i
                nxt = []
                for bb in range(bpl):
                    for v in range(nv):
                        sre, sim = state[(bb * nv + v) * 2], state[(bb * nv + v) * 2 + 1]
                        are, aim = consts[v]
                        row_re = slab(b0 + bb, v * SUBLANES) + tt
                        row_im = slab(b0 + bb, nre + v * SUBLANES) + tt
                        bre = src[pl.ds(row_re, SUBLANES, stride=pitch), :]
                        bim = src[pl.ds(row_im, SUBLANES, stride=pitch), :]
                        new_re = are * sre - aim * sim + bre
                        new_im = are * sim + aim * sre + bim
                        dst[pl.ds(row_re, SUBLANES, stride=pitch), :] = new_re
                        dst[pl.ds(row_im, SUBLANES, stride=pitch), :] = new_im
                        nxt += [new_re, new_im]
                state = nxt
                step_no = gi * t + i + 1
                while done < len(side_work) and done * total < step_no * len(side_work):
                    side_work[done]()
                    done += 1
            k = 0
            for bb in range(bpl):
                for v in range(nv):
                    for part in range(2):
                        st_ref[b0 + bb, dr, part * nre + v * SUBLANES:part * nre + (v + 1) * SUBLANES, :] = state[k]
                        k += 1
        while done < len(side_work):
            side_work[done]()
            done += 1

    for j in range(nj):
        in_tile(0, j, in_s)
    scan_phase(0, in_s, out_s, [functools.partial(in_tile, 1, j, in2_s) for j in range(nj)])

    acc0 = {}

    def fwd_out(q, kc):
        acc0[q] = out_part(0, q, kc, out_s, acc0.get(q))
        if kc == kparts - 1:
            out_store(0, q, acc0.pop(q))

    scan_phase(1, in2_s, in_s, [functools.partial(fwd_out, q, kc) for q in range(nq) for kc in range(kparts)])
    for q in range(nq):
        yq = None
        for kc in range(kparts):
            yq = out_part(1, q, kc, in_s, yq)
        out_store(1, q, yq)


def _s5_scan(x3, mod3, g1, dd, wb, wc, are, aim, st0, t):
    nb, l, d = x3.shape
    nc = l // t
    pitch = t + SUBLANES // 2
    nt = st0.shape[2]
    full = lambda a: pl.BlockSpec(a.shape, lambda c: (0,) * a.ndim)
    buf = pltpu.VMEM((nb * nt * pitch, LANES), F32)
    return pl.pallas_call(
        functools.partial(_s5_kernel, pitch=pitch),
        grid=(nc,),
        in_specs=[
            pl.BlockSpec((nb, t, d), lambda c: (0, c, 0)),
            pl.BlockSpec((nb, t, d), lambda c: (0, nc - 1 - c, 0)),
            pl.BlockSpec((nb, 1, d), lambda c: (0, 0, 0)),
            pl.BlockSpec((nb, 1, d), lambda c: (0, 0, 1)),
            full(g1), full(dd), full(wb), full(wc), full(are), full(aim), full(st0),
        ],
        out_specs=[
            pl.BlockSpec((nb, t, d), lambda c: (0, c, 0)),
            pl.BlockSpec((nb, t, d), lambda c: (0, nc - 1 - c, 0)),
            full(st0),
        ],
        out_shape=[
            jax.ShapeDtypeStruct((nb, l, d), F32),
            jax.ShapeDtypeStruct((nb, l, d), F32),
            jax.ShapeDtypeStruct(st0.shape, F32),
        ],
        scratch_shapes=[buf, buf, buf],
        compiler_params=_cparams(("arbitrary",)),
        name="s5_scan",
    )(x3, x3, mod3, mod3, g1, dd, wb, wc, are, aim, st0)


def _s5_glu_kernel(yf_ref, yr_ref, w_ref, o_ref):
    d = o_ref.shape[-1]
    z = _dot(_gelu(yf_ref[...] + yr_ref[...]).astype(BF16), w_ref[...])
    o_ref[...] = z[:, :d] * _sigmoid_tanh(z[:, d:])


def _s5_glu(yf, yr, w, lseg):
    n, d = yf.shape
    tm = _pick_tile(lseg)
    return pl.pallas_call(
        _s5_glu_kernel,
        grid=(n // tm,),
        in_specs=[pl.BlockSpec((tm, d), lambda i: (i, 0)), pl.BlockSpec((tm, d), lambda i: (i, 0)),
                  pl.BlockSpec(w.shape, lambda i: (0, 0))],
        out_specs=pl.BlockSpec((tm, d), lambda i: (i, 0)),
        out_shape=jax.ShapeDtypeStruct((n, d), F32),
        compiler_params=_cparams(("parallel",)),
        name="s5_glu",
    )(yf, yr, w)


def _s5_weights(a_re, a_im, log_dt, b_re, b_im, c_re, c_im):
    ng, ns = a_re.shape[1:]
    gi = b_re.shape[-1]
    d = ng * gi
    lr, li = a_re.astype(F32), a_im.astype(F32)
    dt = jnp.exp(log_dt.astype(F32))[..., None]
    mag = jnp.exp(lr * dt)
    ar, ai = mag * jnp.cos(li * dt), mag * jnp.sin(li * dt)
    den = lr * lr + li * li
    wr = ((ar - 1.0) * lr + ai * li) / den
    wi = (ai * lr - (ar - 1.0) * li) / den
    br, bi = b_re.astype(F32), b_im.astype(F32)
    bb_re = wr[..., None] * br - wi[..., None] * bi
    bb_im = wr[..., None] * bi + wi[..., None] * br

    gpt = MXU_N // ns
    gpk = LANES // gi
    ncol = ng // gpt
    sel = (jnp.arange(gpk)[None, :, None]
           == gpt * (jnp.arange(ncol)[:, None, None] % (gpk // gpt)) + jnp.arange(gpt)[None, None, :]).astype(F32)

    def in_tiles(part):
        blk = part.reshape(2, ncol, gpt, ns, gi).transpose(0, 1, 4, 2, 3)
        tiles = blk[:, :, None, :, :, :] * sel[None, :, :, None, :, None]
        return tiles.reshape(2, ncol, gpk * gi, gpt * ns)

    wb = jnp.concatenate([in_tiles(bb_re), in_tiles(bb_im)], axis=1).astype(BF16)

    nq = d // LANES
    gpq = ng // nq
    eye = jnp.eye(gpq, dtype=F32)

    def out_tiles(part):
        blk = jnp.swapaxes(part.astype(F32), -1, -2).reshape(2, nq, gpq, ns, gi)
        tiles = blk[:, :, :, :, None, :] * eye[None, None, :, None, :, None]
        return tiles.reshape(2, nq, gpq * ns, gpq * gi)

    wc = jnp.concatenate([out_tiles(c_re), out_tiles(-c_im)], axis=2).astype(BF16)
    are = ar.reshape(2, ng * ns // LANES, LANES)
    aim = ai.reshape(2, ng * ns // LANES, LANES)
    return wb, wc, are, aim


def _s5_layer(lat, cx, mod_l, mod_c, nb, g1, a_re, a_im, log_dt, b_re, b_im, c_re, c_im, dvec, w_glu, need_ctx):
    d = lat.shape[1]
    l = lat.shape[0] // nb
    lc = cx.shape[0] // nb
    wb, wc, are, aim = _s5_weights(a_re, a_im, log_dt, b_re, b_im, c_re, c_im)
    t = _pick_tile(l, lc, cap=64)
    st0 = jnp.zeros((nb, 2, 2 * are.shape[1], LANES), F32)
    dd = dvec[None, :].astype(F32)
    ycf, ycr, st = _s5_scan(cx.reshape(nb, lc, d), mod_c, g1, dd, wb, wc, are, aim, st0, t)
    yf, yr, _ = _s5_scan(lat.reshape(nb, l, d), mod_l, g1, dd, wb, wc, are, aim, st, t)
    wg = w_glu.astype(BF16)
    y = _s5_glu(yf.reshape(nb * l, d), yr.reshape(nb * l, d), wg, l)
    yc = _s5_glu(ycf.reshape(nb * lc, d), ycr.reshape(nb * lc, d), wg, lc) if need_ctx else None
    return y, yc


def _lru_in_kernel(x_ref, xp_ref, xn_ref, sh_ref, sc_ref, g1_ref, w_ref, cw_ref, cb_ref,
                   xr_ref, gl_ref, ext_s, *, lseg):
    tm, wd = xr_ref.shape
    halo = xp_ref.shape[0]
    row0 = pl.program_id(0) * tm
    at_start = (row0 % lseg) == 0
    at_end = ((row0 + tm) % lseg) == 0

    def prep(x):
        return (_rms(x, g1_ref[...]) * (1.0 + sc_ref[...]) + sh_ref[...]).astype(BF16)

    w = w_ref[...]
    h = prep(x_ref[...])
    z = _dot(h, w)
    gl_ref[...] = _gelu(z[:, :wd]).astype(gl_ref.dtype)
    ext_s[0:halo, :] = jnp.where(at_start, 0.0, _dot(prep(xp_ref[...]), w[:, wd:]))
    ext_s[halo:halo + tm, :] = z[:, wd:]
    ext_s[halo + tm:2 * halo + tm, :] = jnp.where(at_end, 0.0, _dot(prep(xn_ref[...]), w[:, wd:]))
    rows = min(tm, LANES)
    win = rows + 2 * halo
    for c0 in range(0, wd, LANES):
        for r0 in range(0, tm, rows):
            xe = ext_s[r0:r0 + win, c0:c0 + LANES]
            acc = cb_ref[:, c0:c0 + LANES] + cw_ref[1:2, c0:c0 + LANES] * xe[halo:halo + rows]
            for tap in (0, 2, 3):
                shifted = pltpu.roll(xe, (1 - tap) % win, 0)[halo:halo + rows]
                acc = acc + cw_ref[tap:tap + 1, c0:c0 + LANES] * shifted
            xr_ref[r0:r0 + rows, c0:c0 + LANES] = acc


def _lru_in(x, mod, lseg, g1, w_in, cw, cb):
    n, d = x.shape
    wd = w_in.shape[1] // 2
    tm = _pick_tile(lseg)
    hb = tm // SUBLANES
    nhb = n // SUBLANES
    full = lambda a: pl.BlockSpec(a.shape, lambda i: (0,) * a.ndim)
    return pl.pallas_call(
        functools.partial(_lru_in_kernel, lseg=lseg),
        grid=(n // tm,),
        in_specs=[pl.BlockSpec((tm, d), lambda i: (i, 0)),
                  pl.BlockSpec((SUBLANES, d), lambda i: (jnp.maximum(i * hb - 1, 0), 0)),
                  pl.BlockSpec((SUBLANES, d), lambda i: (jnp.minimum((i + 1) * hb, nhb - 1), 0)),
                  _mod_spec(0, tm, lseg, d), _mod_spec(1, tm, lseg, d),
                  full(g1), full(w_in), full(cw), full(cb)],
        out_specs=[pl.BlockSpec((tm, wd), lambda i: (i, 0)), pl.BlockSpec((tm, wd), lambda i: (i, 0))],
        out_shape=[jax.ShapeDtypeStruct((n, wd), F32), jax.ShapeDtypeStruct((n, wd), BF16)],
        scratch_shapes=[pltpu.VMEM((tm + 2 * SUBLANES, wd), F32)],
        compiler_params=_cparams(("parallel",)),
        name="lru_in",
    )(x, x, x, mod, mod, g1, w_in, cw, cb)


def _lru_scan_kernel(xf_ref, xr_ref, wg_ref, bg_ref, sp_ref, st0_ref,
                     hf_ref, hr_ref, st_ref, a_s, b_s, h_s, *, pitch):
    nb, t, wd = xf_ref.shape
    nblk = wd // LANES
    step = pl.program_id(0)

    @pl.when(step == 0)
    def _():
        st_ref[...] = st0_ref[...]

    def coeffs(dr, x_ref):
        for n in range(nblk):
            xb = x_ref[:, :, n * LANES:(n + 1) * LANES].reshape(nb * t, LANES)
            g = _dot(xb.astype(BF16), wg_ref[dr, n]) + bg_ref[dr, n]
            c1 = sp_ref[dr, n]
            a = jnp.exp2(c1 * jnp.tanh(g[:, :LANES]) + c1)
            bc = jnp.sqrt(1.0 - a * a) * ((0.5 * xb) * (jnp.tanh(g[:, LANES:]) + 1.0))
            for b in range(nb):
                a_s[dr, n, pl.ds(b * pitch, t), :] = a[b * t:(b + 1) * t]
                b_s[dr, n, pl.ds(b * pitch, t), :] = bc[b * t:(b + 1) * t]

    coeffs(0, xf_ref)
    coeffs(1, xr_ref)

    init = tuple(st_ref[dr, n] for dr in range(2) for n in range(nblk))

    def body(i, carry):
        times, carry = carry[:2], carry[2:]
        out = [times[0] + 1, times[1] - 1]
        for dr in range(2):
            tt = times[dr]
            for n in range(nblk):
                a = a_s[dr, n, pl.ds(tt, nb, stride=pitch), :]
                bc = b_s[dr, n, pl.ds(tt, nb, stride=pitch), :]
                hnew = a * carry[dr * nblk + n] + bc
                h_s[dr, n, pl.ds(tt, nb, stride=pitch), :] = hnew
                out.append(hnew)
        return tuple(out)

    fin = lax.fori_loop(0, t, body, (jnp.int32(0), jnp.int32(t - 1)) + init, unroll=2)[2:]
    k = 0
    for dr in range(2):
        for n in range(nblk):
            st_ref[dr, n] = fin[k]
            k += 1
    outs = (hf_ref, hr_ref)
    for dr in range(2):
        for n in range(nblk):
            for b in range(nb):
                outs[dr][b, :, n * LANES:(n + 1) * LANES] = h_s[dr, n, pl.ds(b * pitch, t), :]


def _lru_scan(xr3, wg, bg, sp, st0, t):
    nb, l, wd = xr3.shape
    nc = l // t
    nblk = wd // LANES
    pitch = t + SUBLANES
    full = lambda a: pl.BlockSpec(a.shape, lambda c: (0,) * a.ndim)
    chunk = lambda f: pl.BlockSpec((nb, t, wd), lambda c: (0, f(c), 0))
    fwd = lambda c: c
    rev = lambda c: nc - 1 - c
    coef = pltpu.VMEM((2, nblk, nb * pitch, LANES), F32)
    return pl.pallas_call(
        functools.partial(_lru_scan_kernel, pitch=pitch),
        grid=(nc,),
        in_specs=[chunk(fwd), chunk(rev), full(wg), full(bg), full(sp), full(st0)],
        out_specs=[chunk(fwd), chunk(rev), full(st0)],
        out_shape=[jax.ShapeDtypeStruct((nb, l, wd), F32), jax.ShapeDtypeStruct((nb, l, wd), F32),
                   jax.ShapeDtypeStruct(st0.shape, F32)],
        scratch_shapes=[coef, coef, coef],
        compiler_params=_cparams(("arbitrary",)),
        name="lru_scan",
    )(xr3, xr3, wg, bg, sp, st0)


def _lru_out_kernel(gl_ref, hf_ref, hr_ref, w_ref, o_ref):
    u = gl_ref[...].astype(F32) * (hf_ref[...] + hr_ref[...])
    o_ref[...] = _dot(u.astype(BF16), w_ref[...])


def _lru_out(gl, hf, hr, w, lseg):
    n, wd = gl.shape
    d = w.shape[1]
    tm = _pick_tile(lseg)
    row = lambda: pl.BlockSpec((tm, wd), lambda i: (i, 0))
    return pl.pallas_call(
        _lru_out_kernel,
        grid=(n // tm,),
        in_specs=[row(), row(), row(), pl.BlockSpec(w.shape, lambda i: (0, 0))],
        out_specs=pl.BlockSpec((tm, d), lambda i: (i, 0)),
        out_shape=jax.ShapeDtypeStruct((n, d), F32),
        compiler_params=_cparams(("parallel",)),
        name="lru_out",
    )(gl, hf, hr, w)


def _lru_layer(lat, cx, mod_l, mod_c, nb, g1, w_in, conv_w, conv_b, w_gate, b_gate, lam, w_out, need_ctx):
    l = lat.shape[0] // nb
    lc = cx.shape[0] // nb
    wd = w_in.shape[1] // 2
    nblk = wd // LANES
    w_in_b = w_in.astype(BF16)
    cw = conv_w.astype(F32)
    cb = conv_b.astype(F32)[None, :]
    xw, gl = _lru_in(lat, mod_l, l, g1, w_in_b, cw, cb)
    xwc, glc = _lru_in(cx, mod_c, lc, g1, w_in_b, cw, cb)
    wg = (0.5 * jnp.concatenate([w_gate[:, 0], w_gate[:, 1]], axis=-1)).astype(BF16)
    bgate = b_gate.astype(F32).reshape(2, 2, nblk, 1, LANES)
    bg = 0.5 * jnp.concatenate([bgate[:, 0], bgate[:, 1]], axis=-1)
    sp = (-0.5 * LRU_C * math.log2(math.e)) * jax.nn.softplus(-lam.astype(F32)).reshape(2, nblk, 1, LANES)
    t = _pick_tile(l, lc, cap=64)
    st0 = jnp.zeros((2, nblk, nb, LANES), F32)
    hcf, hcr, st = _lru_scan(xwc.reshape(nb, lc, wd), wg, bg, sp, st0, t)
    hf, hr, _ = _lru_scan(xw.reshape(nb, l, wd), wg, bg, sp, st, t)
    wo = w_out.astype(BF16)
    y = _lru_out(gl, hf.reshape(nb * l, wd), hr.reshape(nb * l, wd), wo, l)
    yc = _lru_out(glc, hcf.reshape(nb * lc, wd), hcr.reshape(nb * lc, wd), wo, lc) if need_ctx else None
    return y, yc


def kernel(x, c, ctx, c_ctx, ada_w, ada_b, norm_g, mla_w_dq, mla_g_q, mla_w_uq, mla_w_dkv, mla_g_kv, mla_w_ukv, mla_g_qk, mla_w_o, s5_a_re, s5_a_im, s5_log_dt, s5_b_re, s5_b_im, s5_c_re, s5_c_im, s5_d, s5_w_glu, lru_w_in, lru_conv_w, lru_conv_b, lru_w_gate, lru_b_gate, lru_lambda, lru_w_out, mlp_w1, mlp_w2):
    nb, l, d = x.shape
    lc = ctx.shape[1]
    depth = ada_w.shape[0]
    assert nb < SUBLANES
    cc = jnp.concatenate([c, c_ctx[None, :], jnp.zeros((SUBLANES - nb - 1, d), F32)], axis=0)
    mod = _ada_mod(cc, ada_w, ada_b)
    lat = x.reshape(nb * l, d)
    cx = ctx.reshape(nb * lc, d)
    for i in range(depth):
        need_ctx = i < depth - 1
        mod_l = mod[i, :nb, None, :]
        mod_c = jnp.broadcast_to(mod[i, nb][None, None, :], (nb, 1, mod.shape[-1]))
        g1 = norm_g[i, 0][None, :]
        g2 = norm_g[i, 1][None, :]
        kind, j = i % N_MIXERS, i // N_MIXERS
        if kind == 0:
            y, yc = _mla_layer(lat, cx, mod_l, mod_c, nb, g1,
                               (mla_w_dq[j], mla_g_q[j], mla_w_uq[j], mla_w_dkv[j], mla_g_kv[j], mla_w_ukv[j],
                                mla_g_qk[j], mla_w_o[j]), need_ctx)
        elif kind == 1:
            y, yc = _s5_layer(lat, cx, mod_l, mod_c, nb, g1, s5_a_re[j], s5_a_im[j], s5_log_dt[j], s5_b_re[j],
                              s5_b_im[j], s5_c_re[j], s5_c_im[j], s5_d[j], s5_w_glu[j], need_ctx)
        else:
            y, yc = _lru_layer(lat, cx, mod_l, mod_c, nb, g1, lru_w_in[j], lru_conv_w[j], lru_conv_b[j],
                               lru_w_gate[j], lru_b_gate[j], lru_lambda[j], lru_w_out[j], need_ctx)
        wo = mla_w_o[j] if kind == 0 else None
        lat = _mlp(lat, y, mod_l, l, g2, mlp_w1, mlp_w2, i, wo)
        if need_ctx:
            cx = _mlp(cx, yc, mod_c, lc, g2, mlp_w1, mlp_w2, i, wo)
    return lat.reshape(nb, l, d)
```

```python
import functools
import math

import jax
import jax.numpy as jnp
import numpy as np
from jax import lax
from jax.experimental import pallas as pl
from jax.experimental.pallas import tpu as pltpu

F32 = jnp.float32
BF16 = jnp.bfloat16

EPS = 1e-6
ROPE_THETA = 10000.0
GRID_W = 64
N_MIXERS = 3

MLA_HEADS = 8
MLA_NOPE = 128
MLA_ROPE = 64
MLA_V = 128
MLA_QK = MLA_NOPE + MLA_ROPE
MLA_SCALE = math.log2(math.e) / math.sqrt(MLA_QK)

S5_GROUP = 16
S5_STATE = 64
LRU_BLOCKS = 10
LRU_BW = 128
LRU_C = 8.0

LANES = 128
SUBLANES = 8
MXU_N = 256
VMEM_LIMIT = 56 * 1024 * 1024


def _cparams(sem):
    return pltpu.CompilerParams(dimension_semantics=sem, vmem_limit_bytes=VMEM_LIMIT)


def _dot(a, b):
    return jnp.dot(a, b, preferred_element_type=F32)


def _dot_nt(a, b):
    return lax.dot_general(a, b, (((1,), (1,)), ((), ())), preferred_element_type=F32)


def _rms(x, g):
    inv = lax.rsqrt(jnp.mean(x * x, axis=-1, keepdims=True) + EPS)
    return x * inv * g


def _gelu(x):
    c = math.sqrt(2.0 / math.pi)
    return x * (0.5 * (1.0 + jnp.tanh(c * (x + 0.044715 * (x * x * x)))))


def _sigmoid(x):
    return 1.0 / (1.0 + jnp.exp(-x))


def _sigmoid_tanh(x):
    return 0.5 * jnp.tanh(0.5 * x) + 0.5


def _pick_tile(*lens, cap=512):
    t = cap
    while any(n % t for n in lens):
        t //= 2
    assert t >= SUBLANES
    return t


def _ada_kernel(c_ref, w_ref, b_ref, o_ref):
    c = c_ref[...]
    s = c * _sigmoid(c)
    w = w_ref[...]
    s_hi = s.astype(BF16)
    s_lo = (s - s_hi.astype(F32)).astype(BF16)
    w_hi = w.astype(BF16)
    w_lo = (w - w_hi.astype(F32)).astype(BF16)
    o_ref[...] = _dot(s_hi, w_hi) + _dot(s_lo, w_hi) + _dot(s_hi, w_lo) + b_ref[...]


def _ada_mod(cc, ada_w, ada_b):
    depth, d, d6 = ada_w.shape
    nk = d6 // d
    return pl.pallas_call(
        _ada_kernel,
        grid=(depth, nk),
        in_specs=[
            pl.BlockSpec((SUBLANES, d), lambda l, k: (0, 0)),
            pl.BlockSpec((None, d, d), lambda l, k: (l, 0, k)),
            pl.BlockSpec((None, 1, d), lambda l, k: (l, 0, k)),
        ],
        out_specs=pl.BlockSpec((None, SUBLANES, d), lambda l, k: (l, 0, k)),
        out_shape=jax.ShapeDtypeStruct((depth, SUBLANES, d6), F32),
        compiler_params=_cparams(("parallel", "parallel")),
        name="ada_mod",
    )(cc, ada_w, ada_b.reshape(depth, 1, d6))


def _mod_spec(k, tm, lseg, d, tile_of=lambda i: i):
    return pl.BlockSpec((None, 1, d), lambda i, *_: ((tile_of(i) * tm) // lseg, 0, k))


def _mlp_kernel(*refs, has_wo):
    if has_wo:
        (x_ref, y_ref, ga_ref, shm_ref, scm_ref, gm_ref, g2_ref, w1_ref, w2_ref, wo_ref,
         o_ref, w1_s, w2_s, wo_s) = refs
    else:
        x_ref, y_ref, ga_ref, shm_ref, scm_ref, gm_ref, g2_ref, w1_ref, w2_ref, o_ref, w1_s, w2_s = refs
    step = pl.program_id(0)
    nj = w1_s.shape[0]

    @pl.when(step < nj)
    def _():
        w1_s[step] = w1_ref[...].astype(BF16)
        w2_s[step] = w2_ref[...].astype(BF16)

    if has_wo:
        nk, kc = wo_s.shape[:2]

        @pl.when(step < nk)
        def _():
            wo_s[step] = wo_ref[...].astype(BF16)

    @pl.when(step >= nj)
    def _():
        if has_wo:
            yb = y_ref[...].astype(BF16)
            y = _dot(yb[:, :kc], wo_s[0])
            for c in range(1, nk):
                y = y + _dot(yb[:, c * kc:(c + 1) * kc], wo_s[c])
        else:
            y = y_ref[...].astype(F32)
        x1 = x_ref[...] + ga_ref[...] * y
        h = (_rms(x1, g2_ref[...]) * (1.0 + scm_ref[...]) + shm_ref[...]).astype(BF16)
        acc = jnp.zeros_like(x1)
        for c in range(nj):
            t = jnp.maximum(_dot(h, w1_s[c]), 0.0)
            acc = acc + _dot((t * t).astype(BF16), w2_s[c])
        o_ref[...] = x1 + gm_ref[...] * acc


def _mlp(x, y, mod, lseg, g2, w1_all, w2_all, layer, wo=None):
    n, d = x.shape
    dy = y.shape[1]
    dff = w1_all.shape[2]
    tm = _pick_tile(lseg)
    tf = min(dff, 512)
    nj = dff // tf
    tile = lambda s: jnp.maximum(s - nj, 0)
    wblk = lambda s: jnp.minimum(s, nj - 1)
    row = lambda width: pl.BlockSpec((tm, width), lambda s: (tile(s), 0))
    in_specs = [
        row(d), row(dy),
        _mod_spec(2, tm, lseg, d, tile),
        _mod_spec(3, tm, lseg, d, tile),
        _mod_spec(4, tm, lseg, d, tile),
        _mod_spec(5, tm, lseg, d, tile),
        pl.BlockSpec((1, d), lambda s: (0, 0)),
        pl.BlockSpec((None, d, tf), lambda s: (layer, 0, wblk(s))),
        pl.BlockSpec((None, tf, d), lambda s: (layer, wblk(s), 0)),
    ]
    scratch = [pltpu.VMEM((nj, d, tf), BF16), pltpu.VMEM((nj, tf, d), BF16)]
    args = [x, y, mod, mod, mod, mod, g2, w1_all, w2_all]
    if wo is not None:
        kc = min(dy, MXU_N)
        nk = dy // kc
        assert nk <= nj
        in_specs.append(pl.BlockSpec((kc, d), lambda s: (jnp.minimum(s, nk - 1), 0)))
        scratch.append(pltpu.VMEM((nk, kc, d), BF16))
        args.append(wo)
    return pl.pallas_call(
        functools.partial(_mlp_kernel, has_wo=wo is not None),
        grid=(nj + n // tm,),
        in_specs=in_specs,
        out_specs=row(d),
        out_shape=jax.ShapeDtypeStruct((n, d), F32),
        scratch_shapes=scratch,
        compiler_params=_cparams(("arbitrary",)),
        name="mlp",
    )(*args)


def _rope(x, cos, sin_signed):
    width = x.shape[-1]
    quarter = MLA_ROPE // 4
    lane = lax.broadcasted_iota(jnp.int32, x.shape, 1)
    even_quarter = ((lane // quarter) % 2) == 0
    rot = jnp.where(even_quarter, pltpu.roll(x, width - quarter, 1), pltpu.roll(x, quarter, 1))
    return x * cos + rot * sin_signed


def _mla_proj_kernel(x_ref, sh_ref, sc_ref, g1_ref, cos_ref, sin_ref,
                     wd_ref, gq_ref, wuq_ref, gkv_ref, wukv_ref,
                     gqn_ref, gqr_ref, gkn_ref, gkr_ref, seg_ref,
                     q_ref, k_ref, v_ref):
    nh = q_ref.shape[0]
    dn = nh * MLA_NOPE
    dr = nh * MLA_ROPE
    kvl = gkv_ref.shape[-1]
    h = (_rms(x_ref[...], g1_ref[...]) * (1.0 + sc_ref[...]) + sh_ref[...]).astype(BF16)
    qlora = gq_ref.shape[-1]
    down = _dot(h, wd_ref[...])
    ql = _rms(down[:, :qlora], gq_ref[...]).astype(BF16)
    q = _dot(ql, wuq_ref[...])
    kv = down[:, qlora:]
    ckv = _rms(kv[:, :kvl], gkv_ref[...]).astype(BF16)
    kvu = _dot(ckv, wukv_ref[...])

    cos = cos_ref[...]
    sin = sin_ref[...]
    reps = dr // LANES
    cos_q = jnp.concatenate([cos] * reps, axis=1)
    sin_q = jnp.concatenate([sin] * reps, axis=1)

    qr = q[:, dn:]
    sq = qr * qr
    sq_hi = sq.astype(BF16)
    sq_lo = (sq - sq_hi.astype(F32)).astype(BF16)
    ms = _dot(sq_hi, seg_ref[...]) + _dot(sq_lo, seg_ref[...])
    qr = qr * lax.rsqrt(ms + EPS) * gqr_ref[...]
    qr = _rope(qr, cos_q, sin_q) * MLA_SCALE

    kr = kv[:, kvl:]
    kr = _rms(kr, gkr_ref[...])
    kr = _rope(kr, cos, sin)[:, :MLA_ROPE]

    for hh in range(nh):
        qn = _rms(q[:, hh * MLA_NOPE:(hh + 1) * MLA_NOPE], gqn_ref[...]) * MLA_SCALE
        qrh = qr[:, hh * MLA_ROPE:(hh + 1) * MLA_ROPE]
        q_ref[hh] = jnp.concatenate([qn, qrh], axis=1).astype(q_ref.dtype)
        kn = _rms(kvu[:, hh * MLA_NOPE:(hh + 1) * MLA_NOPE], gkn_ref[...])
        k_ref[hh] = jnp.concatenate([kn, kr], axis=1).astype(k_ref.dtype)
        v_ref[hh, :, :MLA_V] = kvu[:, dn + hh * MLA_V:dn + (hh + 1) * MLA_V].astype(v_ref.dtype)
        v_ref[hh, :, MLA_V:] = jnp.ones((kvu.shape[0], MLA_V), v_ref.dtype)


def _mla_proj(x, mod, lseg, g1, cos_t, sin_t, w):
    n, d = x.shape
    tm = _pick_tile(lseg, cap=512)
    nrope = cos_t.shape[0] // tm
    full = lambda a: pl.BlockSpec(a.shape, lambda i: (0,) * a.ndim)
    weights = [w["wd"], w["gq"], w["wuq"], w["gkv"], w["wukv"],
               w["gqn"], w["gqr"], w["gkn"], w["gkr"], w["seg"]]
    return pl.pallas_call(
        _mla_proj_kernel,
        grid=(n // tm,),
        in_specs=[
            pl.BlockSpec((tm, d), lambda i: (i, 0)),
            _mod_spec(0, tm, lseg, d),
            _mod_spec(1, tm, lseg, d),
            pl.BlockSpec((1, d), lambda i: (0, 0)),
            pl.BlockSpec((tm, LANES), lambda i: (i % nrope, 0)),
            pl.BlockSpec((tm, LANES), lambda i: (i % nrope, 0)),
        ] + [full(a) for a in weights],
        out_specs=[
            pl.BlockSpec((MLA_HEADS, tm, MLA_QK), lambda i: (0, i, 0)),
            pl.BlockSpec((MLA_HEADS, tm, MLA_QK), lambda i: (0, i, 0)),
            pl.BlockSpec((MLA_HEADS, tm, 2 * MLA_V), lambda i: (0, i, 0)),
        ],
        out_shape=[
            jax.ShapeDtypeStruct((MLA_HEADS, n, MLA_QK), BF16),
            jax.ShapeDtypeStruct((MLA_HEADS, n, MLA_QK), BF16),
            jax.ShapeDtypeStruct((MLA_HEADS, n, 2 * MLA_V), BF16),
        ],
        compiler_params=_cparams(("parallel",)),
        name="mla_proj",
    )(x, mod, mod, g1, cos_t, sin_t, *weights)


def _attn_lat_kernel(q_ref, kc_ref, kl_ref, vc_ref, vl_ref, o_ref, sa_s, sb_s, ma_s, mb_s, *, tq, kb):
    heads, lq = q_ref.shape[:2]
    lc = kc_ref.shape[1]
    n_tiles = lq // tq
    blocks = [(kc_ref, vc_ref, 0, lc, 0)] + [(kl_ref, vl_ref, j * kb, kb, lc + j * kb)
                                             for j in range(kl_ref.shape[1] // kb)]
    slots = ((sa_s, ma_s), (sb_s, mb_s))

    def iteration(first, second, slot):
        s_w, m_w = slots[slot]
        s_r, m_r = slots[1 - slot]
        if first is not None:
            q = q_ref[first[0], pl.ds(pl.multiple_of(first[1] * tq, tq), tq), :]
            mp = jnp.full((tq, LANES), -jnp.inf, F32)
        if second is not None:
            m = m_r[...]
            acc = jnp.zeros((tq, 2 * MLA_V), F32)
        for k_ref, v_ref, start, size, off in blocks:
            if first is not None:
                s = _dot_nt(q, k_ref[first[0], start:start + size, :])
                s_w[:, off:off + size] = s
                for c in range(size // LANES):
                    mp = jnp.maximum(mp, s[:, c * LANES:(c + 1) * LANES])
            if second is not None:
                p = jnp.concatenate(
                    [jnp.exp2(s_r[:, off + c * LANES:off + (c + 1) * LANES] - m)
                     for c in range(size // LANES)], axis=1).astype(BF16)
                acc = acc + _dot(p, v_ref[second[0], start:start + size, :])
        if first is not None:
            m_w[...] = jnp.broadcast_to(jnp.max(mp, axis=-1, keepdims=True), (tq, LANES))
        if second is not None:
            rows = pl.ds(pl.multiple_of(second[1] * tq, tq), tq)
            o_ref[rows, second[0] * MLA_V:(second[0] + 1) * MLA_V] = (
                acc[:, :MLA_V] / acc[:, MLA_V:]).astype(o_ref.dtype)

    def in_head(hd, lo, hi):
        count = hi - lo
        par = (hd * n_tiles + lo) % 2

        def pair(j, carry):
            i = lo + 2 * j
            iteration((hd, i), (hd, i - 1), par)
            iteration((hd, i + 1), (hd, i), 1 - par)
            return carry

        lax.fori_loop(0, count // 2, pair, 0)
        if count % 2:
            iteration((hd, hi - 1), (hd, hi - 2), (hd * n_tiles + hi - 1) % 2)

    iteration((0, 0), None, 0)
    for hd in range(heads):
        if hd > 0:
            iteration((hd, 0), (hd - 1, n_tiles - 1), (hd * n_tiles) % 2)
        in_head(hd, 1, n_tiles)
    iteration(None, (heads - 1, n_tiles - 1), (heads * n_tiles) % 2)


def _attn_ctx_kernel(q_ref, kc_ref, vc_ref, o_ref):
    sc = _dot_nt(q_ref[...], kc_ref[...])
    m = jnp.max(sc, axis=-1, keepdims=True)
    o = _dot(jnp.exp2(sc - m).astype(BF16), vc_ref[...])
    o_ref[...] = (o[:, :MLA_V] / o[:, MLA_V:]).astype(o_ref.dtype)


def _attn_lat(q, kl, vl, kc, vc, nb):
    nh, n, _ = q.shape
    l = n // nb
    lc = kc.shape[1] // nb
    tq = _pick_tile(l, cap=256)
    kb = _pick_tile(l, cap=MXU_N)
    hp = 2 if nh % 2 == 0 else 1
    assert lc % LANES == 0
    return pl.pallas_call(
        functools.partial(_attn_lat_kernel, tq=tq, kb=kb),
        grid=(nb, nh // hp),
        in_specs=[
            pl.BlockSpec((hp, l, MLA_QK), lambda b, h: (h, b, 0)),
            pl.BlockSpec((hp, lc, MLA_QK), lambda b, h: (h, b, 0)),
            pl.BlockSpec((hp, l, MLA_QK), lambda b, h: (h, b, 0)),
            pl.BlockSpec((hp, lc, 2 * MLA_V), lambda b, h: (h, b, 0)),
            pl.BlockSpec((hp, l, 2 * MLA_V), lambda b, h: (h, b, 0)),
        ],
        out_specs=pl.BlockSpec((l, hp * MLA_V), lambda b, h: (b, h)),
        out_shape=jax.ShapeDtypeStruct((n, nh * MLA_V), BF16),
        scratch_shapes=[pltpu.VMEM((tq, lc + l), F32), pltpu.VMEM((tq, lc + l), F32),
                        pltpu.VMEM((tq, LANES), F32), pltpu.VMEM((tq, LANES), F32)],
        compiler_params=_cparams(("parallel", "parallel")),
        name="attn_lat",
    )(q, kc, kl, vc, vl)


def _attn_ctx(q, kc, vc, nb):
    nh, n, _ = q.shape
    lc = n // nb
    return pl.pallas_call(
        _attn_ctx_kernel,
        grid=(nb, nh),
        in_specs=[
            pl.BlockSpec((None, lc, MLA_QK), lambda b, h: (h, b, 0)),
            pl.BlockSpec((None, lc, MLA_QK), lambda b, h: (h, b, 0)),
            pl.BlockSpec((None, lc, 2 * MLA_V), lambda b, h: (h, b, 0)),
        ],
        out_specs=pl.BlockSpec((lc, MLA_V), lambda b, h: (b, h)),
        out_shape=jax.ShapeDtypeStruct((n, nh * MLA_V), BF16),
        compiler_params=_cparams(("parallel", "parallel")),
        name="attn_ctx",
    )(q, kc, vc)


def _rope_tables(n_tok):
    rows = n_tok // GRID_W
    row = np.repeat(np.arange(rows, dtype=np.float32), GRID_W)
    col = np.tile(np.arange(GRID_W, dtype=np.float32), rows)
    n_freq = MLA_ROPE // 4
    freqs = (np.float32(ROPE_THETA) ** (-np.arange(n_freq, dtype=np.float32) / np.float32(n_freq))).astype(np.float32)
    ang_r = row[:, None] * freqs[None, :]
    ang_c = col[:, None] * freqs[None, :]
    ang = np.concatenate([ang_r, ang_r, ang_c, ang_c], axis=-1).astype(np.float32)
    sign = np.tile(np.repeat(np.array([-1.0, 1.0], np.float32), n_freq), 2)
    cos = np.cos(ang).astype(np.float32)
    sin = (np.sin(ang) * sign).astype(np.float32)
    return jnp.asarray(np.concatenate([cos, cos], axis=1)), jnp.asarray(np.concatenate([sin, sin], axis=1))


def _mla_weights(w_dq, g_q, w_uq, w_dkv, g_kv, w_ukv, g_qk, w_o):
    nh = MLA_HEADS
    qlora = w_dq.shape[1]
    kvl = g_kv.shape[0]
    uq = w_uq.reshape(qlora, nh, MLA_QK)
    wuq = jnp.concatenate([uq[:, :, :MLA_NOPE].reshape(qlora, nh * MLA_NOPE),
                           uq[:, :, MLA_NOPE:].reshape(qlora, nh * MLA_ROPE)], axis=1)
    ukv = w_ukv.reshape(kvl, nh, MLA_NOPE + MLA_V)
    wukv = jnp.concatenate([ukv[:, :, :MLA_NOPE].reshape(kvl, nh * MLA_NOPE),
                            ukv[:, :, MLA_NOPE:].reshape(kvl, nh * MLA_V)], axis=1)
    wdkv = jnp.concatenate([w_dkv, w_dkv[:, kvl:]], axis=1)
    dr = nh * MLA_ROPE
    seg_id = jnp.arange(dr) // MLA_ROPE
    seg = (seg_id[:, None] == seg_id[None, :]).astype(F32) / MLA_ROPE
    return {
        "wd": jnp.concatenate([w_dq, wdkv], axis=1).astype(BF16), "gq": g_q[None, :], "wuq": wuq.astype(BF16),
        "gkv": g_kv[None, :], "wukv": wukv.astype(BF16),
        "gqn": g_qk[0:1, :MLA_NOPE], "gqr": jnp.tile(g_qk[0:1, MLA_NOPE:], (1, nh)),
        "gkn": g_qk[1:2, :MLA_NOPE], "gkr": jnp.tile(g_qk[1:2, MLA_NOPE:], (1, 2)),
        "seg": seg.astype(BF16),
    }


def _mla_layer(lat, cx, mod_l, mod_c, nb, g1, wts, need_ctx):
    l = lat.shape[0] // nb
    lc = cx.shape[0] // nb
    w = _mla_weights(*wts)
    cos_t, sin_t = _rope_tables(l)
    tmc = _pick_tile(lc, cap=512)
    ql, kl, vl = _mla_proj(lat, mod_l, l, g1, cos_t, sin_t, w)
    qc, kc, vc = _mla_proj(cx, mod_c, lc, g1, jnp.ones((tmc, LANES), F32), jnp.zeros((tmc, LANES), F32), w)
    y = _attn_lat(ql, kl, vl, kc, vc, nb)
    yc = _attn_ctx(qc, kc, vc, nb) if need_ctx else None
    return y, yc


def _s5_kernel(xf_ref, xr_ref, sh_ref, sc_ref, g1_ref, dd_ref, wb_ref, wc_ref, are_ref, aim_ref, st0_ref,
               yf_ref, yr_ref, st_ref, in_s, out_s, in2_s, *, pitch):
    nb, t, d = xf_ref.shape
    nt = st_ref.shape[2]
    nre = nt // 2
    nv = nre // SUBLANES
    nj = wb_ref.shape[1]
    kblocks = d // LANES
    per_kb = (nj // 2) // kblocks
    nq = wc_ref.shape[1]
    tiles_q = nre // nq
    step = pl.program_id(0)

    @pl.when(step == 0)
    def _():
        st_ref[...] = st0_ref[...]

    def prep(x_ref):
        return _rms(x_ref[...], g1_ref[...]) * (1.0 + sc_ref[...]) + sh_ref[...]

    hs = (prep(xf_ref), prep(xr_ref))
    skip = hs[0] * dd_ref[...]
    outs = (yf_ref, yr_ref)
    bpl = 2 if nb % 2 == 0 else 1
    hbs = [hs[dr].reshape(nb * t, d).astype(BF16) for dr in range(2)]

    def slab(b, n):
        return (b * nt + n) * pitch

    def in_tile(dr, j, dst):
        kb = (j % (nj // 2)) // per_kb
        r = _dot(hbs[dr][:, kb * LANES:(kb + 1) * LANES], wb_ref[dr, j])
        for b in range(nb):
            for half in range(MXU_N // LANES):
                n = j * (MXU_N // LANES) + half
                dst[pl.ds(slab(b, n), t), :] = r[b * t:(b + 1) * t, half * LANES:(half + 1) * LANES]

    kparts = 2 * tiles_q // (MXU_N // LANES)

    def out_part(dr, q, kc, src, acc):
        rows = []
        for b in range(nb):
            pieces = []
            for i in range(MXU_N // LANES):
                idx = kc * (MXU_N // LANES) + i
                part, within = divmod(idx, tiles_q)
                pieces.append(src[pl.ds(slab(b, part * nre + q * tiles_q + within), t), :])
            rows.append(jnp.concatenate(pieces, axis=1))
        part_dot = _dot(jnp.concatenate(rows, axis=0).astype(BF16), wc_ref[dr, q, kc * MXU_N:(kc + 1) * MXU_N, :])
        return part_dot if acc is None else acc + part_dot

    def out_store(dr, q, yq):
        for b in range(nb):
            val = yq[b * t:(b + 1) * t, :]
            if dr == 0:
                val = val + skip[b, :, q * LANES:(q + 1) * LANES]
            outs[dr][b, :, q * LANES:(q + 1) * LANES] = val

    def scan_phase(dr, src, dst, side_work):
        consts = [(are_ref[dr, v * SUBLANES:(v + 1) * SUBLANES, :], aim_ref[dr, v * SUBLANES:(v + 1) * SUBLANES, :])
                  for v in range(nv)]
        groups = list(range(0, nb, bpl))
        total = len(groups) * t
        done = 0
        for gi, b0 in enumerate(groups):
            state = [st_ref[b0 + bb, dr, part * nre + v * SUBLANES:part * nre + (v + 1) * SUBLANES, :]
                     for bb in range(bpl) for v in range(nv) for part in range(2)]
            for i in range(t):
                tt = i if dr == 0 else t - 1 - i
                nxt = []
                for bb in range(bpl):
                    for v in range(nv):
                        sre, sim = state[(bb * nv + v) * 2], state[(bb * nv + v) * 2 + 1]
                        are, aim = consts[v]
                        row_re = slab(b0 + bb, v * SUBLANES) + tt
                        row_im = slab(b0 + bb, nre + v * SUBLANES) + tt
                        bre = src[pl.ds(row_re, SUBLANES, stride=pitch), :]
                        bim = src[pl.ds(row_im, SUBLANES, stride=pitch), :]
                        new_re = are * sre - aim * sim + bre
                        new_im = are * sim + aim * sre + bim
                        dst[pl.ds(row_re, SUBLANES, stride=pitch), :] = new_re
                        dst[pl.ds(row_im, SUBLANES, stride=pitch), :] = new_im
                        nxt += [new_re, new_im]
                state = nxt
                step_no = gi * t + i + 1
                while done < len(side_work) and done * total < step_no * len(side_work):
                    side_work[done]()
                    done += 1
            k = 0
            for bb in range(bpl):
                for v in range(nv):
                    for part in range(2):
                        st_ref[b0 + bb, dr, part * nre + v * SUBLANES:part * nre + (v + 1) * SUBLANES, :] = state[k]
                        k += 1
        while done < len(side_work):
            side_work[done]()
            done += 1

    for j in range(nj):
        in_tile(0, j, in_s)
    scan_phase(0, in_s, out_s, [functools.partial(in_tile, 1, j, in2_s) for j in range(nj)])

    acc0 = {}

    def fwd_out(q, kc):
        acc0[q] = out_part(0, q, kc, out_s, acc0.get(q))
        if kc == kparts - 1:
            out_store(0, q, acc0.pop(q))

    scan_phase(1, in2_s, in_s, [functools.partial(fwd_out, q, kc) for q in range(nq) for kc in range(kparts)])
    for q in range(nq):
        yq = None
        for kc in range(kparts):
            yq = out_part(1, q, kc, in_s, yq)
        out_store(1, q, yq)


def _s5_scan(x3, mod3, g1, dd, wb, wc, are, aim, st0, t):
    nb, l, d = x3.shape
    nc = l // t
    pitch = t + SUBLANES // 2
    nt = st0.shape[2]
    full = lambda a: pl.BlockSpec(a.shape, lambda c: (0,) * a.ndim)
    buf = pltpu.VMEM((nb * nt * pitch, LANES), F32)
    return pl.pallas_call(
        functools.partial(_s5_kernel, pitch=pitch),
        grid=(nc,),
        in_specs=[
            pl.BlockSpec((nb, t, d), lambda c: (0, c, 0)),
            pl.BlockSpec((nb, t, d), lambda c: (0, nc - 1 - c, 0)),
            pl.BlockSpec((nb, 1, d), lambda c: (0, 0, 0)),
            pl.BlockSpec((nb, 1, d), lambda c: (0, 0, 1)),
            full(g1), full(dd), full(wb), full(wc), full(are), full(aim), full(st0),
        ],
        out_specs=[
            pl.BlockSpec((nb, t, d), lambda c: (0, c, 0)),
            pl.BlockSpec((nb, t, d), lambda c: (0, nc - 1 - c, 0)),
            full(st0),
        ],
        out_shape=[
            jax.ShapeDtypeStruct((nb, l, d), F32),
            jax.ShapeDtypeStruct((nb, l, d), F32),
            jax.ShapeDtypeStruct(st0.shape, F32),
        ],
        scratch_shapes=[buf, buf, buf],
        compiler_params=_cparams(("arbitrary",)),
        name="s5_scan",
    )(x3, x3, mod3, mod3, g1, dd, wb, wc, are, aim, st0)


def _s5_glu_kernel(yf_ref, yr_ref, w_ref, o_ref):
    d = o_ref.shape[-1]
    z = _dot(_gelu(yf_ref[...] + yr_ref[...]).astype(BF16), w_ref[...])
    o_ref[...] = z[:, :d] * _sigmoid_tanh(z[:, d:])


def _s5_glu(yf, yr, w, lseg):
    n, d = yf.shape
    tm = _pick_tile(lseg)
    return pl.pallas_call(
        _s5_glu_kernel,
        grid=(n // tm,),
        in_specs=[pl.BlockSpec((tm, d), lambda i: (i, 0)), pl.BlockSpec((tm, d), lambda i: (i, 0)),
                  pl.BlockSpec(w.shape, lambda i: (0, 0))],
        out_specs=pl.BlockSpec((tm, d), lambda i: (i, 0)),
        out_shape=jax.ShapeDtypeStruct((n, d), F32),
        compiler_params=_cparams(("parallel",)),
        name="s5_glu",
    )(yf, yr, w)


def _s5_weights(a_re, a_im, log_dt, b_re, b_im, c_re, c_im):
    ng, ns = a_re.shape[1:]
    gi = b_re.shape[-1]
    d = ng * gi
    lr, li = a_re.astype(F32), a_im.astype(F32)
    dt = jnp.exp(log_dt.astype(F32))[..., None]
    mag = jnp.exp(lr * dt)
    ar, ai = mag * jnp.cos(li * dt), mag * jnp.sin(li * dt)
    den = lr * lr + li * li
    wr = ((ar - 1.0) * lr + ai * li) / den
    wi = (ai * lr - (ar - 1.0) * li) / den
    br, bi = b_re.astype(F32), b_im.astype(F32)
    bb_re = wr[..., None] * br - wi[..., None] * bi
    bb_im = wr[..., None] * bi + wi[..., None] * br

    gpt = MXU_N // ns
    gpk = LANES // gi
    ncol = ng // gpt
    sel = (jnp.arange(gpk)[None, :, None]
           == gpt * (jnp.arange(ncol)[:, None, None] % (gpk // gpt)) + jnp.arange(gpt)[None, None, :]).astype(F32)

    def in_tiles(part):
        blk = part.reshape(2, ncol, gpt, ns, gi).transpose(0, 1, 4, 2, 3)
        tiles = blk[:, :, None, :, :, :] * sel[None, :, :, None, :, None]
        return tiles.reshape(2, ncol, gpk * gi, gpt * ns)

    wb = jnp.concatenate([in_tiles(bb_re), in_tiles(bb_im)], axis=1).astype(BF16)

    nq = d // LANES
    gpq = ng // nq
    eye = jnp.eye(gpq, dtype=F32)

    def out_tiles(part):
        blk = jnp.swapaxes(part.astype(F32), -1, -2).reshape(2, nq, gpq, ns, gi)
        tiles = blk[:, :, :, :, None, :] * eye[None, None, :, None, :, None]
        return tiles.reshape(2, nq, gpq * ns, gpq * gi)

    wc = jnp.concatenate([out_tiles(c_re), out_tiles(-c_im)], axis=2).astype(BF16)
    are = ar.reshape(2, ng * ns // LANES, LANES)
    aim = ai.reshape(2, ng * ns // LANES, LANES)
    return wb, wc, are, aim


def _s5_layer(lat, cx, mod_l, mod_c, nb, g1, a_re, a_im, log_dt, b_re, b_im, c_re, c_im, dvec, w_glu, need_ctx):
    d = lat.shape[1]
    l = lat.shape[0] // nb
    lc = cx.shape[0] // nb
    wb, wc, are, aim = _s5_weights(a_re, a_im, log_dt, b_re, b_im, c_re, c_im)
    t = _pick_tile(l, lc, cap=64)
    st0 = jnp.zeros((nb, 2, 2 * are.shape[1], LANES), F32)
    dd = dvec[None, :].astype(F32)
    ycf, ycr, st = _s5_scan(cx.reshape(nb, lc, d), mod_c, g1, dd, wb, wc, are, aim, st0, t)
    yf, yr, _ = _s5_scan(lat.reshape(nb, l, d), mod_l, g1, dd, wb, wc, are, aim, st, t)
    wg = w_glu.astype(BF16)
    y = _s5_glu(yf.reshape(nb * l, d), yr.reshape(nb * l, d), wg, l)
    yc = _s5_glu(ycf.reshape(nb * lc, d), ycr.reshape(nb * lc, d), wg, lc) if need_ctx else None
    return y, yc


def _lru_in_kernel(x_ref, xp_ref, xn_ref, sh_ref, sc_ref, g1_ref, w_ref, cw_ref, cb_ref,
                   xr_ref, gl_ref, ext_s, *, lseg):
    tm, wd = xr_ref.shape
    halo = xp_ref.shape[0]
    row0 = pl.program_id(0) * tm
    at_start = (row0 % lseg) == 0
    at_end = ((row0 + tm) % lseg) == 0

    xe = jnp.concatenate([xp_ref[...], x_ref[...], xn_ref[...]], axis=0)
    he = (_rms(xe, g1_ref[...]) * (1.0 + sc_ref[...]) + sh_ref[...]).astype(BF16)
    hm = he[halo:halo + tm]
    rows = min(tm, LANES)
    win = rows + 2 * halo
    cblk = MXU_N if wd % MXU_N == 0 else LANES
    for cb0 in range(0, wd, cblk):
        gl_ref[:, cb0:cb0 + cblk] = _gelu(_dot(hm, w_ref[:, cb0:cb0 + cblk])).astype(gl_ref.dtype)
        zx = _dot(he, w_ref[:, wd + cb0:wd + cb0 + cblk])
        ext_s[0:halo, cb0:cb0 + cblk] = jnp.where(at_start, 0.0, zx[0:halo])
        ext_s[halo:halo + tm, cb0:cb0 + cblk] = zx[halo:halo + tm]
        ext_s[halo + tm:2 * halo + tm, cb0:cb0 + cblk] = jnp.where(at_end, 0.0, zx[halo + tm:])
        for c0 in range(cb0, cb0 + cblk, LANES):
            for r0 in range(0, tm, rows):
                xw = ext_s[r0:r0 + win, c0:c0 + LANES]
                acc = cb_ref[:, c0:c0 + LANES] + cw_ref[1:2, c0:c0 + LANES] * xw[halo:halo + rows]
                for tap in (0, 2, 3):
                    shifted = pltpu.roll(xw, (1 - tap) % win, 0)[halo:halo + rows]
                    acc = acc + cw_ref[tap:tap + 1, c0:c0 + LANES] * shifted
                xr_ref[r0:r0 + rows, c0:c0 + LANES] = acc


def _lru_in(x, mod, lseg, g1, w_in, cw, cb):
    n, d = x.shape
    wd = w_in.shape[1] // 2
    tm = _pick_tile(lseg)
    hb = tm // SUBLANES
    nhb = n // SUBLANES
    full = lambda a: pl.BlockSpec(a.shape, lambda i: (0,) * a.ndim)
    return pl.pallas_call(
        functools.partial(_lru_in_kernel, lseg=lseg),
        grid=(n // tm,),
        in_specs=[pl.BlockSpec((tm, d), lambda i: (i, 0)),
                  pl.BlockSpec((SUBLANES, d), lambda i: (jnp.maximum(i * hb - 1, 0), 0)),
                  pl.BlockSpec((SUBLANES, d), lambda i: (jnp.minimum((i + 1) * hb, nhb - 1), 0)),
                  _mod_spec(0, tm, lseg, d), _mod_spec(1, tm, lseg, d),
                  full(g1), full(w_in), full(cw), full(cb)],
        out_specs=[pl.BlockSpec((tm, wd), lambda i: (i, 0)), pl.BlockSpec((tm, wd), lambda i: (i, 0))],
        out_shape=[jax.ShapeDtypeStruct((n, wd), F32), jax.ShapeDtypeStruct((n, wd), BF16)],
        scratch_shapes=[pltpu.VMEM((tm + 2 * SUBLANES, wd), F32)],
        compiler_params=_cparams(("parallel",)),
        name="lru_in",
    )(x, x, x, mod, mod, g1, w_in, cw, cb)


def _lru_scan_kernel(xf_ref, xr_ref, wg_ref, bg_ref, sp_ref, st0_ref,
                     hf_ref, hr_ref, st_ref, a_s, b_s, h_s, *, pitch):
    nb, t, wd = xf_ref.shape
    nblk = wd // LANES
    step = pl.program_id(0)

    @pl.when(step == 0)
    def _():
        st_ref[...] = st0_ref[...]

    def coeffs(dr, x_ref):
        for n in range(nblk):
            xb = x_ref[:, :, n * LANES:(n + 1) * LANES].reshape(nb * t, LANES)
            g = _dot(xb.astype(BF16), wg_ref[dr, n]) + bg_ref[dr, n]
            c1 = sp_ref[dr, n]
            a = jnp.exp2(c1 * jnp.tanh(g[:, :LANES]) + c1)
            bc = jnp.sqrt(1.0 - a * a) * ((0.5 * xb) * (jnp.tanh(g[:, LANES:]) + 1.0))
            for b in range(nb):
                a_s[dr, n, pl.ds(b * pitch, t), :] = a[b * t:(b + 1) * t]
                b_s[dr, n, pl.ds(b * pitch, t), :] = bc[b * t:(b + 1) * t]

    coeffs(0, xf_ref)
    coeffs(1, xr_ref)

    init = tuple(st_ref[dr, n] for dr in range(2) for n in range(nblk))

    def body(i, carry):
        times, carry = carry[:2], carry[2:]
        out = [times[0] + 1, times[1] - 1]
        for dr in range(2):
            tt = times[dr]
            for n in range(nblk):
                a = a_s[dr, n, pl.ds(tt, nb, stride=pitch), :]
                bc = b_s[dr, n, pl.ds(tt, nb, stride=pitch), :]
                hnew = a * carry[dr * nblk + n] + bc
                h_s[dr, n, pl.ds(tt, nb, stride=pitch), :] = hnew
                out.append(hnew)
        return tuple(out)

    fin = lax.fori_loop(0, t, body, (jnp.int32(0), jnp.int32(t - 1)) + init, unroll=2)[2:]
    k = 0
    for dr in range(2):
        for n in range(nblk):
            st_ref[dr, n] = fin[k]
            k += 1
    outs = (hf_ref, hr_ref)
    for dr in range(2):
        for n in range(nblk):
            for b in range(nb):
                outs[dr][b, :, n * LANES:(n + 1) * LANES] = h_s[dr, n, pl.ds(b * pitch, t), :].astype(outs[dr].dtype)


def _lru_scan(xr3, wg, bg, sp, st0, t):
    nb, l, wd = xr3.shape
    nc = l // t
    nblk = wd // LANES
    pitch = t + SUBLANES
    full = lambda a: pl.BlockSpec(a.shape, lambda c: (0,) * a.ndim)
    chunk = lambda f: pl.BlockSpec((nb, t, wd), lambda c: (0, f(c), 0))
    fwd = lambda c: c
    rev = lambda c: nc - 1 - c
    coef = pltpu.VMEM((2, nblk, nb * pitch, LANES), F32)
    return pl.pallas_call(
        functools.partial(_lru_scan_kernel, pitch=pitch),
        grid=(nc,),
        in_specs=[chunk(fwd), chunk(rev), full(wg), full(bg), full(sp), full(st0)],
        out_specs=[chunk(fwd), chunk(rev), full(st0)],
        out_shape=[jax.ShapeDtypeStruct((nb, l, wd), BF16), jax.ShapeDtypeStruct((nb, l, wd), BF16),
                   jax.ShapeDtypeStruct(st0.shape, F32)],
        scratch_shapes=[coef, coef, coef],
        compiler_params=_cparams(("arbitrary",)),
        name="lru_scan",
    )(xr3, xr3, wg, bg, sp, st0)


def _lru_out_kernel(gl_ref, hf_ref, hr_ref, w_ref, o_ref):
    u = gl_ref[...].astype(F32) * (hf_ref[...].astype(F32) + hr_ref[...].astype(F32))
    o_ref[...] = _dot(u.astype(BF16), w_ref[...])


def _lru_out(gl, hf, hr, w, lseg):
    n, wd = gl.shape
    d = w.shape[1]
    tm = _pick_tile(lseg)
    row = lambda: pl.BlockSpec((tm, wd), lambda i: (i, 0))
    return pl.pallas_call(
        _lru_out_kernel,
        grid=(n // tm,),
        in_specs=[row(), row(), row(), pl.BlockSpec(w.shape, lambda i: (0, 0))],
        out_specs=pl.BlockSpec((tm, d), lambda i: (i, 0)),
        out_shape=jax.ShapeDtypeStruct((n, d), F32),
        compiler_params=_cparams(("parallel",)),
        name="lru_out",
    )(gl, hf, hr, w)


def _lru_layer(lat, cx, mod_l, mod_c, nb, g1, w_in, conv_w, conv_b, w_gate, b_gate, lam, w_out, need_ctx):
    l = lat.shape[0] // nb
    lc = cx.shape[0] // nb
    wd = w_in.shape[1] // 2
    nblk = wd // LANES
    w_in_b = w_in.astype(BF16)
    cw = conv_w.astype(F32)
    cb = conv_b.astype(F32)[None, :]
    xw, gl = _lru_in(lat, mod_l, l, g1, w_in_b, cw, cb)
    xwc, glc = _lru_in(cx, mod_c, lc, g1, w_in_b, cw, cb)
    wg = (0.5 * jnp.concatenate([w_gate[:, 0], w_gate[:, 1]], axis=-1)).astype(BF16)
    bgate = b_gate.astype(F32).reshape(2, 2, nblk, 1, LANES)
    bg = 0.5 * jnp.concatenate([bgate[:, 0], bgate[:, 1]], axis=-1)
    sp = (-0.5 * LRU_C * math.log2(math.e)) * jax.nn.softplus(-lam.astype(F32)).reshape(2, nblk, 1, LANES)
    t = _pick_tile(l, lc, cap=64)
    st0 = jnp.zeros((2, nblk, nb, LANES), F32)
    hcf, hcr, st = _lru_scan(xwc.reshape(nb, lc, wd), wg, bg, sp, st0, t)
    hf, hr, _ = _lru_scan(xw.reshape(nb, l, wd), wg, bg, sp, st, t)
    wo = w_out.astype(BF16)
    y = _lru_out(gl, hf.reshape(nb * l, wd), hr.reshape(nb * l, wd), wo, l)
    yc = _lru_out(glc, hcf.reshape(nb * lc, wd), hcr.reshape(nb * lc, wd), wo, lc) if need_ctx else None
    return y, yc


def kernel(x, c, ctx, c_ctx, ada_w, ada_b, norm_g, mla_w_dq, mla_g_q, mla_w_uq, mla_w_dkv, mla_g_kv, mla_w_ukv, mla_g_qk, mla_w_o, s5_a_re, s5_a_im, s5_log_dt, s5_b_re, s5_b_im, s5_c_re, s5_c_im, s5_d, s5_w_glu, lru_w_in, lru_conv_w, lru_conv_b, lru_w_gate, lru_b_gate, lru_lambda, lru_w_out, mlp_w1, mlp_w2):
    nb, l, d = x.shape
    lc = ctx.shape[1]
    depth = ada_w.shape[0]
    assert nb < SUBLANES
    cc = jnp.concatenate([c, c_ctx[None, :], jnp.zeros((SUBLANES - nb - 1, d), F32)], axis=0)
    mod = _ada_mod(cc, ada_w, ada_b)
    lat = x.reshape(nb * l, d)
    cx = ctx.reshape(nb * lc, d)
    for i in range(depth):
        need_ctx = i < depth - 1
        mod_l = mod[i, :nb, None, :]
        mod_c = jnp.broadcast_to(mod[i, nb][None, None, :], (nb, 1, mod.shape[-1]))
        g1 = norm_g[i, 0][None, :]
        g2 = norm_g[i, 1][None, :]
        kind, j = i % N_MIXERS, i // N_MIXERS
        if kind == 0:
            y, yc = _mla_layer(lat, cx, mod_l, mod_c, nb, g1,
                               (mla_w_dq[j], mla_g_q[j], mla_w_uq[j], mla_w_dkv[j], mla_g_kv[j], mla_w_ukv[j],
                                mla_g_qk[j], mla_w_o[j]), need_ctx)
        elif kind == 1:
            y, yc = _s5_layer(lat, cx, mod_l, mod_c, nb, g1, s5_a_re[j], s5_a_im[j], s5_log_dt[j], s5_b_re[j],
                              s5_b_im[j], s5_c_re[j], s5_c_im[j], s5_d[j], s5_w_glu[j], need_ctx)
        else:
            y, yc = _lru_layer(lat, cx, mod_l, mod_c, nb, g1, lru_w_in[j], lru_conv_w[j], lru_conv_b[j],
                               lru_w_gate[j], lru_b_gate[j], lru_lambda[j], lru_w_out[j], need_ctx)
        wo = mla_w_o[j] if kind == 0 else None
        lat = _mlp(lat, y, mod_l, l, g2, mlp_w1, mlp_w2, i, wo)
        if need_ctx:
            cx = _mlp(cx, yc, mod_c, lc, g2, mlp_w1, mlp_w2, i, wo)
    return lat.reshape(nb, l, d)
```

```python
import functools
import math

import jax
import jax.numpy as jnp
import numpy as np
from jax import lax
from jax.experimental import pallas as pl
from jax.experimental.pallas import tpu as pltpu

F32 = jnp.float32
BF16 = jnp.bfloat16

EPS = 1e-6
ROPE_THETA = 10000.0
GRID_W = 64
N_MIXERS = 3

MLA_HEADS = 8
MLA_NOPE = 128
MLA_ROPE = 64
MLA_V = 128
MLA_QK = MLA_NOPE + MLA_ROPE
MLA_SCALE = math.log2(math.e) / math.sqrt(MLA_QK)

S5_GROUP = 16
S5_STATE = 64
LRU_BLOCKS = 10
LRU_BW = 128
LRU_C = 8.0

LANES = 128
SUBLANES = 8
MXU_N = 256
VMEM_LIMIT = 56 * 1024 * 1024


def _cparams(sem):
    return pltpu.CompilerParams(dimension_semantics=sem, vmem_limit_bytes=VMEM_LIMIT)


def _dot(a, b):
    return jnp.dot(a, b, preferred_element_type=F32)


def _dot_nt(a, b):
    return lax.dot_general(a, b, (((1,), (1,)), ((), ())), preferred_element_type=F32)


def _rms(x, g):
    inv = lax.rsqrt(jnp.mean(x * x, axis=-1, keepdims=True) + EPS)
    return x * inv * g


def _gelu(x):
    c = math.sqrt(2.0 / math.pi)
    return x * (0.5 * (1.0 + jnp.tanh(c * (x + 0.044715 * (x * x * x)))))


def _sigmoid(x):
    return 1.0 / (1.0 + jnp.exp(-x))


def _sigmoid_tanh(x):
    return 0.5 * jnp.tanh(0.5 * x) + 0.5


def _pick_tile(*lens, cap=512):
    t = cap
    while any(n % t for n in lens):
        t //= 2
    assert t >= SUBLANES
    return t


def _ada_kernel(c_ref, w_ref, b_ref, o_ref):
    c = c_ref[...]
    s = c * _sigmoid(c)
    w = w_ref[...]
    s_hi = s.astype(BF16)
    s_lo = (s - s_hi.astype(F32)).astype(BF16)
    w_hi = w.astype(BF16)
    w_lo = (w - w_hi.astype(F32)).astype(BF16)
    o_ref[...] = _dot(s_hi, w_hi) + _dot(s_lo, w_hi) + _dot(s_hi, w_lo) + b_ref[...]


def _ada_mod(cc, ada_w, ada_b):
    depth, d, d6 = ada_w.shape
    nk = d6 // d
    return pl.pallas_call(
        _ada_kernel,
        grid=(depth, nk),
        in_specs=[
            pl.BlockSpec((SUBLANES, d), lambda l, k: (0, 0)),
            pl.BlockSpec((None, d, d), lambda l, k: (l, 0, k)),
            pl.BlockSpec((None, 1, d), lambda l, k: (l, 0, k)),
        ],
        out_specs=pl.BlockSpec((None, SUBLANES, d), lambda l, k: (l, 0, k)),
        out_shape=jax.ShapeDtypeStruct((depth, SUBLANES, d6), F32),
        compiler_params=_cparams(("parallel", "parallel")),
        name="ada_mod",
    )(cc, ada_w, ada_b.reshape(depth, 1, d6))


def _mod_spec(k, tm, lseg, d, tile_of=lambda i: i):
    return pl.BlockSpec((None, 1, d), lambda i, *_: ((tile_of(i) * tm) // lseg, 0, k))


def _mlp_kernel(*refs, has_wo):
    if has_wo:
        (x_ref, y_ref, ga_ref, shm_ref, scm_ref, gm_ref, g2_ref, w1_ref, w2_ref, wo_ref,
         o_ref, w1_s, w2_s, wo_s) = refs
    else:
        x_ref, y_ref, ga_ref, shm_ref, scm_ref, gm_ref, g2_ref, w1_ref, w2_ref, o_ref, w1_s, w2_s = refs
    step = pl.program_id(0)
    nj = w1_s.shape[0]

    @pl.when(step < nj)
    def _():
        w1_s[step] = w1_ref[...].astype(BF16)
        w2_s[step] = w2_ref[...].astype(BF16)

    if has_wo:
        nk, kc = wo_s.shape[:2]

        @pl.when(step < nk)
        def _():
            wo_s[step] = wo_ref[...].astype(BF16)

    @pl.when(step >= nj)
    def _():
        if has_wo:
            yb = y_ref[...].astype(BF16)
            y = _dot(yb[:, :kc], wo_s[0])
            for c in range(1, nk):
                y = y + _dot(yb[:, c * kc:(c + 1) * kc], wo_s[c])
        else:
            y = y_ref[...].astype(F32)
        x1 = x_ref[...] + ga_ref[...] * y
        h = (_rms(x1, g2_ref[...]) * (1.0 + scm_ref[...]) + shm_ref[...]).astype(BF16)
        acc = jnp.zeros_like(x1)
        for c in range(nj):
            t = jnp.maximum(_dot(h, w1_s[c]), 0.0)
            acc = acc + _dot((t * t).astype(BF16), w2_s[c])
        o_ref[...] = x1 + gm_ref[...] * acc


def _mlp(x, y, mod, lseg, g2, w1_all, w2_all, layer, wo=None):
    n, d = x.shape
    dy = y.shape[1]
    dff = w1_all.shape[2]
    tm = _pick_tile(lseg)
    tf = min(dff, 512)
    nj = dff // tf
    tile = lambda s: jnp.maximum(s - nj, 0)
    wblk = lambda s: jnp.minimum(s, nj - 1)
    row = lambda width: pl.BlockSpec((tm, width), lambda s: (tile(s), 0))
    in_specs = [
        row(d), row(dy),
        _mod_spec(2, tm, lseg, d, tile),
        _mod_spec(3, tm, lseg, d, tile),
        _mod_spec(4, tm, lseg, d, tile),
        _mod_spec(5, tm, lseg, d, tile),
        pl.BlockSpec((1, d), lambda s: (0, 0)),
        pl.BlockSpec((None, d, tf), lambda s: (layer, 0, wblk(s))),
        pl.BlockSpec((None, tf, d), lambda s: (layer, wblk(s), 0)),
    ]
    scratch = [pltpu.VMEM((nj, d, tf), BF16), pltpu.VMEM((nj, tf, d), BF16)]
    args = [x, y, mod, mod, mod, mod, g2, w1_all, w2_all]
    if wo is not None:
        kc = min(dy, MXU_N)
        nk = dy // kc
        assert nk <= nj
        in_specs.append(pl.BlockSpec((kc, d), lambda s: (jnp.minimum(s, nk - 1), 0)))
        scratch.append(pltpu.VMEM((nk, kc, d), BF16))
        args.append(wo)
    return pl.pallas_call(
        functools.partial(_mlp_kernel, has_wo=wo is not None),
        grid=(nj + n // tm,),
        in_specs=in_specs,
        out_specs=row(d),
        out_shape=jax.ShapeDtypeStruct((n, d), F32),
        scratch_shapes=scratch,
        compiler_params=_cparams(("arbitrary",)),
        name="mlp",
    )(*args)


def _rope(x, cos, sin_signed):
    width = x.shape[-1]
    quarter = MLA_ROPE // 4
    lane = lax.broadcasted_iota(jnp.int32, x.shape, 1)
    even_quarter = ((lane // quarter) % 2) == 0
    rot = jnp.where(even_quarter, pltpu.roll(x, width - quarter, 1), pltpu.roll(x, quarter, 1))
    return x * cos + rot * sin_signed


def _mla_proj_kernel(x_ref, sh_ref, sc_ref, g1_ref, cos_ref, sin_ref,
                     wd_ref, gq_ref, wuq_ref, gkv_ref, wukv_ref,
                     gqn_ref, gqr_ref, gkn_ref, gkr_ref, seg_ref,
                     q_ref, k_ref, v_ref):
    nh = q_ref.shape[0]
    dn = nh * MLA_NOPE
    dr = nh * MLA_ROPE
    kvl = gkv_ref.shape[-1]
    h = (_rms(x_ref[...], g1_ref[...]) * (1.0 + sc_ref[...]) + sh_ref[...]).astype(BF16)
    qlora = gq_ref.shape[-1]
    down = _dot(h, wd_ref[...])
    ql = _rms(down[:, :qlora], gq_ref[...]).astype(BF16)
    q = _dot(ql, wuq_ref[...])
    kv = down[:, qlora:]
    ckv = _rms(kv[:, :kvl], gkv_ref[...]).astype(BF16)
    kvu = _dot(ckv, wukv_ref[...])

    cos = cos_ref[...]
    sin = sin_ref[...]
    reps = dr // LANES
    cos_q = jnp.concatenate([cos] * reps, axis=1)
    sin_q = jnp.concatenate([sin] * reps, axis=1)

    qr = q[:, dn:]
    sq = qr * qr
    sq_hi = sq.astype(BF16)
    sq_lo = (sq - sq_hi.astype(F32)).astype(BF16)
    ms = _dot(sq_hi, seg_ref[...]) + _dot(sq_lo, seg_ref[...])
    qr = qr * lax.rsqrt(ms + EPS) * gqr_ref[...]
    qr = _rope(qr, cos_q, sin_q) * MLA_SCALE

    kr = kv[:, kvl:]
    kr = _rms(kr, gkr_ref[...])
    kr = _rope(kr, cos, sin)[:, :MLA_ROPE]

    for hh in range(nh):
        qn = _rms(q[:, hh * MLA_NOPE:(hh + 1) * MLA_NOPE], gqn_ref[...]) * MLA_SCALE
        qrh = qr[:, hh * MLA_ROPE:(hh + 1) * MLA_ROPE]
        q_ref[hh] = jnp.concatenate([qn, qrh], axis=1).astype(q_ref.dtype)
        kn = _rms(kvu[:, hh * MLA_NOPE:(hh + 1) * MLA_NOPE], gkn_ref[...])
        k_ref[hh] = jnp.concatenate([kn, kr], axis=1).astype(k_ref.dtype)
        v_ref[hh, :, :MLA_V] = kvu[:, dn + hh * MLA_V:dn + (hh + 1) * MLA_V].astype(v_ref.dtype)
        v_ref[hh, :, MLA_V:] = jnp.ones((kvu.shape[0], MLA_V), v_ref.dtype)


def _mla_proj(x, mod, lseg, g1, cos_t, sin_t, w):
    n, d = x.shape
    tm = _pick_tile(lseg, cap=512)
    nrope = cos_t.shape[0] // tm
    full = lambda a: pl.BlockSpec(a.shape, lambda i: (0,) * a.ndim)
    weights = [w["wd"], w["gq"], w["wuq"], w["gkv"], w["wukv"],
               w["gqn"], w["gqr"], w["gkn"], w["gkr"], w["seg"]]
    return pl.pallas_call(
        _mla_proj_kernel,
        grid=(n // tm,),
        in_specs=[
            pl.BlockSpec((tm, d), lambda i: (i, 0)),
            _mod_spec(0, tm, lseg, d),
            _mod_spec(1, tm, lseg, d),
            pl.BlockSpec((1, d), lambda i: (0, 0)),
            pl.BlockSpec((tm, LANES), lambda i: (i % nrope, 0)),
            pl.BlockSpec((tm, LANES), lambda i: (i % nrope, 0)),
        ] + [full(a) for a in weights],
        out_specs=[
            pl.BlockSpec((MLA_HEADS, tm, MLA_QK), lambda i: (0, i, 0)),
            pl.BlockSpec((MLA_HEADS, tm, MLA_QK), lambda i: (0, i, 0)),
            pl.BlockSpec((MLA_HEADS, tm, 2 * MLA_V), lambda i: (0, i, 0)),
        ],
        out_shape=[
            jax.ShapeDtypeStruct((MLA_HEADS, n, MLA_QK), BF16),
            jax.ShapeDtypeStruct((MLA_HEADS, n, MLA_QK), BF16),
            jax.ShapeDtypeStruct((MLA_HEADS, n, 2 * MLA_V), BF16),
        ],
        compiler_params=_cparams(("parallel",)),
        name="mla_proj",
    )(x, mod, mod, g1, cos_t, sin_t, *weights)


def _attn_lat_kernel(q_ref, kc_ref, kl_ref, vc_ref, vl_ref, o_ref, sa_s, sb_s, ma_s, mb_s, *, tq, kb):
    heads, lq = q_ref.shape[:2]
    lc = kc_ref.shape[1]
    n_tiles = lq // tq
    blocks = [(kc_ref, vc_ref, 0, lc, 0)] + [(kl_ref, vl_ref, j * kb, kb, lc + j * kb)
                                             for j in range(kl_ref.shape[1] // kb)]
    slots = ((sa_s, ma_s), (sb_s, mb_s))

    def iteration(first, second, slot):
        s_w, m_w = slots[slot]
        s_r, m_r = slots[1 - slot]
        if first is not None:
            q = q_ref[first[0], pl.ds(pl.multiple_of(first[1] * tq, tq), tq), :]
            mp = jnp.full((tq, LANES), -jnp.inf, F32)
        if second is not None:
            m = m_r[...]
            acc = jnp.zeros((tq, 2 * MLA_V), F32)
        for k_ref, v_ref, start, size, off in blocks:
            if first is not None:
                s = _dot_nt(q, k_ref[first[0], start:start + size, :])
                s_w[:, off:off + size] = s
                for c in range(size // LANES):
                    mp = jnp.maximum(mp, s[:, c * LANES:(c + 1) * LANES])
            if second is not None:
                p = jnp.concatenate(
                    [jnp.exp2(s_r[:, off + c * LANES:off + (c + 1) * LANES] - m)
                     for c in range(size // LANES)], axis=1).astype(BF16)
                acc = acc + _dot(p, v_ref[second[0], start:start + size, :])
        if first is not None:
            m_w[...] = jnp.broadcast_to(jnp.max(mp, axis=-1, keepdims=True), (tq, LANES))
        if second is not None:
            rows = pl.ds(pl.multiple_of(second[1] * tq, tq), tq)
            o_ref[rows, second[0] * MLA_V:(second[0] + 1) * MLA_V] = (
                acc[:, :MLA_V] / acc[:, MLA_V:]).astype(o_ref.dtype)

    def in_head(hd, lo, hi):
        count = hi - lo
        par = (hd * n_tiles + lo) % 2

        def pair(j, carry):
            i = lo + 2 * j
            iteration((hd, i), (hd, i - 1), par)
            iteration((hd, i + 1), (hd, i), 1 - par)
            return carry

        lax.fori_loop(0, count // 2, pair, 0)
        if count % 2:
            iteration((hd, hi - 1), (hd, hi - 2), (hd * n_tiles + hi - 1) % 2)

    iteration((0, 0), None, 0)
    for hd in range(heads):
        if hd > 0:
            iteration((hd, 0), (hd - 1, n_tiles - 1), (hd * n_tiles) % 2)
        in_head(hd, 1, n_tiles)
    iteration(None, (heads - 1, n_tiles - 1), (heads * n_tiles) % 2)


def _attn_ctx_kernel(q_ref, kc_ref, vc_ref, o_ref):
    sc = _dot_nt(q_ref[...], kc_ref[...])
    m = jnp.max(sc, axis=-1, keepdims=True)
    o = _dot(jnp.exp2(sc - m).astype(BF16), vc_ref[...])
    o_ref[...] = (o[:, :MLA_V] / o[:, MLA_V:]).astype(o_ref.dtype)


def _attn_lat(q, kl, vl, kc, vc, nb):
    nh, n, _ = q.shape
    l = n // nb
    lc = kc.shape[1] // nb
    tq = _pick_tile(l, cap=512)
    kb = _pick_tile(l, cap=MXU_N)
    hp = 2 if nh % 2 == 0 else 1
    assert lc % LANES == 0
    return pl.pallas_call(
        functools.partial(_attn_lat_kernel, tq=tq, kb=kb),
        grid=(nb, nh // hp),
        in_specs=[
            pl.BlockSpec((hp, l, MLA_QK), lambda b, h: (h, b, 0)),
            pl.BlockSpec((hp, lc, MLA_QK), lambda b, h: (h, b, 0)),
            pl.BlockSpec((hp, l, MLA_QK), lambda b, h: (h, b, 0)),
            pl.BlockSpec((hp, lc, 2 * MLA_V), lambda b, h: (h, b, 0)),
            pl.BlockSpec((hp, l, 2 * MLA_V), lambda b, h: (h, b, 0)),
        ],
        out_specs=pl.BlockSpec((l, hp * MLA_V), lambda b, h: (b, h)),
        out_shape=jax.ShapeDtypeStruct((n, nh * MLA_V), BF16),
        scratch_shapes=[pltpu.VMEM((tq, lc + l), F32), pltpu.VMEM((tq, lc + l), F32),
                        pltpu.VMEM((tq, LANES), F32), pltpu.VMEM((tq, LANES), F32)],
        compiler_params=_cparams(("parallel", "parallel")),
        name="attn_lat",
    )(q, kc, kl, vc, vl)


def _attn_ctx(q, kc, vc, nb):
    nh, n, _ = q.shape
    lc = n // nb
    return pl.pallas_call(
        _attn_ctx_kernel,
        grid=(nb, nh),
        in_specs=[
            pl.BlockSpec((None, lc, MLA_QK), lambda b, h: (h, b, 0)),
            pl.BlockSpec((None, lc, MLA_QK), lambda b, h: (h, b, 0)),
            pl.BlockSpec((None, lc, 2 * MLA_V), lambda b, h: (h, b, 0)),
        ],
        out_specs=pl.BlockSpec((lc, MLA_V), lambda b, h: (b, h)),
        out_shape=jax.ShapeDtypeStruct((n, nh * MLA_V), BF16),
        compiler_params=_cparams(("parallel", "parallel")),
        name="attn_ctx",
    )(q, kc, vc)


def _rope_tables(n_tok):
    rows = n_tok // GRID_W
    row = np.repeat(np.arange(rows, dtype=np.float32), GRID_W)
    col = np.tile(np.arange(GRID_W, dtype=np.float32), rows)
    n_freq = MLA_ROPE // 4
    freqs = (np.float32(ROPE_THETA) ** (-np.arange(n_freq, dtype=np.float32) / np.float32(n_freq))).astype(np.float32)
    ang_r = row[:, None] * freqs[None, :]
    ang_c = col[:, None] * freqs[None, :]
    ang = np.concatenate([ang_r, ang_r, ang_c, ang_c], axis=-1).astype(np.float32)
    sign = np.tile(np.repeat(np.array([-1.0, 1.0], np.float32), n_freq), 2)
    cos = np.cos(ang).astype(np.float32)
    sin = (np.sin(ang) * sign).astype(np.float32)
    return jnp.asarray(np.concatenate([cos, cos], axis=1)), jnp.asarray(np.concatenate([sin, sin], axis=1))


def _mla_weights(w_dq, g_q, w_uq, w_dkv, g_kv, w_ukv, g_qk, w_o):
    nh = MLA_HEADS
    qlora = w_dq.shape[1]
    kvl = g_kv.shape[0]
    uq = w_uq.reshape(qlora, nh, MLA_QK)
    wuq = jnp.concatenate([uq[:, :, :MLA_NOPE].reshape(qlora, nh * MLA_NOPE),
                           uq[:, :, MLA_NOPE:].reshape(qlora, nh * MLA_ROPE)], axis=1)
    ukv = w_ukv.reshape(kvl, nh, MLA_NOPE + MLA_V)
    wukv = jnp.concatenate([ukv[:, :, :MLA_NOPE].reshape(kvl, nh * MLA_NOPE),
                            ukv[:, :, MLA_NOPE:].reshape(kvl, nh * MLA_V)], axis=1)
    wdkv = jnp.concatenate([w_dkv, w_dkv[:, kvl:]], axis=1)
    dr = nh * MLA_ROPE
    seg_id = jnp.arange(dr) // MLA_ROPE
    seg = (seg_id[:, None] == seg_id[None, :]).astype(F32) / MLA_ROPE
    return {
        "wd": jnp.concatenate([w_dq, wdkv], axis=1).astype(BF16), "gq": g_q[None, :], "wuq": wuq.astype(BF16),
        "gkv": g_kv[None, :], "wukv": wukv.astype(BF16),
        "gqn": g_qk[0:1, :MLA_NOPE], "gqr": jnp.tile(g_qk[0:1, MLA_NOPE:], (1, nh)),
        "gkn": g_qk[1:2, :MLA_NOPE], "gkr": jnp.tile(g_qk[1:2, MLA_NOPE:], (1, 2)),
        "seg": seg.astype(BF16),
    }


def _mla_layer(lat, cx, mod_l, mod_c, nb, g1, wts, need_ctx):
    l = lat.shape[0] // nb
    lc = cx.shape[0] // nb
    w = _mla_weights(*wts)
    cos_t, sin_t = _rope_tables(l)
    tmc = _pick_tile(lc, cap=512)
    ql, kl, vl = _mla_proj(lat, mod_l, l, g1, cos_t, sin_t, w)
    qc, kc, vc = _mla_proj(cx, mod_c, lc, g1, jnp.ones((tmc, LANES), F32), jnp.zeros((tmc, LANES), F32), w)
    y = _attn_lat(ql, kl, vl, kc, vc, nb)
    yc = _attn_ctx(qc, kc, vc, nb) if need_ctx else None
    return y, yc


def _s5_kernel(xf_ref, xr_ref, sh_ref, sc_ref, g1_ref, dd_ref, wb_ref, wc_ref, are_ref, aim_ref, st0_ref,
               yf_ref, yr_ref, st_ref, in_s, out_s, in2_s, *, pitch):
    nb, t, d = xf_ref.shape
    nt = st_ref.shape[2]
    nre = nt // 2
    nv = nre // SUBLANES
    nj = wb_ref.shape[1]
    kblocks = d // LANES
    per_kb = (nj // 2) // kblocks
    nq = wc_ref.shape[1]
    tiles_q = nre // nq
    step = pl.program_id(0)

    @pl.when(step == 0)
    def _():
        st_ref[...] = st0_ref[...]

    def prep(x_ref):
        return _rms(x_ref[...], g1_ref[...]) * (1.0 + sc_ref[...]) + sh_ref[...]

    hs = (prep(xf_ref), prep(xr_ref))
    skip = hs[0] * dd_ref[...]
    outs = (yf_ref, yr_ref)
    bpl = 2 if nb % 2 == 0 else 1
    hbs = [hs[dr].reshape(nb * t, d).astype(BF16) for dr in range(2)]

    def slab(b, n):
        return (b * nt + n) * pitch

    def in_tile(dr, j, dst):
        kb = (j % (nj // 2)) // per_kb
        r = _dot(hbs[dr][:, kb * LANES:(kb + 1) * LANES], wb_ref[dr, j])
        for b in range(nb):
            for half in range(MXU_N // LANES):
                n = j * (MXU_N // LANES) + half
                dst[pl.ds(slab(b, n), t), :] = r[b * t:(b + 1) * t, half * LANES:(half + 1) * LANES]

    kparts = 2 * tiles_q // (MXU_N // LANES)

    def out_part(dr, q, kc, src, acc):
        rows = []
        for b in range(nb):
            pieces = []
            for i in range(MXU_N // LANES):
                idx = kc * (MXU_N // LANES) + i
                part, within = divmod(idx, tiles_q)
                pieces.append(src[pl.ds(slab(b, part * nre + q * tiles_q + within), t), :])
            rows.append(jnp.concatenate(pieces, axis=1))
        part_dot = _dot(jnp.concatenate(rows, axis=0).astype(BF16), wc_ref[dr, q, kc * MXU_N:(kc + 1) * MXU_N, :])
        return part_dot if acc is None else acc + part_dot

    def out_store(dr, q, yq):
        for b in range(nb):
            val = yq[b * t:(b + 1) * t, :]
            if dr == 0:
                val = val + skip[b, :, q * LANES:(q + 1) * LANES]
            outs[dr][b, :, q * LANES:(q + 1) * LANES] = val

    def scan_phase(dr, src, dst, side_work):
        consts = [(are_ref[dr, v * SUBLANES:(v + 1) * SUBLANES, :], aim_ref[dr, v * SUBLANES:(v + 1) * SUBLANES, :])
                  for v in range(nv)]
        groups = list(range(0, nb, bpl))
        total = len(groups) * t
        done = 0
        for gi, b0 in enumerate(groups):
            state = [st_ref[b0 + bb, dr, part * nre + v * SUBLANES:part * nre + (v + 1) * SUBLANES, :]
                     for bb in range(bpl) for v in range(nv) for part in range(2)]
            for i in range(t):
                tt = i if dr == 0 else t - 1 - i
                nxt = []
                for bb in range(bpl):
                    for v in range(nv):
                        sre, sim = state[(bb * nv + v) * 2], state[(bb * nv + v) * 2 + 1]
                        are, aim = consts[v]
                        row_re = slab(b0 + bb, v * SUBLANES) + tt
                        row_im = slab(b0 + bb, nre + v * SUBLANES) + tt
                        bre = src[pl.ds(row_re, SUBLANES, stride=pitch), :]
                        bim = src[pl.ds(row_im, SUBLANES, stride=pitch), :]
                        new_re = are * sre - aim * sim + bre
                        new_im = are * sim + aim * sre + bim
                        dst[pl.ds(row_re, SUBLANES, stride=pitch), :] = new_re
                        dst[pl.ds(row_im, SUBLANES, stride=pitch), :] = new_im
                        nxt += [new_re, new_im]
                state = nxt
                step_no = gi * t + i + 1
                while done < len(side_work) and done * total < step_no * len(side_work):
                    side_work[done]()
                    done += 1
            k = 0
            for bb in range(bpl):
                for v in range(nv):
                    for part in range(2):
                        st_ref[b0 + bb, dr, part * nre + v * SUBLANES:part * nre + (v + 1) * SUBLANES, :] = state[k]
                        k += 1
        while done < len(side_work):
            side_work[done]()
            done += 1

    for j in range(nj):
        in_tile(0, j, in_s)
    scan_phase(0, in_s, out_s, [functools.partial(in_tile, 1, j, in2_s) for j in range(nj)])

    acc0 = {}

    def fwd_out(q, kc):
        acc0[q] = out_part(0, q, kc, out_s, acc0.get(q))
        if kc == kparts - 1:
            out_store(0, q, acc0.pop(q))

    scan_phase(1, in2_s, in_s, [functools.partial(fwd_out, q, kc) for q in range(nq) for kc in range(kparts)])
    for q in range(nq):
        yq = None
        for kc in range(kparts):
            yq = out_part(1, q, kc, in_s, yq)
        out_store(1, q, yq)


def _s5_scan(x3, mod3, g1, dd, wb, wc, are, aim, st0, t):
    nb, l, d = x3.shape
    nc = l // t
    pitch = t + SUBLANES // 2
    nt = st0.shape[2]
    full = lambda a: pl.BlockSpec(a.shape, lambda c: (0,) * a.ndim)
    buf = pltpu.VMEM((nb * nt * pitch, LANES), F32)
    return pl.pallas_call(
        functools.partial(_s5_kernel, pitch=pitch),
        grid=(nc,),
        in_specs=[
            pl.BlockSpec((nb, t, d), lambda c: (0, c, 0)),
            pl.BlockSpec((nb, t, d), lambda c: (0, nc - 1 - c, 0)),
            pl.BlockSpec((nb, 1, d), lambda c: (0, 0, 0)),
            pl.BlockSpec((nb, 1, d), lambda c: (0, 0, 1)),
            full(g1), full(dd), full(wb), full(wc), full(are), full(aim), full(st0),
        ],
        out_specs=[
            pl.BlockSpec((nb, t, d), lambda c: (0, c, 0)),
            pl.BlockSpec((nb, t, d), lambda c: (0, nc - 1 - c, 0)),
            full(st0),
        ],
        out_shape=[
            jax.ShapeDtypeStruct((nb, l, d), F32),
            jax.ShapeDtypeStruct((nb, l, d), F32),
            jax.ShapeDtypeStruct(st0.shape, F32),
        ],
        scratch_shapes=[buf, buf, buf],
        compiler_params=_cparams(("arbitrary",)),
        name="s5_scan",
    )(x3, x3, mod3, mod3, g1, dd, wb, wc, are, aim, st0)


def _s5_glu_kernel(yf_ref, yr_ref, w_ref, o_ref):
    d = o_ref.shape[-1]
    z = _dot(_gelu(yf_ref[...] + yr_ref[...]).astype(BF16), w_ref[...])
    o_ref[...] = z[:, :d] * _sigmoid_tanh(z[:, d:])


def _s5_glu(yf, yr, w, lseg):
    n, d = yf.shape
    tm = _pick_tile(lseg)
    return pl.pallas_call(
        _s5_glu_kernel,
        grid=(n // tm,),
        in_specs=[pl.BlockSpec((tm, d), lambda i: (i, 0)), pl.BlockSpec((tm, d), lambda i: (i, 0)),
                  pl.BlockSpec(w.shape, lambda i: (0, 0))],
        out_specs=pl.BlockSpec((tm, d), lambda i: (i, 0)),
        out_shape=jax.ShapeDtypeStruct((n, d), F32),
        compiler_params=_cparams(("parallel",)),
        name="s5_glu",
    )(yf, yr, w)


def _s5_weights(a_re, a_im, log_dt, b_re, b_im, c_re, c_im):
    ng, ns = a_re.shape[1:]
    gi = b_re.shape[-1]
    d = ng * gi
    lr, li = a_re.astype(F32), a_im.astype(F32)
    dt = jnp.exp(log_dt.astype(F32))[..., None]
    mag = jnp.exp(lr * dt)
    ar, ai = mag * jnp.cos(li * dt), mag * jnp.sin(li * dt)
    den = lr * lr + li * li
    wr = ((ar - 1.0) * lr + ai * li) / den
    wi = (ai * lr - (ar - 1.0) * li) / den
    br, bi = b_re.astype(F32), b_im.astype(F32)
    bb_re = wr[..., None] * br - wi[..., None] * bi
    bb_im = wr[..., None] * bi + wi[..., None] * br

    gpt = MXU_N // ns
    gpk = LANES // gi
    ncol = ng // gpt
    sel = (jnp.arange(gpk)[None, :, None]
           == gpt * (jnp.arange(ncol)[:, None, None] % (gpk // gpt)) + jnp.arange(gpt)[None, None, :]).astype(F32)

    def in_tiles(part):
        blk = part.reshape(2, ncol, gpt, ns, gi).transpose(0, 1, 4, 2, 3)
        tiles = blk[:, :, None, :, :, :] * sel[None, :, :, None, :, None]
        return tiles.reshape(2, ncol, gpk * gi, gpt * ns)

    wb = jnp.concatenate([in_tiles(bb_re), in_tiles(bb_im)], axis=1).astype(BF16)

    nq = d // LANES
    gpq = ng // nq
    eye = jnp.eye(gpq, dtype=F32)

    def out_tiles(part):
        blk = jnp.swapaxes(part.astype(F32), -1, -2).reshape(2, nq, gpq, ns, gi)
        tiles = blk[:, :, :, :, None, :] * eye[None, None, :, None, :, None]
        return tiles.reshape(2, nq, gpq * ns, gpq * gi)

    wc = jnp.concatenate([out_tiles(c_re), out_tiles(-c_im)], axis=2).astype(BF16)
    are = ar.reshape(2, ng * ns // LANES, LANES)
    aim = ai.reshape(2, ng * ns // LANES, LANES)
    return wb, wc, are, aim


def _s5_layer(lat, cx, mod_l, mod_c, nb, g1, a_re, a_im, log_dt, b_re, b_im, c_re, c_im, dvec, w_glu, need_ctx):
    d = lat.shape[1]
    l = lat.shape[0] // nb
    lc = cx.shape[0] // nb
    wb, wc, are, aim = _s5_weights(a_re, a_im, log_dt, b_re, b_im, c_re, c_im)
    t = _pick_tile(l, lc, cap=64)
    st0 = jnp.zeros((nb, 2, 2 * are.shape[1], LANES), F32)
    dd = dvec[None, :].astype(F32)
    ycf, ycr, st = _s5_scan(cx.reshape(nb, lc, d), mod_c, g1, dd, wb, wc, are, aim, st0, t)
    yf, yr, _ = _s5_scan(lat.reshape(nb, l, d), mod_l, g1, dd, wb, wc, are, aim, st, t)
    wg = w_glu.astype(BF16)
    y = _s5_glu(yf.reshape(nb * l, d), yr.reshape(nb * l, d), wg, l)
    yc = _s5_glu(ycf.reshape(nb * lc, d), ycr.reshape(nb * lc, d), wg, lc) if need_ctx else None
    return y, yc


def _lru_in_kernel(x_ref, xp_ref, xn_ref, sh_ref, sc_ref, g1_ref, w_ref, cw_ref, cb_ref,
                   xr_ref, gl_ref, ext_s, *, lseg):
    tm, wd = xr_ref.shape
    halo = xp_ref.shape[0]
    row0 = pl.program_id(0) * tm
    at_start = (row0 % lseg) == 0
    at_end = ((row0 + tm) % lseg) == 0

    xe = jnp.concatenate([xp_ref[...], x_ref[...], xn_ref[...]], axis=0)
    he = (_rms(xe, g1_ref[...]) * (1.0 + sc_ref[...]) + sh_ref[...]).astype(BF16)
    hm = he[halo:halo + tm]
    rows = min(tm, LANES)
    win = rows + 2 * halo
    cblk = MXU_N if wd % MXU_N == 0 else LANES
    for cb0 in range(0, wd, cblk):
        gl_ref[:, cb0:cb0 + cblk] = _gelu(_dot(hm, w_ref[:, cb0:cb0 + cblk])).astype(gl_ref.dtype)
        zx = _dot(he, w_ref[:, wd + cb0:wd + cb0 + cblk])
        ext_s[0:halo, cb0:cb0 + cblk] = jnp.where(at_start, 0.0, zx[0:halo])
        ext_s[halo:halo + tm, cb0:cb0 + cblk] = zx[halo:halo + tm]
        ext_s[halo + tm:2 * halo + tm, cb0:cb0 + cblk] = jnp.where(at_end, 0.0, zx[halo + tm:])
        for c0 in range(cb0, cb0 + cblk, LANES):
            for r0 in range(0, tm, rows):
                xw = ext_s[r0:r0 + win, c0:c0 + LANES]
                acc = cb_ref[:, c0:c0 + LANES] + cw_ref[1:2, c0:c0 + LANES] * xw[halo:halo + rows]
                for tap in (0, 2, 3):
                    shifted = pltpu.roll(xw, (1 - tap) % win, 0)[halo:halo + rows]
                    acc = acc + cw_ref[tap:tap + 1, c0:c0 + LANES] * shifted
                xr_ref[r0:r0 + rows, c0:c0 + LANES] = acc


def _lru_in(x, mod, lseg, g1, w_in, cw, cb):
    n, d = x.shape
    wd = w_in.shape[1] // 2
    tm = _pick_tile(lseg)
    hb = tm // SUBLANES
    nhb = n // SUBLANES
    full = lambda a: pl.BlockSpec(a.shape, lambda i: (0,) * a.ndim)
    return pl.pallas_call(
        functools.partial(_lru_in_kernel, lseg=lseg),
        grid=(n // tm,),
        in_specs=[pl.BlockSpec((tm, d), lambda i: (i, 0)),
                  pl.BlockSpec((SUBLANES, d), lambda i: (jnp.maximum(i * hb - 1, 0), 0)),
                  pl.BlockSpec((SUBLANES, d), lambda i: (jnp.minimum((i + 1) * hb, nhb - 1), 0)),
                  _mod_spec(0, tm, lseg, d), _mod_spec(1, tm, lseg, d),
                  full(g1), full(w_in), full(cw), full(cb)],
        out_specs=[pl.BlockSpec((tm, wd), lambda i: (i, 0)), pl.BlockSpec((tm, wd), lambda i: (i, 0))],
        out_shape=[jax.ShapeDtypeStruct((n, wd), F32), jax.ShapeDtypeStruct((n, wd), BF16)],
        scratch_shapes=[pltpu.VMEM((tm + 2 * SUBLANES, wd), F32)],
        compiler_params=_cparams(("parallel",)),
        name="lru_in",
    )(x, x, x, mod, mod, g1, w_in, cw, cb)


def _lru_scan_kernel(xf_ref, xr_ref, wg_ref, bg_ref, sp_ref, st0_ref,
                     hf_ref, hr_ref, st_ref, a_s, b_s, h_s, *, pitch):
    nb, t, wd = xf_ref.shape
    nblk = wd // LANES
    step = pl.program_id(0)

    @pl.when(step == 0)
    def _():
        st_ref[...] = st0_ref[...]

    def coeffs(dr, x_ref):
        for n in range(nblk):
            xb = x_ref[:, :, n * LANES:(n + 1) * LANES].reshape(nb * t, LANES)
            g = _dot(xb.astype(BF16), wg_ref[dr, n]) + bg_ref[dr, n]
            c1 = sp_ref[dr, n]
            a = jnp.exp2(c1 * jnp.tanh(g[:, :LANES]) + c1)
            bc = jnp.sqrt(1.0 - a * a) * ((0.5 * xb) * (jnp.tanh(g[:, LANES:]) + 1.0))
            for b in range(nb):
                a_s[dr, n, pl.ds(b * pitch, t), :] = a[b * t:(b + 1) * t]
                b_s[dr, n, pl.ds(b * pitch, t), :] = bc[b * t:(b + 1) * t]

    coeffs(0, xf_ref)
    coeffs(1, xr_ref)

    init = tuple(st_ref[dr, n] for dr in range(2) for n in range(nblk))

    def body(i, carry):
        times, carry = carry[:2], carry[2:]
        out = [times[0] + 1, times[1] - 1]
        for dr in range(2):
            tt = times[dr]
            for n in range(nblk):
                a = a_s[dr, n, pl.ds(tt, nb, stride=pitch), :]
                bc = b_s[dr, n, pl.ds(tt, nb, stride=pitch), :]
                hnew = a * carry[dr * nblk + n] + bc
                h_s[dr, n, pl.ds(tt, nb, stride=pitch), :] = hnew
                out.append(hnew)
        return tuple(out)

    fin = lax.fori_loop(0, t, body, (jnp.int32(0), jnp.int32(t - 1)) + init, unroll=2)[2:]
    k = 0
    for dr in range(2):
        for n in range(nblk):
            st_ref[dr, n] = fin[k]
            k += 1
    outs = (hf_ref, hr_ref)
    for dr in range(2):
        for n in range(nblk):
            for b in range(nb):
                outs[dr][b, :, n * LANES:(n + 1) * LANES] = h_s[dr, n, pl.ds(b * pitch, t), :].astype(outs[dr].dtype)


def _lru_scan(xr3, wg, bg, sp, st0, t):
    nb, l, wd = xr3.shape
    nc = l // t
    nblk = wd // LANES
    pitch = t + SUBLANES
    full = lambda a: pl.BlockSpec(a.shape, lambda c: (0,) * a.ndim)
    chunk = lambda f: pl.BlockSpec((nb, t, wd), lambda c: (0, f(c), 0))
    fwd = lambda c: c
    rev = lambda c: nc - 1 - c
    coef = pltpu.VMEM((2, nblk, nb * pitch, LANES), F32)
    return pl.pallas_call(
        functools.partial(_lru_scan_kernel, pitch=pitch),
        grid=(nc,),
        in_specs=[chunk(fwd), chunk(rev), full(wg), full(bg), full(sp), full(st0)],
        out_specs=[chunk(fwd), chunk(rev), full(st0)],
        out_shape=[jax.ShapeDtypeStruct((nb, l, wd), BF16), jax.ShapeDtypeStruct((nb, l, wd), BF16),
                   jax.ShapeDtypeStruct(st0.shape, F32)],
        scratch_shapes=[coef, coef, coef],
        compiler_params=_cparams(("arbitrary",)),
        name="lru_scan",
    )(xr3, xr3, wg, bg, sp, st0)


def _lru_out_kernel(gl_ref, hf_ref, hr_ref, w_ref, o_ref):
    u = gl_ref[...].astype(F32) * (hf_ref[...].astype(F32) + hr_ref[...].astype(F32))
    o_ref[...] = _dot(u.astype(BF16), w_ref[...])


def _lru_out(gl, hf, hr, w, lseg):
    n, wd = gl.shape
    d = w.shape[1]
    tm = _pick_tile(lseg)
    row = lambda: pl.BlockSpec((tm, wd), lambda i: (i, 0))
    return pl.pallas_call(
        _lru_out_kernel,
        grid=(n // tm,),
        in_specs=[row(), row(), row(), pl.BlockSpec(w.shape, lambda i: (0, 0))],
        out_specs=pl.BlockSpec((tm, d), lambda i: (i, 0)),
        out_shape=jax.ShapeDtypeStruct((n, d), F32),
        compiler_params=_cparams(("parallel",)),
        name="lru_out",
    )(gl, hf, hr, w)


def _lru_layer(lat, cx, mod_l, mod_c, nb, g1, w_in, conv_w, conv_b, w_gate, b_gate, lam, w_out, need_ctx):
    l = lat.shape[0] // nb
    lc = cx.shape[0] // nb
    wd = w_in.shape[1] // 2
    nblk = wd // LANES
    w_in_b = w_in.astype(BF16)
    cw = conv_w.astype(F32)
    cb = conv_b.astype(F32)[None, :]
    xw, gl = _lru_in(lat, mod_l, l, g1, w_in_b, cw, cb)
    xwc, glc = _lru_in(cx, mod_c, lc, g1, w_in_b, cw, cb)
    wg = (0.5 * jnp.concatenate([w_gate[:, 0], w_gate[:, 1]], axis=-1)).astype(BF16)
    bgate = b_gate.astype(F32).reshape(2, 2, nblk, 1, LANES)
    bg = 0.5 * jnp.concatenate([bgate[:, 0], bgate[:, 1]], axis=-1)
    sp = (-0.5 * LRU_C * math.log2(math.e)) * jax.nn.softplus(-lam.astype(F32)).reshape(2, nblk, 1, LANES)
    t = _pick_tile(l, lc, cap=64)
    st0 = jnp.zeros((2, nblk, nb, LANES), F32)
    hcf, hcr, st = _lru_scan(xwc.reshape(nb, lc, wd), wg, bg, sp, st0, t)
    hf, hr, _ = _lru_scan(xw.reshape(nb, l, wd), wg, bg, sp, st, t)
    wo = w_out.astype(BF16)
    y = _lru_out(gl, hf.reshape(nb * l, wd), hr.reshape(nb * l, wd), wo, l)
    yc = _lru_out(glc, hcf.reshape(nb * lc, wd), hcr.reshape(nb * lc, wd), wo, lc) if need_ctx else None
    return y, yc


def kernel(x, c, ctx, c_ctx, ada_w, ada_b, norm_g, mla_w_dq, mla_g_q, mla_w_uq, mla_w_dkv, mla_g_kv, mla_w_ukv, mla_g_qk, mla_w_o, s5_a_re, s5_a_im, s5_log_dt, s5_b_re, s5_b_im, s5_c_re, s5_c_im, s5_d, s5_w_glu, lru_w_in, lru_conv_w, lru_conv_b, lru_w_gate, lru_b_gate, lru_lambda, lru_w_out, mlp_w1, mlp_w2):
    nb, l, d = x.shape
    lc = ctx.shape[1]
    depth = ada_w.shape[0]
    assert nb < SUBLANES
    cc = jnp.concatenate([c, c_ctx[None, :], jnp.zeros((SUBLANES - nb - 1, d), F32)], axis=0)
    mod = _ada_mod(cc, ada_w, ada_b)
    lat = x.reshape(nb * l, d)
    cx = ctx.reshape(nb * lc, d)
    for i in range(depth):
        need_ctx = i < depth - 1
        mod_l = mod[i, :nb, None, :]
        mod_c = jnp.broadcast_to(mod[i, nb][None, None, :], (nb, 1, mod.shape[-1]))
        g1 = norm_g[i, 0][None, :]
        g2 = norm_g[i, 1][None, :]
        kind, j = i % N_MIXERS, i // N_MIXERS
        if kind == 0:
            y, yc = _mla_layer(lat, cx, mod_l, mod_c, nb, g1,
                               (mla_w_dq[j], mla_g_q[j], mla_w_uq[j], mla_w_dkv[j], mla_g_kv[j], mla_w_ukv[j],
                                mla_g_qk[j], mla_w_o[j]), need_ctx)
        elif kind == 1:
            y, yc = _s5_layer(lat, cx, mod_l, mod_c, nb, g1, s5_a_re[j], s5_a_im[j], s5_log_dt[j], s5_b_re[j],
                              s5_b_im[j], s5_c_re[j], s5_c_im[j], s5_d[j], s5_w_glu[j], need_ctx)
        else:
            y, yc = _lru_layer(lat, cx, mod_l, mod_c, nb, g1, lru_w_in[j], lru_conv_w[j], lru_conv_b[j],
                               lru_w_gate[j], lru_b_gate[j], lru_lambda[j], lru_w_out[j], need_ctx)
        wo = mla_w_o[j] if kind == 0 else None
        lat = _mlp(lat, y, mod_l, l, g2, mlp_w1, mlp_w2, i, wo)
        if need_ctx:
            cx = _mlp(cx, yc, mod_c, lc, g2, mlp_w1, mlp_w2, i, wo)
    return lat.reshape(nb, l, d)
```
